```python
import math
import jax, jax.numpy as jnp
from jax import lax
import numpy as np

D_MODEL = 1024
BATCH = 16
SEQ = 4096
DEPTH = 4

D_PLE = 256
D_REC = 512
REC_BLOCKS = 8
REC_BLOCK = D_REC // REC_BLOCKS
CONV_W = 4
LRU_C = 8.0
N_HEADS = 8
HEAD_DIM = 64
N_KV = 2
GQA_R = N_HEADS // N_KV
D_ATT = N_HEADS * HEAD_DIM
KV_W = N_KV * HEAD_DIM
L_CMP = 32
STRIDE_CMP = 16
CMP_HIDDEN = 256
L_SEL = 64
N_SEL = 16
WINDOW = 512
Q_BLOCK = 64
N_GROUPS = 4
EXP_PER_GROUP = 8
N_EXPERTS = N_GROUPS * EXP_PER_GROUP
TOP_K_IN_GROUP = 2
D_EXPERT = 512
MOE_CHUNK = 256
D_IN = 2 * D_REC + D_ATT + 6 * KV_W + 3 * N_HEADS
EPS = 1e-6
NEG = -1e30
BIG = 1e30

kernel_name = "hybrid_rglru_nsa_hmoe_ple"


def rmsnorm(x, g):
    xf = x.astype(jnp.float32)
    y = xf * lax.rsqrt(jnp.mean(xf * xf, axis=-1, keepdims=True) + EPS)
    return (y * g.astype(jnp.float32)).astype(x.dtype)


def masked_softmax(s, mask):
    s = jnp.where(mask, s, NEG)
    m = jnp.max(s, axis=-1, keepdims=True)
    e = jnp.where(mask, jnp.exp(s - m), 0.0)
    return e / jnp.maximum(jnp.sum(e, axis=-1, keepdims=True), 1e-30)


def alibi_slopes():
    return jnp.exp2(-8.0 * (jnp.arange(N_HEADS, dtype=jnp.float32) + 1.0) / N_HEADS)


def split_columns(z):
    sizes = (D_REC, D_REC, D_ATT, KV_W, KV_W, KV_W, KV_W, KV_W, KV_W, 3 * N_HEADS)
    out, off = [], 0
    for s in sizes:
        out.append(z[..., off:off + s])
        off += s
    return out


def rglru_branch(xr, gate, conv_w, conv_b, w_a, b_a, w_x, b_x, lam):
    B, S, C = xr.shape
    xc = lax.conv_general_dilated(xr, conv_w[:, None, :], window_strides=(1,),
                                  padding=[(CONV_W - 1, 0)],
                                  dimension_numbers=("NWC", "WIO", "NWC"),
                                  feature_group_count=C) + conv_b
    xb = xc.reshape(B, S, REC_BLOCKS, REC_BLOCK)
    r = jax.nn.sigmoid(jnp.einsum("bshi,hij->bshj", xb, w_a).reshape(B, S, C) + b_a)
    i = jax.nn.sigmoid(jnp.einsum("bshi,hij->bshj", xb, w_x).reshape(B, S, C) + b_x)
    log_a = -LRU_C * r.astype(jnp.float32) * jax.nn.softplus(-lam.astype(jnp.float32))
    a = jnp.exp(log_a)
    b = jnp.sqrt(-jnp.expm1(2.0 * log_a)) * (i * xc).astype(jnp.float32)

    def combine(left, right):
        a1, b1 = left
        a2, b2 = right
        return a1 * a2, a2 * b1 + b2

    _, h = lax.associative_scan(combine, (a, b), axis=1)
    return h.astype(xr.dtype) * jax.nn.gelu(gate)


def nsa_branch(q, kc, vc, ks, vs, kw, vw, gates, pe_k, pe_v,
               wk1, bk1, wk2, bk2, wv1, bv1, wv2, bv2):
    B, S, _ = q.shape
    dt = q.dtype

    def heads_first(t):
        return t.reshape(B, S, N_KV, HEAD_DIM).transpose(0, 2, 1, 3)

    q = q.reshape(B, S, N_KV, GQA_R, HEAD_DIM).transpose(0, 2, 3, 1, 4)
    g = jax.nn.sigmoid(gates.astype(jnp.float32)).reshape(B, S, N_KV, GQA_R, 3)
    g = g.transpose(0, 2, 3, 1, 4)
    kc, vc, ks, vs, kw, vw = [heads_first(t) for t in (kc, vc, ks, vs, kw, vw)]

    n_cmp = (S - L_CMP) // STRIDE_CMP + 1
    cmp_start = jnp.arange(n_cmp) * STRIDE_CMP
    cmp_idx = cmp_start[:, None] + jnp.arange(L_CMP)[None, :]

    def compress(t, pe, w1, b1, w2, b2):
        blocks = t[:, :, cmp_idx, :] + pe
        flat = blocks.reshape(B, N_KV, n_cmp, L_CMP * HEAD_DIM)
        return jax.nn.gelu(flat @ w1 + b1) @ w2 + b2

    k_cmp = compress(kc, pe_k, wk1, bk1, wk2, bk2)
    v_cmp = compress(vc, pe_v, wv1, bv1, wv2, bv2)
    cmp_last = cmp_start + L_CMP - 1
    cmp_center = cmp_start.astype(jnp.float32) + (L_CMP - 1) * 0.5

    n_sel = S // L_SEL
    n_top = min(N_SEL, n_sel)
    sel_start = jnp.arange(n_sel) * L_SEL
    overlap = jnp.clip(jnp.minimum(cmp_start[:, None] + L_CMP, sel_start[None, :] + L_SEL)
                       - jnp.maximum(cmp_start[:, None], sel_start[None, :]), 0)
    overlap = overlap.astype(jnp.float32) / STRIDE_CMP
    ks_blk = ks.reshape(B, N_KV, n_sel, L_SEL, HEAD_DIM)
    vs_blk = vs.reshape(B, N_KV, n_sel, L_SEL, HEAD_DIM)

    kw_pad = jnp.pad(kw, ((0, 0), (0, 0), (WINDOW, 0), (0, 0)))
    vw_pad = jnp.pad(vw, ((0, 0), (0, 0), (WINDOW, 0), (0, 0)))

    slopes = alibi_slopes().reshape(N_KV, GQA_R)[None, :, :, None, None]
    scale = HEAD_DIM ** -0.5
    n_qb = S // Q_BLOCK
    q_blk = q.reshape(B, N_KV, GQA_R, n_qb, Q_BLOCK, HEAD_DIM).transpose(3, 0, 1, 2, 4, 5)
    g_blk = g.reshape(B, N_KV, GQA_R, n_qb, Q_BLOCK, 3).transpose(3, 0, 1, 2, 4, 5)
    bi = jnp.arange(B)[:, None, None, None]
    gi = jnp.arange(N_KV)[None, :, None, None]
    j_sel = jnp.arange(n_sel)
    band = jnp.arange(WINDOW + Q_BLOCK)

    def one_block(args):
        qb, gb, blk = args
        t = blk * Q_BLOCK + jnp.arange(Q_BLOCK)
        tf = t.astype(jnp.float32)

        s = jnp.einsum("bgrqd,bgnd->bgrqn", qb, k_cmp).astype(jnp.float32) * scale
        s = s - slopes * (tf[:, None] - cmp_center[None, :])
        p_c = masked_softmax(s, cmp_last[None, :] <= t[:, None])
        o_c = jnp.einsum("bgrqn,bgnd->bgrqd", p_c.astype(dt), v_cmp)

        imp = jnp.einsum("bgrqn,nm->bgqm", p_c, overlap)
        cur = t // L_SEL
        valid = sel_start[None, :] <= t[:, None]
        forced = valid & ((j_sel[None, :] == 0) | (j_sel[None, :] == cur[:, None])
                          | (j_sel[None, :] == cur[:, None] - 1))
        score = jnp.where(forced, BIG, jnp.where(valid, imp, NEG))
        _, idx = lax.top_k(score, n_top)
        k_g = ks_blk[bi, gi, idx]
        v_g = vs_blk[bi, gi, idx]
        s_pos = idx[..., None] * L_SEL + jnp.arange(L_SEL)
        dist = t[None, None, :, None, None] - s_pos
        s = jnp.einsum("bgrqd,bgqnkd->bgrqnk", qb, k_g).astype(jnp.float32) * scale
        s = s - slopes[..., None] * dist[:, :, None].astype(jnp.float32)
        mask_s = (dist >= 0).reshape(B, N_KV, 1, Q_BLOCK, n_top * L_SEL)
        p_s = masked_softmax(s.reshape(B, N_KV, GQA_R, Q_BLOCK, n_top * L_SEL), mask_s)
        o_s = jnp.einsum("bgrqm,bgqmd->bgrqd", p_s.astype(dt),
                         v_g.reshape(B, N_KV, Q_BLOCK, n_top * L_SEL, HEAD_DIM))

        k_w = lax.dynamic_slice_in_dim(kw_pad, blk * Q_BLOCK, WINDOW + Q_BLOCK, axis=2)
        v_w = lax.dynamic_slice_in_dim(vw_pad, blk * Q_BLOCK, WINDOW + Q_BLOCK, axis=2)
        w_pos = blk * Q_BLOCK - WINDOW + band
        dw = t[:, None] - w_pos[None, :]
        mask_w = (dw >= 0) & (dw < WINDOW) & (w_pos[None, :] >= 0)
        s = jnp.einsum("bgrqd,bgkd->bgrqk", qb, k_w).astype(jnp.float32) * scale
        s = s - slopes * dw.astype(jnp.float32)
        p_w = masked_softmax(s, mask_w)
        o_w = jnp.einsum("bgrqk,bgkd->bgrqd", p_w.astype(dt), v_w)

        gb = gb.astype(dt)
        return gb[..., 0:1] * o_c + gb[..., 1:2] * o_s + gb[..., 2:3] * o_w

    out = lax.map(one_block, (q_blk, g_blk, jnp.arange(n_qb)))
    return out.transpose(1, 0, 4, 2, 3, 5).reshape(B, S, D_ATT)


def hier_moe(u, w_grp, b_grp, w_exp_r, b_exp_r, w_gate, w_up, w_down):
    B, S, D = u.shape
    N = B * S
    xt = u.reshape(N, D)
    grp_logits = (xt @ w_grp + b_grp).astype(jnp.float32)
    grp_prob = jax.nn.softmax(grp_logits, axis=-1)
    p_g, g_idx = lax.top_k(grp_prob, 1)
    exp_logits = (xt @ w_exp_r + b_exp_r).astype(jnp.float32).reshape(N, N_GROUPS, EXP_PER_GROUP)
    in_grp = exp_logits[jnp.arange(N), g_idx[:, 0]]
    top_v, top_i = lax.top_k(in_grp, TOP_K_IN_GROUP)
    gate = p_g * jax.nn.softmax(top_v, axis=-1)
    eid = g_idx * EXP_PER_GROUP + top_i

    A = N * TOP_K_IN_GROUP
    flat_e = eid.reshape(A)
    flat_tok = jnp.repeat(jnp.arange(N, dtype=jnp.int32), TOP_K_IN_GROUP)
    flat_g = gate.reshape(A)
    order = jnp.argsort(flat_e)
    se = flat_e[order]
    counts = jnp.bincount(flat_e, length=N_EXPERTS)
    starts = jnp.cumsum(counts) - counts
    padded = ((counts + MOE_CHUNK - 1) // MOE_CHUNK) * MOE_CHUNK
    pstarts = jnp.cumsum(padded) - padded
    dest = pstarts[se] + (jnp.arange(A) - starts[se])
    n_chunks = -(-A // MOE_CHUNK) + N_EXPERTS
    P = n_chunks * MOE_CHUNK
    buf_tok = jnp.full((P,), N, dtype=jnp.int32).at[dest].set(flat_tok[order])
    buf_gate = jnp.zeros((P,), jnp.float32).at[dest].set(flat_g[order])
    chunk_start = jnp.arange(n_chunks) * MOE_CHUNK
    chunk_e = jnp.minimum(jnp.sum((pstarts + padded)[None, :] <= chunk_start[:, None], axis=1),
                          N_EXPERTS - 1)
    x_pad = jnp.concatenate([xt, jnp.zeros((1, D), xt.dtype)], axis=0)

    def expert_chunk(args):
        tok, e = args
        xs = x_pad[tok]
        hdn = jax.nn.silu(xs @ w_gate[e]) * (xs @ w_up[e])
        return hdn @ w_down[e]

    y = lax.map(expert_chunk, (buf_tok.reshape(n_chunks, MOE_CHUNK), chunk_e)).reshape(P, D)
    y = y * buf_gate[:, None].astype(y.dtype)
    out = jax.ops.segment_sum(y, buf_tok, num_segments=N + 1)[:N]
    return out.reshape(B, S, D)


def setup_inputs(seed: int = 0) -> dict:
    key = jax.random.key(seed)
    ks = jax.random.split(key, 40)
    f32 = jnp.float32

    def nrm(k, shape, scale):
        return jax.random.normal(k, shape, f32) * scale

    def gain(k, shape):
        return 1.0 + 0.01 * jax.random.normal(k, shape, f32)

    a8 = jax.random.uniform(ks[10], (DEPTH, D_REC), f32, minval=0.9, maxval=0.999)
    a_base = a8 ** (1.0 / LRU_C)
    lam = jnp.log(a_base) - jnp.log1p(-a_base)
    return {
        "x": nrm(ks[0], (BATCH, SEQ, D_MODEL), 1.0),
        "p": nrm(ks[1], (DEPTH, BATCH, SEQ, D_PLE), 1.0),
        "mix_norm": gain(ks[2], (DEPTH, D_MODEL)),
        "w_in": nrm(ks[3], (DEPTH, D_MODEL, D_IN), D_MODEL ** -0.5),
        "conv_w": nrm(ks[4], (DEPTH, CONV_W, D_REC), CONV_W ** -0.5),
        "conv_b": nrm(ks[5], (DEPTH, D_REC), 0.01),
        "lru_wa": nrm(ks[6], (DEPTH, REC_BLOCKS, REC_BLOCK, REC_BLOCK), REC_BLOCK ** -0.5),
        "lru_ba": nrm(ks[7], (DEPTH, D_REC), 0.01),
        "lru_wx": nrm(ks[8], (DEPTH, REC_BLOCKS, REC_BLOCK, REC_BLOCK), REC_BLOCK ** -0.5),
        "lru_bx": nrm(ks[9], (DEPTH, D_REC), 0.01),
        "lru_lambda": lam,
        "cmp_pe_k": nrm(ks[11], (DEPTH, L_CMP, HEAD_DIM), 0.02),
        "cmp_pe_v": nrm(ks[12], (DEPTH, L_CMP, HEAD_DIM), 0.02),
        "cmp_wk1": nrm(ks[13], (DEPTH, L_CMP * HEAD_DIM, CMP_HIDDEN), (L_CMP * HEAD_DIM) ** -0.5),
        "cmp_bk1": nrm(ks[14], (DEPTH, CMP_HIDDEN), 0.01),
        "cmp_wk2": nrm(ks[15], (DEPTH, CMP_HIDDEN, HEAD_DIM), CMP_HIDDEN ** -0.5),
        "cmp_bk2": nrm(ks[16], (DEPTH, HEAD_DIM), 0.01),
        "cmp_wv1": nrm(ks[17], (DEPTH, L_CMP * HEAD_DIM, CMP_HIDDEN), (L_CMP * HEAD_DIM) ** -0.5),
        "cmp_bv1": nrm(ks[18], (DEPTH, CMP_HIDDEN), 0.01),
        "cmp_wv2": nrm(ks[19], (DEPTH, CMP_HIDDEN, HEAD_DIM), CMP_HIDDEN ** -0.5),
        "cmp_bv2": nrm(ks[20], (DEPTH, HEAD_DIM), 0.01),
        "rec_out_norm": gain(ks[21], (DEPTH, D_REC)),
        "att_out_norm": gain(ks[22], (DEPTH, D_ATT)),
        "w_out": nrm(ks[23], (DEPTH, D_REC + D_ATT, D_MODEL), (D_REC + D_ATT) ** -0.5),
        "ffn_norm": gain(ks[24], (DEPTH, D_MODEL)),
        "w_group_router": nrm(ks[25], (DEPTH, D_MODEL, N_GROUPS), D_MODEL ** -0.5),
        "b_group_router": nrm(ks[26], (DEPTH, N_GROUPS), 0.01),
        "w_expert_router": nrm(ks[27], (DEPTH, D_MODEL, N_EXPERTS), D_MODEL ** -0.5),
        "b_expert_router": nrm(ks[28], (DEPTH, N_EXPERTS), 0.01),
        "w_gate_exp": nrm(ks[29], (DEPTH, N_EXPERTS, D_MODEL, D_EXPERT), D_MODEL ** -0.5),
        "w_up_exp": nrm(ks[30], (DEPTH, N_EXPERTS, D_MODEL, D_EXPERT), D_MODEL ** -0.5),
        "w_down_exp": nrm(ks[31], (DEPTH, N_EXPERTS, D_EXPERT, D_MODEL), D_EXPERT ** -0.5),
        "ple_norm": gain(ks[32], (DEPTH, D_MODEL)),
        "w_ple_gate": nrm(ks[33], (DEPTH, D_MODEL, D_MODEL), D_MODEL ** -0.5),
        "b_ple_gate": nrm(ks[34], (DEPTH, D_MODEL), 0.01),
        "w_ple_up": nrm(ks[35], (DEPTH, D_PLE, D_MODEL), D_PLE ** -0.5),
        "final_norm": gain(ks[36], (D_MODEL,)),
    }


def reference(x, p, mix_norm, w_in, conv_w, conv_b, lru_wa, lru_ba, lru_wx, lru_bx, lru_lambda,
              cmp_pe_k, cmp_pe_v, cmp_wk1, cmp_bk1, cmp_wk2, cmp_bk2, cmp_wv1, cmp_bv1, cmp_wv2, cmp_bv2,
              rec_out_norm, att_out_norm, w_out, ffn_norm, w_group_router, b_group_router,
              w_expert_router, b_expert_router, w_gate_exp, w_up_exp, w_down_exp,
              ple_norm, w_ple_gate, b_ple_gate, w_ple_up, final_norm):
    h = x
    for i in range(DEPTH):
        u = rmsnorm(h, mix_norm[i])
        z = u @ w_in[i]
        xr, gr, q, kc, vc, ksl, vsl, kwn, vwn, ng = split_columns(z)
        rec = rglru_branch(xr, gr, conv_w[i], conv_b[i], lru_wa[i], lru_ba[i],
                           lru_wx[i], lru_bx[i], lru_lambda[i])
        att = nsa_branch(q, kc, vc, ksl, vsl, kwn, vwn, ng, cmp_pe_k[i], cmp_pe_v[i],
                         cmp_wk1[i], cmp_bk1[i], cmp_wk2[i], cmp_bk2[i],
                         cmp_wv1[i], cmp_bv1[i], cmp_wv2[i], cmp_bv2[i])
        mixed = jnp.concatenate([rmsnorm(rec, rec_out_norm[i]), rmsnorm(att, att_out_norm[i])], axis=-1)
        h = h + mixed @ w_out[i]
        h = h + hier_moe(rmsnorm(h, ffn_norm[i]), w_group_router[i], b_group_router[i],
                         w_expert_router[i], b_expert_router[i],
                         w_gate_exp[i], w_up_exp[i], w_down_exp[i])
        gate = jax.nn.sigmoid(rmsnorm(h, ple_norm[i]) @ w_ple_gate[i] + b_ple_gate[i])
        h = h + (p[i] @ w_ple_up[i]) * gate
    return rmsnorm(h, final_norm)
```

```python
import functools

import jax
import jax.numpy as jnp
from jax import lax
from jax.experimental import pallas as pl
from jax.experimental.pallas import tpu as pltpu

F32 = jnp.float32
BF16 = jnp.bfloat16

D_REC = 512
REC_BLOCKS = 8
CONV_W = 4
LRU_C = 8.0
N_HEADS = 8
HEAD_DIM = 64
N_KV = 2
GQA_R = N_HEADS // N_KV
D_ATT = N_HEADS * HEAD_DIM
KV_W = N_KV * HEAD_DIM
L_CMP = 32
STRIDE_CMP = 16
L_SEL = 64
N_SEL = 16
WINDOW = 512
N_GROUPS = 4
EXP_PER_GROUP = 8
N_EXPERTS = N_GROUPS * EXP_PER_GROUP
EPS = 1e-6
NEG = -1e30
BIG = 1e30

LANES = 128
SUBLANES = 8
MXU_DIM = 256
VMEM_LIMIT = 56 * 1024 * 1024

ROW_TILE = 512
SCAN_TILE = 512
Q_TILE = 128
K_TILE = 512
EXPERT_CHUNK = 256


def _cparams(sem):
    return pltpu.CompilerParams(dimension_semantics=sem, vmem_limit_bytes=VMEM_LIMIT)


def _rms(x, g):
    return x * lax.rsqrt(jnp.mean(x * x, axis=-1, keepdims=True) + EPS) * g


def _dot(a, b):
    return jnp.dot(a, b, preferred_element_type=F32)


def _dot_nt(a, b):
    return lax.dot_general(a, b, (((1,), (1,)), ((), ())), preferred_element_type=F32)


def _masked_softmax(s, mask):
    s = jnp.where(mask, s, NEG)
    m = jnp.max(s, axis=-1, keepdims=True)
    e = jnp.where(mask, jnp.exp(s - m), 0.0)
    return e / jnp.maximum(jnp.sum(e, axis=-1, keepdims=True), 1e-30)


def _full(shape):
    n = len(shape)
    return pl.BlockSpec(shape, lambda *_: (0,) * n)


def _proj_in_kernel(h_ref, g_ref, wa_ref, wq_ref, wkv_ref, wg_ref,
                    xr_ref, gr_ref, q_ref, kc_ref, vc_ref, ks_ref, vs_ref, kw_ref, vw_ref,
                    gate_ref):
    u = _rms(h_ref[0], g_ref[...]).astype(BF16)
    za = _dot(u, wa_ref[...])
    xr_ref[0] = za[:, :D_REC]
    gr_ref[0] = za[:, D_REC:]
    zq = (_dot(u, wq_ref[...]) * (HEAD_DIM ** -0.5)).astype(q_ref.dtype)
    for h in range(N_HEADS):
        q_ref[0, h] = zq[:, h * HEAD_DIM:(h + 1) * HEAD_DIM]
    zkv = _dot(u, wkv_ref[...])
    for i, ref in enumerate((kc_ref, vc_ref, ks_ref, vs_ref, kw_ref, vw_ref)):
        for g in range(N_KV):
            lo = i * KV_W + g * HEAD_DIM
            ref[0, g] = zkv[:, lo:lo + HEAD_DIM].astype(ref.dtype)
    gate_ref[0] = jax.nn.sigmoid(_dot(u, wg_ref[...]))


def _proj_in(h, g, wa, wq, wkv, wg):
    B, S, D = h.shape
    TM = min(ROW_TILE, S)
    kv_f32 = jax.ShapeDtypeStruct((B, N_KV, S, HEAD_DIM), F32)
    kv_b16 = jax.ShapeDtypeStruct((B, N_KV, S, HEAD_DIM), BF16)
    kv_spec = pl.BlockSpec((1, N_KV, TM, HEAD_DIM), lambda b, s: (b, 0, s, 0))
    row = lambda w: pl.BlockSpec((1, TM, w), lambda b, s: (b, s, 0))
    return pl.pallas_call(
        _proj_in_kernel,
        grid=(B, S // TM),
        in_specs=[row(D), _full(g.shape), _full(wa.shape), _full(wq.shape), _full(wkv.shape),
                  _full(wg.shape)],
        out_specs=[row(D_REC), row(D_REC),
                   pl.BlockSpec((1, N_HEADS, TM, HEAD_DIM), lambda b, s: (b, 0, s, 0)),
                   kv_spec, kv_spec, kv_spec, kv_spec, kv_spec, kv_spec, row(LANES)],
        out_shape=[jax.ShapeDtypeStruct((B, S, D_REC), F32),
                   jax.ShapeDtypeStruct((B, S, D_REC), F32),
                   jax.ShapeDtypeStruct((B, N_HEADS, S, HEAD_DIM), BF16),
                   kv_f32, kv_f32, kv_b16, kv_b16, kv_b16, kv_b16,
                   jax.ShapeDtypeStruct((B, S, LANES), F32)],
        compiler_params=_cparams(("parallel", "parallel")),
        name="proj_in",
    )(h, g, wa, wq, wkv, wg)


def _rglru_kernel(xr_ref, gr_ref, cw_ref, cb_ref, wa_ref, ba_ref, wx_ref, bx_ref, lam_ref,
                  gn_ref, out_ref, ext_ref, a_ref, b_ref, carry_ref):
    TS = xr_ref.shape[1]
    C = xr_ref.shape[2]

    @pl.when(pl.program_id(1) == 0)
    def _():
        ext_ref[0:SUBLANES, :] = jnp.zeros((SUBLANES, C), F32)
        carry_ref[...] = jnp.zeros(carry_ref.shape, F32)

    x = xr_ref[0]
    ext_ref[SUBLANES:, :] = x
    xc = cb_ref[...] + cw_ref[CONV_W - 1:CONV_W, :] * x
    for j in range(CONV_W - 1):
        off = SUBLANES - (CONV_W - 1) + j
        xc = xc + cw_ref[j:j + 1, :] * ext_ref[off:off + TS, :]
    ext_ref[0:SUBLANES, :] = x[TS - SUBLANES:, :]

    xcb = xc.astype(BF16)
    nt = C // MXU_DIM
    ra = jnp.concatenate(
        [_dot(xcb[:, k * MXU_DIM:(k + 1) * MXU_DIM], wa_ref[k]) for k in range(nt)], axis=-1)
    rx = jnp.concatenate(
        [_dot(xcb[:, k * MXU_DIM:(k + 1) * MXU_DIM], wx_ref[k]) for k in range(nt)], axis=-1)
    r = jax.nn.sigmoid(ra + ba_ref[...])
    i = jax.nn.sigmoid(rx + bx_ref[...])
    lam = lam_ref[...]
    softplus_neg_lam = jnp.maximum(-lam, 0.0) + jnp.log1p(jnp.exp(-jnp.abs(lam)))
    log_a = -LRU_C * r * softplus_neg_lam
    a_ref[...] = jnp.exp(log_a)
    b_ref[...] = jnp.sqrt(1.0 - jnp.exp(2.0 * log_a)) * (i * xc)

    row = lax.broadcasted_iota(jnp.int32, (SUBLANES, C), 0)

    def group(gi, carry):
        r0 = pl.multiple_of(gi * SUBLANES, SUBLANES)
        a = a_ref[pl.ds(r0, SUBLANES), :]
        b = b_ref[pl.ds(r0, SUBLANES), :]
        d = 1
        while d < SUBLANES:
            keep = row >= d
            a_s = pltpu.roll(a, d, axis=0)
            b_s = pltpu.roll(b, d, axis=0)
            b = jnp.where(keep, a * b_s, 0.0) + b
            a = jnp.where(keep, a * a_s, a)
            d *= 2
        hcur = b + a * carry
        b_ref[pl.ds(r0, SUBLANES), :] = hcur
        return hcur[SUBLANES - 1:SUBLANES, :]

    carry_ref[...] = lax.fori_loop(0, TS // SUBLANES, group, carry_ref[...])

    y = b_ref[...] * jax.nn.gelu(gr_ref[0])
    out_ref[0] = _rms(y, gn_ref[...]).astype(out_ref.dtype)


def _rglru(xr, gr, cw, cb, wa, ba, wx, bx, lam, gn):
    B, S, C = xr.shape
    TS = min(SCAN_TILE, S)
    row = pl.BlockSpec((1, TS, C), lambda b, s: (b, s, 0))
    args = (cw, cb, wa, ba, wx, bx, lam, gn)
    return pl.pallas_call(
        _rglru_kernel,
        grid=(B, S // TS),
        in_specs=[row, row] + [_full(a.shape) for a in args],
        out_specs=row,
        out_shape=jax.ShapeDtypeStruct((B, S, C), BF16),
        scratch_shapes=[pltpu.VMEM((TS + SUBLANES, C), F32), pltpu.VMEM((TS, C), F32),
                        pltpu.VMEM((TS, C), F32), pltpu.VMEM((1, C), F32)],
        compiler_params=_cparams(("parallel", "arbitrary")),
        name="rglru",
    )(xr, gr, *args)


def _compress_kernel(kc_ref, vc_ref, pek_ref, pev_ref, wk1_ref, bk1_ref, wk2_ref, bk2_ref,
                     wv1_ref, bv1_ref, wv2_ref, bv2_ref, ko_ref, vo_ref):
    def one(x_ref, pe_ref, w1_ref, b1_ref, w2_ref, b2_ref, o_ref):
        x = x_ref[0, 0]
        nb = x.shape[0]
        first = _dot((x + pe_ref[0:1, :]).astype(BF16), w1_ref[0])
        second = _dot((x + pe_ref[1:2, :]).astype(BF16), w1_ref[1])
        hid = jax.nn.gelu(first + pltpu.roll(second, nb - 1, axis=0) + b1_ref[...])
        o_ref[0, 0] = (_dot(hid.astype(BF16), w2_ref[...]) + b2_ref[...]).astype(o_ref.dtype)

    one(kc_ref, pek_ref, wk1_ref, bk1_ref, wk2_ref, bk2_ref, ko_ref)
    one(vc_ref, pev_ref, wv1_ref, bv1_ref, wv2_ref, bv2_ref, vo_ref)


def _compress(kc, vc, pek, pev, wk1, bk1, wk2, bk2, wv1, bv1, wv2, bv2):
    B, G, NB, W = kc.shape
    blk = pl.BlockSpec((1, 1, NB, W), lambda b, g: (b, g, 0, 0))
    oblk = pl.BlockSpec((1, 1, NB, HEAD_DIM), lambda b, g: (b, g, 0, 0))
    args = (pek, pev, wk1, bk1, wk2, bk2, wv1, bv1, wv2, bv2)
    oshape = jax.ShapeDtypeStruct((B, G, NB, HEAD_DIM), BF16)
    return pl.pallas_call(
        _compress_kernel,
        grid=(B, G),
        in_specs=[blk, blk] + [_full(a.shape) for a in args],
        out_specs=[oblk, oblk],
        out_shape=[oshape, oshape],
        compiler_params=_cparams(("parallel", "parallel")),
        name="compress",
    )(kc, vc, *args)


def _nsa_kernel(q_ref, kcmp_ref, vcmp_ref, ovl_ref, ks_ref, vs_ref, kw_ref, vw_ref, gate_ref,
                out_ref, *, n_top):
    TQ = q_ref.shape[2]
    NC = kcmp_ref.shape[2]
    S = ks_ref.shape[2]
    NS = S // L_SEL
    g = pl.program_id(1)
    t0 = pl.program_id(2) * TQ
    t_i = t0 + lax.broadcasted_iota(jnp.int32, (TQ, 1), 0)
    t_f = t_i.astype(F32)
    slopes = [jnp.where(g == 0, 2.0 ** -(r + 1), 2.0 ** -(GQA_R + r + 1)).astype(F32)
              for r in range(GQA_R)]
    qs = [q_ref[0, r] for r in range(GQA_R)]

    n_i = lax.broadcasted_iota(jnp.int32, (1, NC), 1)
    center = (n_i * STRIDE_CMP).astype(F32) + (L_CMP - 1) * 0.5
    mask_c = (n_i * STRIDE_CMP + (L_CMP - 1)) <= t_i
    dist_c = t_f - center
    kc = kcmp_ref[0, 0]
    vc = vcmp_ref[0, 0]
    p_sum = jnp.zeros((TQ, NC), F32)
    o_c = []
    for r in range(GQA_R):
        p = _masked_softmax(_dot_nt(qs[r], kc) - slopes[r] * dist_c, mask_c)
        p_sum = p_sum + p
        o_c.append(_dot(p.astype(BF16), vc))

    imp = jnp.dot(p_sum, ovl_ref[...], preferred_element_type=F32,
                  precision=lax.Precision.HIGHEST)
    m_i = lax.broadcasted_iota(jnp.int32, (1, NS), 1)
    cur = t_i // L_SEL
    valid = (m_i * L_SEL) <= t_i
    forced = valid & ((m_i == 0) | (m_i == cur) | (m_i == cur - 1))
    score = jnp.where(forced, BIG, jnp.where(valid, imp, NEG))
    rank = jnp.zeros((TQ, NS), F32)
    for j in range(NS):
        sj = score[:, j:j + 1]
        ahead = (sj > score) | ((sj == score) & (m_i > j))
        rank = rank + jnp.where(ahead, 1.0, 0.0)
    sel = jnp.where(rank < n_top, 1.0, 0.0).astype(BF16)

    blk_of_row = lax.broadcasted_iota(jnp.int32, (NS, K_TILE), 0)
    key_in_tile = lax.broadcasted_iota(jnp.int32, (NS, K_TILE), 1)
    pos_in_tile = lax.broadcasted_iota(jnp.int32, (1, K_TILE), 1)

    def sel_tile(j, carry):
        k0 = pl.multiple_of(j * K_TILE, K_TILE)
        kt = ks_ref[0, 0, pl.ds(k0, K_TILE), :]
        vt = vs_ref[0, 0, pl.ds(k0, K_TILE), :]
        expand = jnp.where((k0 + key_in_tile) // L_SEL == blk_of_row, 1.0, 0.0).astype(BF16)
        chosen = _dot(sel, expand)
        pos = k0 + pos_in_tile
        dist = t_i - pos
        mask = (chosen > 0.5) & (dist >= 0)
        dist_f = dist.astype(F32)
        new = []
        for r in range(GQA_R):
            m_old, l_old, acc = carry[r]
            s = jnp.where(mask, _dot_nt(qs[r], kt) - slopes[r] * dist_f, NEG)
            m_new = jnp.maximum(m_old, jnp.max(s, axis=-1, keepdims=True))
            alpha = jnp.exp(m_old - m_new)
            e = jnp.where(mask, jnp.exp(s - m_new), 0.0)
            l_new = alpha * l_old + jnp.sum(e, axis=-1, keepdims=True)
            acc = alpha * acc + _dot(e.astype(BF16), vt)
            new.append((m_new, l_new, acc))
        return tuple(new)

    init = tuple((jnp.full((TQ, 1), NEG, F32), jnp.zeros((TQ, 1), F32),
                  jnp.zeros((TQ, HEAD_DIM), F32)) for _ in range(GQA_R))
    n_kt = (t0 + TQ + K_TILE - 1) // K_TILE
    fin = lax.fori_loop(0, n_kt, sel_tile, init)
    o_s = [acc / jnp.maximum(l, 1e-30) for (_, l, acc) in fin]

    WK = WINDOW + TQ
    start = pl.multiple_of(jnp.maximum(t0 - WINDOW, 0), TQ)
    kt = kw_ref[0, 0, pl.ds(start, WK), :]
    vt = vw_ref[0, 0, pl.ds(start, WK), :]
    dw = t_i - (start + lax.broadcasted_iota(jnp.int32, (1, WK), 1))
    mask_w = (dw >= 0) & (dw < WINDOW)
    dw_f = dw.astype(F32)
    o_w = []
    for r in range(GQA_R):
        p = _masked_softmax(_dot_nt(qs[r], kt) - slopes[r] * dw_f, mask_w)
        o_w.append(_dot(p.astype(BF16), vt))

    gl = gate_ref[0]
    gsel = jnp.where(g == 0, gl[:, :LANES // 2], gl[:, LANES // 2:])
    outs = []
    for r in range(GQA_R):
        outs.append(gsel[:, 3 * r:3 * r + 1] * o_c[r] + gsel[:, 3 * r + 1:3 * r + 2] * o_s[r]
                    + gsel[:, 3 * r + 2:3 * r + 3] * o_w[r])
    out_ref[0] = jnp.concatenate(outs, axis=-1).astype(out_ref.dtype)


def _nsa(q, kcmp, vcmp, ovl, ks, vs, kw, vw, gates):
    B, _, S, _ = q.shape
    NC = kcmp.shape[2]
    TQ = min(Q_TILE, S)
    n_top = min(N_SEL, S // L_SEL)
    cmp_spec = pl.BlockSpec((1, 1, NC, HEAD_DIM), lambda b, g, i: (b, g, 0, 0))
    seq_spec = pl.BlockSpec((1, 1, S, HEAD_DIM), lambda b, g, i: (b, g, 0, 0))
    return pl.pallas_call(
        functools.partial(_nsa_kernel, n_top=n_top),
        grid=(B, N_KV, S // TQ),
        in_specs=[pl.BlockSpec((1, GQA_R, TQ, HEAD_DIM), lambda b, g, i: (b, g, i, 0)),
                  cmp_spec, cmp_spec, _full(ovl.shape), seq_spec, seq_spec, seq_spec, seq_spec,
                  pl.BlockSpec((1, TQ, LANES), lambda b, g, i: (b, i, 0))],
        out_specs=pl.BlockSpec((1, TQ, GQA_R * HEAD_DIM), lambda b, g, i: (b, i, g)),
        out_shape=jax.ShapeDtypeStruct((B, S, D_ATT), F32),
        compiler_params=_cparams(("parallel", "parallel", "parallel")),
        name="nsa",
    )(q, kcmp, vcmp, ovl, ks, vs, kw, vw, gates)


def _out_proj_kernel(h_ref, rec_ref, att_ref, an_ref, wor_ref, woa_ref, fn_ref, wr_ref, br_ref,
                     h2_ref, u_ref, route_ref):
    att = _rms(att_ref[...], an_ref[...]).astype(BF16)
    h2 = h_ref[...] + _dot(rec_ref[...], wor_ref[...]) + _dot(att, woa_ref[...])
    h2_ref[...] = h2
    u = _rms(h2, fn_ref[...])
    ub = u.astype(BF16)
    u_ref[...] = ub
    lg = _dot(ub, wr_ref[...]) + br_ref[...]
    lane = lax.broadcasted_iota(jnp.int32, lg.shape, 1)
    is_grp = lane < N_GROUPS
    gl = jnp.where(is_grp, lg, NEG)
    gm = jnp.max(gl, axis=-1, keepdims=True)
    p_g = 1.0 / jnp.sum(jnp.where(is_grp, jnp.exp(gl - gm), 0.0), axis=-1, keepdims=True)
    g_idx = jnp.min(jnp.where(gl == gm, lane, LANES), axis=-1, keepdims=True)
    lo = N_GROUPS + EXP_PER_GROUP * g_idx
    in_g = (lane >= lo) & (lane < lo + EXP_PER_GROUP)
    el = jnp.where(in_g, lg, NEG)
    v1 = jnp.max(el, axis=-1, keepdims=True)
    i1 = jnp.min(jnp.where(in_g & (el == v1), lane, LANES), axis=-1, keepdims=True)
    rest = in_g & (lane != i1)
    el2 = jnp.where(rest, lg, NEG)
    v2 = jnp.max(el2, axis=-1, keepdims=True)
    i2 = jnp.min(jnp.where(rest & (el2 == v2), lane, LANES), axis=-1, keepdims=True)
    e2 = jnp.exp(v2 - v1)
    den = 1.0 + e2
    route = jnp.where(lane == 0, (i1 - N_GROUPS).astype(F32),
                      jnp.where(lane == 1, (i2 - N_GROUPS).astype(F32),
                                jnp.where(lane == 2, p_g * (1.0 / den),
                                          jnp.where(lane == 3, p_g * (e2 / den), 0.0))))
    route_ref[...] = route


def _out_proj(h, rec, att, an, wor, woa, fn, wr, br):
    N, D = h.shape
    TM = min(ROW_TILE, N)
    row = lambda w: pl.BlockSpec((TM, w), lambda i: (i, 0))
    args = (an, wor, woa, fn, wr, br)
    return pl.pallas_call(
        _out_proj_kernel,
        grid=(N // TM,),
        in_specs=[row(D), row(D_REC), row(D_ATT)] + [_full(a.shape) for a in args],
        out_specs=[row(D), row(D), row(LANES)],
        out_shape=[jax.ShapeDtypeStruct((N, D), F32), jax.ShapeDtypeStruct((N, D), BF16),
                   jax.ShapeDtypeStruct((N, LANES), F32)],
        compiler_params=_cparams(("parallel",)),
        name="out_proj",
    )(h, rec, att, *args)


def _expert_kernel(ce_ref, xs_ref, gate_ref, wg_ref, wu_ref, wd_ref, y_ref, wgb, wub, wdb):
    c = pl.program_id(0)
    prev = ce_ref[jnp.maximum(c - 1, 0)]

    @pl.when((c == 0) | (ce_ref[c] != prev))
    def _():
        wgb[...] = wg_ref[0].astype(BF16)
        wub[...] = wu_ref[0].astype(BF16)
        wdb[...] = wd_ref[0].astype(BF16)

    xs = xs_ref[...]
    hdn = jax.nn.silu(_dot(xs, wgb[...])) * _dot(xs, wub[...])
    y_ref[...] = (_dot(hdn.astype(BF16), wdb[...]) * gate_ref[...]).astype(y_ref.dtype)


def _experts(chunk_e, xs, gates, wg, wu, wd):
    P, D = xs.shape
    DE = wg.shape[-1]
    C = EXPERT_CHUNK
    grid_spec = pltpu.PrefetchScalarGridSpec(
        num_scalar_prefetch=1,
        grid=(P // C,),
        in_specs=[pl.BlockSpec((C, D), lambda c, ce: (c, 0)),
                  pl.BlockSpec((C, 1), lambda c, ce: (c, 0)),
                  pl.BlockSpec((1, D, DE), lambda c, ce: (ce[c], 0, 0)),
                  pl.BlockSpec((1, D, DE), lambda c, ce: (ce[c], 0, 0)),
                  pl.BlockSpec((1, DE, D), lambda c, ce: (ce[c], 0, 0))],
        out_specs=pl.BlockSpec((C, D), lambda c, ce: (c, 0)),
        scratch_shapes=[pltpu.VMEM((D, DE), BF16), pltpu.VMEM((D, DE), BF16),
                        pltpu.VMEM((DE, D), BF16)],
    )
    return pl.pallas_call(
        _expert_kernel,
        grid_spec=grid_spec,
        out_shape=jax.ShapeDtypeStruct((P, D), F32),
        compiler_params=_cparams(("arbitrary",)),
        name="experts",
    )(chunk_e, xs, gates, wg, wu, wd)


def _dispatch(route, n_tok):
    C = EXPERT_CHUNK
    eid = route[:, 0:2].astype(jnp.int32)
    gate = route[:, 2:4]
    A = n_tok * 2
    flat_e = eid.reshape(A)
    flat_tok = jnp.repeat(jnp.arange(n_tok, dtype=jnp.int32), 2)
    onehot = (flat_e[:, None] == jnp.arange(N_EXPERTS, dtype=jnp.int32)[None, :]).astype(jnp.int32)
    within = jnp.take_along_axis(jnp.cumsum(onehot, axis=0), flat_e[:, None], axis=1)[:, 0] - 1
    counts = jnp.sum(onehot, axis=0)
    padded = ((counts + C - 1) // C) * C
    pstarts = jnp.cumsum(padded) - padded
    dest = pstarts[flat_e] + within
    n_chunks = -(-A // C) + N_EXPERTS
    P = n_chunks * C
    buf_tok = jnp.zeros((P,), jnp.int32).at[dest].set(flat_tok)
    buf_gate = jnp.zeros((P,), F32).at[dest].set(gate.reshape(A))
    chunk_start = jnp.arange(n_chunks, dtype=jnp.int32) * C
    chunk_e = jnp.minimum(jnp.sum((pstarts + padded)[None, :] <= chunk_start[:, None], axis=1),
                          N_EXPERTS - 1).astype(jnp.int32)
    return buf_tok, buf_gate, chunk_e, dest.reshape(n_tok, 2)


def _ple_kernel(h_ref, y0_ref, y1_ref, p_ref, pn_ref, wg_ref, bg_ref, wu_ref, fnorm_ref, o_ref,
                *, final):
    h3 = h_ref[...] + (y0_ref[...] + y1_ref[...])
    gate = jax.nn.sigmoid(_dot(_rms(h3, pn_ref[...]).astype(BF16), wg_ref[...]) + bg_ref[...])
    h4 = h3 + _dot(p_ref[...].astype(BF16), wu_ref[...]) * gate
    o_ref[...] = _rms(h4, fnorm_ref[...]) if final else h4


def _ple(h, y0, y1, p, pn, wg, bg, wu, fnorm, final):
    N, D = h.shape
    TM = min(ROW_TILE, N)
    row = lambda w: pl.BlockSpec((TM, w), lambda i: (i, 0))
    args = (pn, wg, bg, wu, fnorm)
    return pl.pallas_call(
        functools.partial(_ple_kernel, final=final),
        grid=(N // TM,),
        in_specs=[row(D), row(D), row(D), row(p.shape[1])] + [_full(a.shape) for a in args],
        out_specs=row(D),
        out_shape=jax.ShapeDtypeStruct((N, D), F32),
        compiler_params=_cparams(("parallel",)),
        name="ple",
    )(h, y0, y1, p, *args)


def _block_diag_tiles(w):
    nb, k, _ = w.shape
    per = MXU_DIM // k
    tiles = jnp.zeros((nb // per, MXU_DIM, MXU_DIM), w.dtype)
    for b in range(nb):
        t, o = divmod(b, per)
        tiles = tiles.at[t, o * k:(o + 1) * k, o * k:(o + 1) * k].set(w[b])
    return tiles


def _overlap(S):
    n = jnp.arange(S // STRIDE_CMP) * STRIDE_CMP
    m = jnp.arange(S // L_SEL) * L_SEL
    ov = jnp.clip(jnp.minimum(n[:, None] + L_CMP, m[None, :] + L_SEL)
                  - jnp.maximum(n[:, None], m[None, :]), 0)
    return ov.astype(F32) / STRIDE_CMP


def kernel(x, p, mix_norm, w_in, conv_w, conv_b, lru_wa, lru_ba, lru_wx, lru_bx, lru_lambda, cmp_pe_k, cmp_pe_v, cmp_wk1, cmp_bk1, cmp_wk2, cmp_bk2, cmp_wv1, cmp_bv1, cmp_wv2, cmp_bv2, rec_out_norm, att_out_norm, w_out, ffn_norm, w_group_router, b_group_router, w_expert_router, b_expert_router, w_gate_exp, w_up_exp, w_down_exp, ple_norm, w_ple_gate, b_ple_gate, w_ple_up, final_norm):
    B, S, D = x.shape
    depth = w_in.shape[0]
    N = B * S
    half = L_CMP // 2 * HEAD_DIM
    row = lambda v: v.reshape(1, -1)
    ovl = _overlap(S)
    o_q = 2 * D_REC
    o_kv = o_q + D_ATT
    o_g = o_kv + 6 * KV_W
    h = x
    for i in range(depth):
        wi = w_in[i]
        wgate = jnp.zeros((D, LANES), F32)
        for g in range(N_KV):
            wgate = wgate.at[:, g * (LANES // 2):g * (LANES // 2) + 3 * GQA_R].set(
                wi[:, o_g + g * 3 * GQA_R:o_g + (g + 1) * 3 * GQA_R])
        w_route = jnp.zeros((D, LANES), F32)
        w_route = w_route.at[:, :N_GROUPS].set(w_group_router[i])
        w_route = w_route.at[:, N_GROUPS:N_GROUPS + N_EXPERTS].set(w_expert_router[i])
        b_route = jnp.zeros((1, LANES), F32)
        b_route = b_route.at[0, :N_GROUPS].set(b_group_router[i])
        b_route = b_route.at[0, N_GROUPS:N_GROUPS + N_EXPERTS].set(b_expert_router[i])

        xr, gr, q, kc, vc, ks, vs, kw, vw, gates = _proj_in(
            h.reshape(B, S, D), row(mix_norm[i]), wi[:, :o_q].astype(BF16),
            wi[:, o_q:o_kv].astype(BF16), wi[:, o_kv:o_g].astype(BF16), wgate.astype(BF16))

        rec = _rglru(xr, gr, conv_w[i], row(conv_b[i]),
                     _block_diag_tiles(lru_wa[i]).astype(BF16), row(lru_ba[i]),
                     _block_diag_tiles(lru_wx[i]).astype(BF16), row(lru_bx[i]),
                     row(lru_lambda[i]), row(rec_out_norm[i]))

        nb = S // STRIDE_CMP
        kcmp, vcmp = _compress(
            kc.reshape(B, N_KV, nb, half), vc.reshape(B, N_KV, nb, half),
            cmp_pe_k[i].reshape(2, half), cmp_pe_v[i].reshape(2, half),
            cmp_wk1[i].reshape(2, half, -1).astype(BF16), row(cmp_bk1[i]),
            cmp_wk2[i].astype(BF16), row(cmp_bk2[i]),
            cmp_wv1[i].reshape(2, half, -1).astype(BF16), row(cmp_bv1[i]),
            cmp_wv2[i].astype(BF16), row(cmp_bv2[i]))

        att = _nsa(q, kcmp, vcmp, ovl, ks, vs, kw, vw, gates)

        wo = w_out[i].astype(BF16)
        h2, u, route = _out_proj(
            h.reshape(N, D), rec.reshape(N, D_REC), att.reshape(N, D_ATT), row(att_out_norm[i]),
            wo[:D_REC], wo[D_REC:], row(ffn_norm[i]), w_route.astype(BF16), b_route)

        buf_tok, buf_gate, chunk_e, dest = _dispatch(route, N)
        xs = jnp.take(u, buf_tok, axis=0)
        y = _experts(chunk_e, xs, buf_gate[:, None], w_gate_exp[i], w_up_exp[i], w_down_exp[i])
        y0 = jnp.take(y, dest[:, 0], axis=0)
        y1 = jnp.take(y, dest[:, 1], axis=0)

        last = i == depth - 1
        h = _ple(h2, y0, y1, p[i].reshape(N, -1), row(ple_norm[i]), w_ple_gate[i].astype(BF16),
                 row(b_ple_gate[i]), w_ple_up[i].astype(BF16), row(final_norm), last)
    return h.reshape(B, S, D)
```

```python
import functools

import jax
import jax.numpy as jnp
from jax import lax
from jax.experimental import pallas as pl
from jax.experimental.pallas import tpu as pltpu

F32 = jnp.float32
BF16 = jnp.bfloat16

D_REC = 512
REC_BLOCKS = 8
CONV_W = 4
LRU_C = 8.0
N_HEADS = 8
HEAD_DIM = 64
N_KV = 2
GQA_R = N_HEADS // N_KV
D_ATT = N_HEADS * HEAD_DIM
KV_W = N_KV * HEAD_DIM
L_CMP = 32
STRIDE_CMP = 16
L_SEL = 64
N_SEL = 16
WINDOW = 512
N_GROUPS = 4
EXP_PER_GROUP = 8
N_EXPERTS = N_GROUPS * EXP_PER_GROUP
EPS = 1e-6
NEG = -1e30
BIG = 1e30

LANES = 128
SUBLANES = 8
MXU_DIM = 256
VMEM_LIMIT = 56 * 1024 * 1024

ROW_TILE = 512
SCAN_TILE = 512
Q_TILE = 256
K_TILE = 512
EXPERT_CHUNK = 256


def _cparams(sem):
    return pltpu.CompilerParams(dimension_semantics=sem, vmem_limit_bytes=VMEM_LIMIT)


def _rms(x, g):
    return x * lax.rsqrt(jnp.mean(x * x, axis=-1, keepdims=True) + EPS) * g


def _dot(a, b):
    return jnp.dot(a, b, preferred_element_type=F32)


def _dot_nt(a, b):
    return lax.dot_general(a, b, (((1,), (1,)), ((), ())), preferred_element_type=F32)


def _masked_softmax(s, mask):
    s = jnp.where(mask, s, NEG)
    m = jnp.max(s, axis=-1, keepdims=True)
    e = jnp.where(mask, jnp.exp(s - m), 0.0)
    return e / jnp.maximum(jnp.sum(e, axis=-1, keepdims=True), 1e-30)


def _full(shape):
    n = len(shape)
    return pl.BlockSpec(shape, lambda *_: (0,) * n)


AUG_W = 2 * LANES


def _sel_block_onehot(pos):
    blk = lax.broadcasted_iota(jnp.int32, (pos.shape[0], HEAD_DIM), 1)
    return jnp.where(blk == pos // L_SEL, 1.0, 0.0).astype(BF16)


def _pos_columns(hi, lo):
    lane = lax.broadcasted_iota(jnp.int32, (hi.shape[0], LANES), 1)
    return jnp.where(lane == 0, hi, jnp.where(lane == 1, lo, 0.0)).astype(BF16)


def _proj_in_kernel(h_ref, g_ref, wa_ref, wq_ref, wkv_ref, wg_ref,
                    xr_ref, gr_ref, q_ref, kc_ref, vc_ref, ks_ref, vs_ref, kw_ref, vw_ref,
                    gate_ref):
    u = _rms(h_ref[0], g_ref[...]).astype(BF16)
    za = _dot(u, wa_ref[...])
    xr_ref[0] = za[:, :D_REC]
    gr_ref[0] = za[:, D_REC:]
    zq = (_dot(u, wq_ref[...]) * (HEAD_DIM ** -0.5)).astype(q_ref.dtype)
    for h in range(N_HEADS):
        q_ref[0, h] = zq[:, h * HEAD_DIM:(h + 1) * HEAD_DIM]
    zkv = _dot(u, wkv_ref[...])
    TM = zkv.shape[0]
    pos = pl.program_id(1) * TM + lax.broadcasted_iota(jnp.int32, (TM, 1), 0)
    blk_onehot = _sel_block_onehot(pos)
    no_onehot = jnp.zeros((TM, HEAD_DIM), BF16)
    pos_cols = _pos_columns((pos // L_SEL * L_SEL).astype(F32), (pos % L_SEL).astype(F32))
    for i, ref in enumerate((kc_ref, vc_ref, ks_ref, vs_ref, kw_ref, vw_ref)):
        for g in range(N_KV):
            lo = i * KV_W + g * HEAD_DIM
            piece = zkv[:, lo:lo + HEAD_DIM].astype(ref.dtype)
            if ref is ks_ref:
                piece = jnp.concatenate([piece, blk_onehot, pos_cols], axis=-1)
            elif ref is kw_ref:
                piece = jnp.concatenate([piece, no_onehot, pos_cols], axis=-1)
            ref[0, g] = piece
    gate_ref[0] = jax.nn.sigmoid(_dot(u, wg_ref[...]))


def _proj_in(h, g, wa, wq, wkv, wg):
    B, S, D = h.shape
    TM = min(ROW_TILE, S)
    kv_f32 = jax.ShapeDtypeStruct((B, N_KV, S, HEAD_DIM), F32)
    kv_b16 = jax.ShapeDtypeStruct((B, N_KV, S, HEAD_DIM), BF16)
    kv_aug = jax.ShapeDtypeStruct((B, N_KV, S, AUG_W), BF16)
    kv_spec = pl.BlockSpec((1, N_KV, TM, HEAD_DIM), lambda b, s: (b, 0, s, 0))
    aug_spec = pl.BlockSpec((1, N_KV, TM, AUG_W), lambda b, s: (b, 0, s, 0))
    row = lambda w: pl.BlockSpec((1, TM, w), lambda b, s: (b, s, 0))
    return pl.pallas_call(
        _proj_in_kernel,
        grid=(B, S // TM),
        in_specs=[row(D), _full(g.shape), _full(wa.shape), _full(wq.shape), _full(wkv.shape),
                  _full(wg.shape)],
        out_specs=[row(D_REC), row(D_REC),
                   pl.BlockSpec((1, N_HEADS, TM, HEAD_DIM), lambda b, s: (b, 0, s, 0)),
                   kv_spec, kv_spec, aug_spec, kv_spec, aug_spec, kv_spec, row(LANES)],
        out_shape=[jax.ShapeDtypeStruct((B, S, D_REC), F32),
                   jax.ShapeDtypeStruct((B, S, D_REC), F32),
                   jax.ShapeDtypeStruct((B, N_HEADS, S, HEAD_DIM), BF16),
                   kv_f32, kv_f32, kv_aug, kv_b16, kv_aug, kv_b16,
                   jax.ShapeDtypeStruct((B, S, LANES), F32)],
        compiler_params=_cparams(("parallel", "parallel")),
        name="proj_in",
    )(h, g, wa, wq, wkv, wg)


def _rglru_kernel(xr_ref, gr_ref, cw_ref, cb_ref, wa_ref, ba_ref, wx_ref, bx_ref, lam_ref,
                  gn_ref, out_ref, ext_ref, a_ref, b_ref, carry_ref):
    TS = xr_ref.shape[1]
    C = xr_ref.shape[2]

    @pl.when(pl.program_id(1) == 0)
    def _():
        ext_ref[0:SUBLANES, :] = jnp.zeros((SUBLANES, C), F32)
        carry_ref[...] = jnp.zeros(carry_ref.shape, F32)

    x = xr_ref[0]
    ext_ref[SUBLANES:, :] = x
    xc = cb_ref[...] + cw_ref[CONV_W - 1:CONV_W, :] * x
    for j in range(CONV_W - 1):
        off = SUBLANES - (CONV_W - 1) + j
        xc = xc + cw_ref[j:j + 1, :] * ext_ref[off:off + TS, :]
    ext_ref[0:SUBLANES, :] = x[TS - SUBLANES:, :]

    xcb = xc.astype(BF16)
    nt = C // MXU_DIM
    ra = jnp.concatenate(
        [_dot(xcb[:, k * MXU_DIM:(k + 1) * MXU_DIM], wa_ref[k]) for k in range(nt)], axis=-1)
    rx = jnp.concatenate(
        [_dot(xcb[:, k * MXU_DIM:(k + 1) * MXU_DIM], wx_ref[k]) for k in range(nt)], axis=-1)
    r = jax.nn.sigmoid(ra + ba_ref[...])
    i = jax.nn.sigmoid(rx + bx_ref[...])
    lam = lam_ref[...]
    softplus_neg_lam = jnp.maximum(-lam, 0.0) + jnp.log1p(jnp.exp(-jnp.abs(lam)))
    log_a = -LRU_C * r * softplus_neg_lam
    a_ref[...] = jnp.exp(log_a)
    b_ref[...] = jnp.sqrt(1.0 - jnp.exp(2.0 * log_a)) * (i * xc)

    row = lax.broadcasted_iota(jnp.int32, (SUBLANES, C), 0)

    def group(gi, carry):
        r0 = pl.multiple_of(gi * SUBLANES, SUBLANES)
        a = a_ref[pl.ds(r0, SUBLANES), :]
        b = b_ref[pl.ds(r0, SUBLANES), :]
        d = 1
        while d < SUBLANES:
            keep = row >= d
            a_s = pltpu.roll(a, d, axis=0)
            b_s = pltpu.roll(b, d, axis=0)
            b = jnp.where(keep, a * b_s, 0.0) + b
            a = jnp.where(keep, a * a_s, a)
            d *= 2
        hcur = b + a * carry
        b_ref[pl.ds(r0, SUBLANES), :] = hcur
        return hcur[SUBLANES - 1:SUBLANES, :]

    carry_ref[...] = lax.fori_loop(0, TS // SUBLANES, group, carry_ref[...])

    y = b_ref[...] * jax.nn.gelu(gr_ref[0])
    out_ref[0] = _rms(y, gn_ref[...]).astype(out_ref.dtype)


def _rglru(xr, gr, cw, cb, wa, ba, wx, bx, lam, gn):
    B, S, C = xr.shape
    TS = min(SCAN_TILE, S)
    row = pl.BlockSpec((1, TS, C), lambda b, s: (b, s, 0))
    args = (cw, cb, wa, ba, wx, bx, lam, gn)
    return pl.pallas_call(
        _rglru_kernel,
        grid=(B, S // TS),
        in_specs=[row, row] + [_full(a.shape) for a in args],
        out_specs=row,
        out_shape=jax.ShapeDtypeStruct((B, S, C), BF16),
        scratch_shapes=[pltpu.VMEM((TS + SUBLANES, C), F32), pltpu.VMEM((TS, C), F32),
                        pltpu.VMEM((TS, C), F32), pltpu.VMEM((1, C), F32)],
        compiler_params=_cparams(("parallel", "arbitrary")),
        name="rglru",
    )(xr, gr, *args)


def _compress_kernel(kc_ref, vc_ref, pek_ref, pev_ref, wk1_ref, bk1_ref, wk2_ref, bk2_ref,
                     wv1_ref, bv1_ref, wv2_ref, bv2_ref, ko_ref, vo_ref):
    def one(x_ref, pe_ref, w1_ref, b1_ref, w2_ref, b2_ref, o_ref, augment):
        x = x_ref[0, 0]
        nb = x.shape[0]
        first = _dot((x + pe_ref[0:1, :]).astype(BF16), w1_ref[0])
        second = _dot((x + pe_ref[1:2, :]).astype(BF16), w1_ref[1])
        hid = jax.nn.gelu(first + pltpu.roll(second, nb - 1, axis=0) + b1_ref[...])
        o = (_dot(hid.astype(BF16), w2_ref[...]) + b2_ref[...]).astype(o_ref.dtype)
        if augment:
            n = lax.broadcasted_iota(jnp.int32, (nb, 1), 0)
            per = L_SEL // STRIDE_CMP
            hi = (n // per * L_SEL).astype(F32)
            lo = (n % per * STRIDE_CMP).astype(F32) + (L_CMP - 1) * 0.5
            o = jnp.concatenate([o, jnp.zeros((nb, HEAD_DIM), BF16), _pos_columns(hi, lo)], axis=-1)
        o_ref[0, 0] = o

    one(kc_ref, pek_ref, wk1_ref, bk1_ref, wk2_ref, bk2_ref, ko_ref, True)
    one(vc_ref, pev_ref, wv1_ref, bv1_ref, wv2_ref, bv2_ref, vo_ref, False)


def _compress(kc, vc, pek, pev, wk1, bk1, wk2, bk2, wv1, bv1, wv2, bv2):
    B, G, NB, W = kc.shape
    blk = pl.BlockSpec((1, 1, NB, W), lambda b, g: (b, g, 0, 0))
    oblk = lambda w: pl.BlockSpec((1, 1, NB, w), lambda b, g: (b, g, 0, 0))
    args = (pek, pev, wk1, bk1, wk2, bk2, wv1, bv1, wv2, bv2)
    oshape = lambda w: jax.ShapeDtypeStruct((B, G, NB, w), BF16)
    return pl.pallas_call(
        _compress_kernel,
        grid=(B, G),
        in_specs=[blk, blk] + [_full(a.shape) for a in args],
        out_specs=[oblk(AUG_W), oblk(HEAD_DIM)],
        out_shape=[oshape(AUG_W), oshape(HEAD_DIM)],
        compiler_params=_cparams(("parallel", "parallel")),
        name="compress",
    )(kc, vc, *args)


def _selection_rank(score_t):
    NS, TQ = score_t.shape
    sub = lax.broadcasted_iota(jnp.int32, (SUBLANES, TQ), 0)
    groups = [score_t[v * SUBLANES:(v + 1) * SUBLANES, :] for v in range(NS // SUBLANES)]
    ranks = [jnp.zeros((SUBLANES, TQ), F32) for _ in groups]
    for j in range(NS):
        sj = score_t[j:j + 1, :]
        for v, sv in enumerate(groups):
            if j < v * SUBLANES:
                ahead = sj >= sv
            elif j >= (v + 1) * SUBLANES:
                ahead = sj > sv
            else:
                ahead = (sj > sv) | ((sj == sv) & (sub > j - v * SUBLANES))
            ranks[v] = ranks[v] + jnp.where(ahead, 1.0, 0.0)
    return jnp.concatenate(ranks, axis=0)


def _nsa_kernel(q_ref, kcmp_ref, vcmp_ref, ovlt_ref, ks_ref, vs_ref, kw_ref, vw_ref, gate_ref,
                out_ref, s_ref, mx_ref, l_ref, acc_ref, *, n_top):
    TQ = q_ref.shape[2]
    M = GQA_R * TQ
    NC = kcmp_ref.shape[2]
    S = ks_ref.shape[2]
    NS = S // L_SEL
    g = pl.program_id(1)
    t0 = pl.program_id(2) * TQ
    row = lax.broadcasted_iota(jnp.int32, (M, 1), 0)
    t_i = t0 + (row & (TQ - 1))
    slope = jnp.zeros((M, 1), F32)
    for r in range(GQA_R):
        s_r = jnp.where(g == 0, 2.0 ** -(r + 1), 2.0 ** -(GQA_R + r + 1)).astype(F32)
        slope = jnp.where(row // TQ == r, s_r, slope)
    lane = lax.broadcasted_iota(jnp.int32, (M, LANES), 1)
    slope_cols = jnp.where(lane < 2, slope, 0.0).astype(BF16)
    q4 = jnp.concatenate([q_ref[0, r] for r in range(GQA_R)], axis=0)
    qa0 = jnp.concatenate([q4, jnp.zeros((M, HEAD_DIM), BF16), slope_cols], axis=-1)

    WK = WINDOW + TQ
    start = pl.multiple_of(jnp.maximum(t0 - WINDOW, 0), TQ)
    s = _dot_nt(qa0, kw_ref[0, 0, pl.ds(start, WK), :])
    dw = t_i - (start + lax.broadcasted_iota(jnp.int32, (1, WK), 1))
    s = jnp.where((dw >= 0) & (dw < WINDOW), s, NEG)
    e = jnp.exp(s - jnp.max(s, axis=-1, keepdims=True))
    o_w = (_dot(e.astype(BF16), vw_ref[0, 0, pl.ds(start, WK), :])
           * (1.0 / jnp.sum(e, axis=-1, keepdims=True)))

    n_i = lax.broadcasted_iota(jnp.int32, (1, NC), 1)
    mask_c = (n_i * STRIDE_CMP + (L_CMP - 1)) <= t_i
    s = jnp.where(mask_c, _dot_nt(qa0, kcmp_ref[0, 0]), NEG)
    e = jnp.where(mask_c, jnp.exp(s - jnp.max(s, axis=-1, keepdims=True)), 0.0)
    p = e * (1.0 / jnp.maximum(jnp.sum(e, axis=-1, keepdims=True), 1e-30))
    o_c = _dot(p.astype(BF16), vcmp_ref[0, 0])
    p_sum = p[0:TQ]
    for r in range(1, GQA_R):
        p_sum = p_sum + p[r * TQ:(r + 1) * TQ]

    imp_t = lax.dot_general(ovlt_ref[...], p_sum, (((1,), (1,)), ((), ())),
                            preferred_element_type=F32, precision=lax.Precision.HIGHEST)
    blk = lax.broadcasted_iota(jnp.int32, (NS, TQ), 0)
    tq = t0 + lax.broadcasted_iota(jnp.int32, (NS, TQ), 1)
    cur = tq // L_SEL
    valid = (blk * L_SEL) <= tq
    forced = valid & ((blk == 0) | (blk == cur) | (blk == cur - 1))
    score_t = jnp.where(forced, BIG, jnp.where(valid, imp_t, NEG))
    unsel_t = jnp.where(_selection_rank(score_t) < n_top, 0.0, 1.0).astype(BF16)
    eye = jnp.where(lax.broadcasted_iota(jnp.int32, (TQ, TQ), 0)
                    == lax.broadcasted_iota(jnp.int32, (TQ, TQ), 1), 1.0, 0.0).astype(BF16)
    bias = (_dot_nt(eye, unsel_t) * NEG).astype(BF16)
    if NS < HEAD_DIM:
        bias = jnp.concatenate([bias, jnp.zeros((TQ, HEAD_DIM - NS), BF16)], axis=-1)
    qa = jnp.concatenate([q4, jnp.concatenate([bias] * GQA_R, axis=0), slope_cols], axis=-1)

    n_lane_tiles = K_TILE // LANES
    jd = t0 // K_TILE

    def lane_fold(x, op):
        out = x[:, 0:LANES]
        for c in range(1, n_lane_tiles):
            out = op(out, x[:, c * LANES:(c + 1) * LANES])
        return out

    def keys(j):
        return ks_ref[0, 0, pl.ds(pl.multiple_of(j * K_TILE, K_TILE), K_TILE), :]

    pos = jd * K_TILE + lax.broadcasted_iota(jnp.int32, (1, K_TILE), 1)
    s = jnp.where(pos <= t_i, _dot_nt(qa, keys(jd)), NEG)
    s_ref[jd] = s
    mx_ref[...] = lane_fold(s, jnp.maximum)

    def qk_tile(j, c):
        s = _dot_nt(qa, keys(j))
        s_ref[j] = s
        mx_ref[...] = jnp.maximum(mx_ref[...], lane_fold(s, jnp.maximum))
        return c

    lax.fori_loop(0, jd, qk_tile, 0)
    mx_ref[...] = jnp.broadcast_to(jnp.max(mx_ref[...], axis=-1, keepdims=True), (M, LANES))
    l_ref[...] = jnp.zeros((M, LANES), F32)
    acc_ref[...] = jnp.zeros((M, HEAD_DIM), F32)

    def pv_tile(j, c):
        mb = mx_ref[...]
        p = jnp.exp(s_ref[j] - jnp.concatenate([mb] * n_lane_tiles, axis=-1))
        l_ref[...] += lane_fold(p, jnp.add)
        k0 = pl.multiple_of(j * K_TILE, K_TILE)
        acc_ref[...] += _dot(p.astype(BF16), vs_ref[0, 0, pl.ds(k0, K_TILE), :])
        return c

    lax.fori_loop(0, jd + 1, pv_tile, 0)
    o_s = acc_ref[...] * (1.0 / jnp.sum(l_ref[...], axis=-1, keepdims=True))

    gl = gate_ref[0]
    gsel = jnp.where(g == 0, gl[:, :LANES // 2], gl[:, LANES // 2:])
    outs = []
    for r in range(GQA_R):
        rows = slice(r * TQ, (r + 1) * TQ)
        outs.append(gsel[:, 3 * r:3 * r + 1] * o_c[rows] + gsel[:, 3 * r + 1:3 * r + 2] * o_s[rows]
                    + gsel[:, 3 * r + 2:3 * r + 3] * o_w[rows])
    out_ref[0] = jnp.concatenate(outs, axis=-1).astype(out_ref.dtype)


def _nsa(q, kcmp, vcmp, ovlt, ks, vs, kw, vw, gates):
    B, _, S, _ = q.shape
    NC = kcmp.shape[2]
    TQ = min(Q_TILE, S)
    assert TQ & (TQ - 1) == 0 and S // L_SEL <= HEAD_DIM and S >= WINDOW + TQ
    n_top = min(N_SEL, S // L_SEL)
    M = GQA_R * TQ
    cmp_spec = lambda w: pl.BlockSpec((1, 1, NC, w), lambda b, g, i: (b, g, 0, 0))
    seq_spec = lambda w: pl.BlockSpec((1, 1, S, w), lambda b, g, i: (b, g, 0, 0))
    return pl.pallas_call(
        functools.partial(_nsa_kernel, n_top=n_top),
        grid=(B, N_KV, S // TQ),
        in_specs=[pl.BlockSpec((1, GQA_R, TQ, HEAD_DIM), lambda b, g, i: (b, g, i, 0)),
                  cmp_spec(AUG_W), cmp_spec(HEAD_DIM), _full(ovlt.shape),
                  seq_spec(AUG_W), seq_spec(HEAD_DIM), seq_spec(AUG_W), seq_spec(HEAD_DIM),
                  pl.BlockSpec((1, TQ, LANES), lambda b, g, i: (b, i, 0))],
        out_specs=pl.BlockSpec((1, TQ, GQA_R * HEAD_DIM), lambda b, g, i: (b, i, g)),
        out_shape=jax.ShapeDtypeStruct((B, S, D_ATT), F32),
        scratch_shapes=[pltpu.VMEM((pl.cdiv(S, K_TILE), M, K_TILE), F32),
                        pltpu.VMEM((M, LANES), F32), pltpu.VMEM((M, LANES), F32),
                        pltpu.VMEM((M, HEAD_DIM), F32)],
        compiler_params=_cparams(("parallel", "parallel", "arbitrary")),
        name="nsa",
    )(q, kcmp, vcmp, ovlt, ks, vs, kw, vw, gates)


def _out_proj_kernel(h_ref, rec_ref, att_ref, an_ref, wor_ref, woa_ref, fn_ref, wr_ref, br_ref,
                     h2_ref, u_ref, route_ref):
    att = _rms(att_ref[...], an_ref[...]).astype(BF16)
    h2 = h_ref[...] + _dot(rec_ref[...], wor_ref[...]) + _dot(att, woa_ref[...])
    h2_ref[...] = h2
    u = _rms(h2, fn_ref[...])
    ub = u.astype(BF16)
    u_ref[...] = ub
    lg = _dot(ub, wr_ref[...]) + br_ref[...]
    lane = lax.broadcasted_iota(jnp.int32, lg.shape, 1)
    is_grp = lane < N_GROUPS
    gl = jnp.where(is_grp, lg, NEG)
    gm = jnp.max(gl, axis=-1, keepdims=True)
    p_g = 1.0 / jnp.sum(jnp.where(is_grp, jnp.exp(gl - gm), 0.0), axis=-1, keepdims=True)
    g_idx = jnp.min(jnp.where(gl == gm, lane, LANES), axis=-1, keepdims=True)
    lo = N_GROUPS + EXP_PER_GROUP * g_idx
    in_g = (lane >= lo) & (lane < lo + EXP_PER_GROUP)
    el = jnp.where(in_g, lg, NEG)
    v1 = jnp.max(el, axis=-1, keepdims=True)
    i1 = jnp.min(jnp.where(in_g & (el == v1), lane, LANES), axis=-1, keepdims=True)
    rest = in_g & (lane != i1)
    el2 = jnp.where(rest, lg, NEG)
    v2 = jnp.max(el2, axis=-1, keepdims=True)
    i2 = jnp.min(jnp.where(rest & (el2 == v2), lane, LANES), axis=-1, keepdims=True)
    e2 = jnp.exp(v2 - v1)
    den = 1.0 + e2
    route = jnp.where(lane == 0, (i1 - N_GROUPS).astype(F32),
                      jnp.where(lane == 1, (i2 - N_GROUPS).astype(F32),
                                jnp.where(lane == 2, p_g * (1.0 / den),
                                          jnp.where(lane == 3, p_g * (e2 / den), 0.0))))
    route_ref[...] = route


def _out_proj(h, rec, att, an, wor, woa, fn, wr, br):
    N, D = h.shape
    TM = min(ROW_TILE, N)
    row = lambda w: pl.BlockSpec((TM, w), lambda i: (i, 0))
    args = (an, wor, woa, fn, wr, br)
    return pl.pallas_call(
        _out_proj_kernel,
        grid=(N // TM,),
        in_specs=[row(D), row(D_REC), row(D_ATT)] + [_full(a.shape) for a in args],
        out_specs=[row(D), row(D), row(LANES)],
        out_shape=[jax.ShapeDtypeStruct((N, D), F32), jax.ShapeDtypeStruct((N, D), BF16),
                   jax.ShapeDtypeStruct((N, LANES), F32)],
        compiler_params=_cparams(("parallel",)),
        name="out_proj",
    )(h, rec, att, *args)


def _expert_kernel(ce_ref, xs_ref, gate_ref, wg_ref, wu_ref, wd_ref, y_ref, wgb, wub, wdb):
    c = pl.program_id(0)
    prev = ce_ref[jnp.maximum(c - 1, 0)]

    @pl.when((c == 0) | (ce_ref[c] != prev))
    def _():
        wgb[...] = wg_ref[0].astype(BF16)
        wub[...] = wu_ref[0].astype(BF16)
        wdb[...] = wd_ref[0].astype(BF16)

    xs = xs_ref[...]
    hdn = jax.nn.silu(_dot(xs, wgb[...])) * _dot(xs, wub[...])
    y_ref[...] = (_dot(hdn.astype(BF16), wdb[...]) * gate_ref[...]).astype(y_ref.dtype)


def _experts(chunk_e, xs, gates, wg, wu, wd):
    P, D = xs.shape
    DE = wg.shape[-1]
    C = EXPERT_CHUNK
    grid_spec = pltpu.PrefetchScalarGridSpec(
        num_scalar_prefetch=1,
        grid=(P // C,),
        in_specs=[pl.BlockSpec((C, D), lambda c, ce: (c, 0)),
                  pl.BlockSpec((C, 1), lambda c, ce: (c, 0)),
                  pl.BlockSpec((1, D, DE), lambda c, ce: (ce[c], 0, 0)),
                  pl.BlockSpec((1, D, DE), lambda c, ce: (ce[c], 0, 0)),
                  pl.BlockSpec((1, DE, D), lambda c, ce: (ce[c], 0, 0))],
        out_specs=pl.BlockSpec((C, D), lambda c, ce: (c, 0)),
        scratch_shapes=[pltpu.VMEM((D, DE), BF16), pltpu.VMEM((D, DE), BF16),
                        pltpu.VMEM((DE, D), BF16)],
    )
    return pl.pallas_call(
        _expert_kernel,
        grid_spec=grid_spec,
        out_shape=jax.ShapeDtypeStruct((P, D), F32),
        compiler_params=_cparams(("arbitrary",)),
        name="experts",
    )(chunk_e, xs, gates, wg, wu, wd)


def _dispatch(route, n_tok):
    C = EXPERT_CHUNK
    eid = route[:, 0:2].astype(jnp.int32)
    gate = route[:, 2:4]
    A = n_tok * 2
    flat_e = eid.reshape(A)
    flat_tok = jnp.repeat(jnp.arange(n_tok, dtype=jnp.int32), 2)
    onehot = (flat_e[:, None] == jnp.arange(N_EXPERTS, dtype=jnp.int32)[None, :]).astype(jnp.int32)
    within = jnp.take_along_axis(jnp.cumsum(onehot, axis=0), flat_e[:, None], axis=1)[:, 0] - 1
    counts = jnp.sum(onehot, axis=0)
    padded = ((counts + C - 1) // C) * C
    pstarts = jnp.cumsum(padded) - padded
    dest = pstarts[flat_e] + within
    n_chunks = -(-A // C) + N_EXPERTS
    P = n_chunks * C
    buf_tok = jnp.zeros((P,), jnp.int32).at[dest].set(flat_tok)
    buf_gate = jnp.zeros((P,), F32).at[dest].set(gate.reshape(A))
    chunk_start = jnp.arange(n_chunks, dtype=jnp.int32) * C
    chunk_e = jnp.minimum(jnp.sum((pstarts + padded)[None, :] <= chunk_start[:, None], axis=1),
                          N_EXPERTS - 1).astype(jnp.int32)
    return buf_tok, buf_gate, chunk_e, dest.reshape(n_tok, 2)


def _ple_kernel(h_ref, y0_ref, y1_ref, p_ref, pn_ref, wg_ref, bg_ref, wu_ref, fnorm_ref, o_ref,
                *, final):
    h3 = h_ref[...] + (y0_ref[...] + y1_ref[...])
    gate = jax.nn.sigmoid(_dot(_rms(h3, pn_ref[...]).astype(BF16), wg_ref[...]) + bg_ref[...])
    h4 = h3 + _dot(p_ref[...].astype(BF16), wu_ref[...]) * gate
    o_ref[...] = _rms(h4, fnorm_ref[...]) if final else h4


def _ple(h, y0, y1, p, pn, wg, bg, wu, fnorm, final):
    N, D = h.shape
    TM = min(ROW_TILE, N)
    row = lambda w: pl.BlockSpec((TM, w), lambda i: (i, 0))
    args = (pn, wg, bg, wu, fnorm)
    return pl.pallas_call(
        functools.partial(_ple_kernel, final=final),
        grid=(N // TM,),
        in_specs=[row(D), row(D), row(D), row(p.shape[1])] + [_full(a.shape) for a in args],
        out_specs=row(D),
        out_shape=jax.ShapeDtypeStruct((N, D), F32),
        compiler_params=_cparams(("parallel",)),
        name="ple",
    )(h, y0, y1, p, *args)


def _block_diag_tiles(w):
    nb, k, _ = w.shape
    per = MXU_DIM // k
    tiles = jnp.zeros((nb // per, MXU_DIM, MXU_DIM), w.dtype)
    for b in range(nb):
        t, o = divmod(b, per)
        tiles = tiles.at[t, o * k:(o + 1) * k, o * k:(o + 1) * k].set(w[b])
    return tiles


def _overlap(S):
    n = jnp.arange(S // STRIDE_CMP) * STRIDE_CMP
    m = jnp.arange(S // L_SEL) * L_SEL
    ov = jnp.clip(jnp.minimum(n[:, None] + L_CMP, m[None, :] + L_SEL)
                  - jnp.maximum(n[:, None], m[None, :]), 0)
    return ov.astype(F32) / STRIDE_CMP


def kernel(x, p, mix_norm, w_in, conv_w, conv_b, lru_wa, lru_ba, lru_wx, lru_bx, lru_lambda, cmp_pe_k, cmp_pe_v, cmp_wk1, cmp_bk1, cmp_wk2, cmp_bk2, cmp_wv1, cmp_bv1, cmp_wv2, cmp_bv2, rec_out_norm, att_out_norm, w_out, ffn_norm, w_group_router, b_group_router, w_expert_router, b_expert_router, w_gate_exp, w_up_exp, w_down_exp, ple_norm, w_ple_gate, b_ple_gate, w_ple_up, final_norm):
    B, S, D = x.shape
    depth = w_in.shape[0]
    N = B * S
    half = L_CMP // 2 * HEAD_DIM
    row = lambda v: v.reshape(1, -1)
    ovlt = _overlap(S).T
    o_q = 2 * D_REC
    o_kv = o_q + D_ATT
    o_g = o_kv + 6 * KV_W
    h = x
    for i in range(depth):
        wi = w_in[i]
        wgate = jnp.zeros((D, LANES), F32)
        for g in range(N_KV):
            wgate = wgate.at[:, g * (LANES // 2):g * (LANES // 2) + 3 * GQA_R].set(
                wi[:, o_g + g * 3 * GQA_R:o_g + (g + 1) * 3 * GQA_R])
        w_route = jnp.zeros((D, LANES), F32)
        w_route = w_route.at[:, :N_GROUPS].set(w_group_router[i])
        w_route = w_route.at[:, N_GROUPS:N_GROUPS + N_EXPERTS].set(w_expert_router[i])
        b_route = jnp.zeros((1, LANES), F32)
        b_route = b_route.at[0, :N_GROUPS].set(b_group_router[i])
        b_route = b_route.at[0, N_GROUPS:N_GROUPS + N_EXPERTS].set(b_expert_router[i])

        xr, gr, q, kc, vc, ks, vs, kw, vw, gates = _proj_in(
            h.reshape(B, S, D), row(mix_norm[i]), wi[:, :o_q].astype(BF16),
            wi[:, o_q:o_kv].astype(BF16), wi[:, o_kv:o_g].astype(BF16), wgate.astype(BF16))

        rec = _rglru(xr, gr, conv_w[i], row(conv_b[i]),
                     _block_diag_tiles(lru_wa[i]).astype(BF16), row(lru_ba[i]),
                     _block_diag_tiles(lru_wx[i]).astype(BF16), row(lru_bx[i]),
                     row(lru_lambda[i]), row(rec_out_norm[i]))

        nb = S // STRIDE_CMP
        kcmp, vcmp = _compress(
            kc.reshape(B, N_KV, nb, half), vc.reshape(B, N_KV, nb, half),
            cmp_pe_k[i].reshape(2, half), cmp_pe_v[i].reshape(2, half),
            cmp_wk1[i].reshape(2, half, -1).astype(BF16), row(cmp_bk1[i]),
            cmp_wk2[i].astype(BF16), row(cmp_bk2[i]),
            cmp_wv1[i].reshape(2, half, -1).astype(BF16), row(cmp_bv1[i]),
            cmp_wv2[i].astype(BF16), row(cmp_bv2[i]))

        att = _nsa(q, kcmp, vcmp, ovlt, ks, vs, kw, vw, gates)

        wo = w_out[i].astype(BF16)
        h2, u, route = _out_proj(
            h.reshape(N, D), rec.reshape(N, D_REC), att.reshape(N, D_ATT), row(att_out_norm[i]),
            wo[:D_REC], wo[D_REC:], row(ffn_norm[i]), w_route.astype(BF16), b_route)

        buf_tok, buf_gate, chunk_e, dest = _dispatch(route, N)
        xs = jnp.take(u, buf_tok, axis=0)
        y = _experts(chunk_e, xs, buf_gate[:, None], w_gate_exp[i], w_up_exp[i], w_down_exp[i])
        y0 = jnp.take(y, dest[:, 0], axis=0)
        y1 = jnp.take(y, dest[:, 1], axis=0)

        last = i == depth - 1
        h = _ple(h2, y0, y1, p[i].reshape(N, -1), row(ple_norm[i]), w_ple_gate[i].astype(BF16),
                 row(b_ple_gate[i]), w_ple_up[i].astype(BF16), row(final_norm), last)
    return h.reshape(B, S, D)
```

```python
import functools

import jax
import jax.numpy as jnp
from jax import lax
from jax.experimental import pallas as pl
from jax.experimental.pallas import tpu as pltpu

F32 = jnp.float32
BF16 = jnp.bfloat16

D_REC = 512
REC_BLOCKS = 8
CONV_W = 4
LRU_C = 8.0
N_HEADS = 8
HEAD_DIM = 64
N_KV = 2
GQA_R = N_HEADS // N_KV
D_ATT = N_HEADS * HEAD_DIM
KV_W = N_KV * HEAD_DIM
L_CMP = 32
STRIDE_CMP = 16
L_SEL = 64
N_SEL = 16
WINDOW = 512
N_GROUPS = 4
EXP_PER_GROUP = 8
N_EXPERTS = N_GROUPS * EXP_PER_GROUP
EPS = 1e-6
NEG = -1e30
BIG = 1e30

LANES = 128
SUBLANES = 8
MXU_DIM = 256
VMEM_LIMIT = 56 * 1024 * 1024

ROW_TILE = 512
SCAN_TILE = 512
Q_TILE = 256
K_TILE = 512
EXPERT_CHUNK = 256


def _cparams(sem):
    return pltpu.CompilerParams(dimension_semantics=sem, vmem_limit_bytes=VMEM_LIMIT)


def _rms(x, g):
    return x * lax.rsqrt(jnp.mean(x * x, axis=-1, keepdims=True) + EPS) * g


def _dot(a, b):
    return jnp.dot(a, b, preferred_element_type=F32)


def _dot_nt(a, b):
    return lax.dot_general(a, b, (((1,), (1,)), ((), ())), preferred_element_type=F32)


def _masked_softmax(s, mask):
    s = jnp.where(mask, s, NEG)
    m = jnp.max(s, axis=-1, keepdims=True)
    e = jnp.where(mask, jnp.exp(s - m), 0.0)
    return e / jnp.maximum(jnp.sum(e, axis=-1, keepdims=True), 1e-30)


def _full(shape):
    n = len(shape)
    return pl.BlockSpec(shape, lambda *_: (0,) * n)


AUG_W = 2 * LANES


def _sel_block_onehot(pos):
    blk = lax.broadcasted_iota(jnp.int32, (pos.shape[0], HEAD_DIM), 1)
    return jnp.where(blk == pos // L_SEL, 1.0, 0.0).astype(BF16)


def _pos_columns(hi, lo):
    lane = lax.broadcasted_iota(jnp.int32, (hi.shape[0], LANES), 1)
    return jnp.where(lane == 0, hi, jnp.where(lane == 1, lo, 0.0)).astype(BF16)


def _proj_in_kernel(h_ref, g_ref, wa_ref, wq_ref, wkv_ref, wg_ref,
                    xr_ref, gr_ref, q_ref, kc_ref, vc_ref, ks_ref, vs_ref, kw_ref, vw_ref,
                    gate_ref):
    u = _rms(h_ref[0], g_ref[...]).astype(BF16)
    za = _dot(u, wa_ref[...])
    xr_ref[0] = za[:, :D_REC]
    gr_ref[0] = za[:, D_REC:]
    zq = (_dot(u, wq_ref[...]) * (HEAD_DIM ** -0.5)).astype(q_ref.dtype)
    for h in range(N_HEADS):
        q_ref[0, h] = zq[:, h * HEAD_DIM:(h + 1) * HEAD_DIM]
    zkv = _dot(u, wkv_ref[...])
    TM = zkv.shape[0]
    pos = pl.program_id(1) * TM + lax.broadcasted_iota(jnp.int32, (TM, 1), 0)
    blk_onehot = _sel_block_onehot(pos)
    no_onehot = jnp.zeros((TM, HEAD_DIM), BF16)
    ones_col = jnp.where(lax.broadcasted_iota(jnp.int32, (TM, HEAD_DIM), 1) == 0,
                         1.0, 0.0).astype(BF16)
    pos_cols = _pos_columns((pos // L_SEL * L_SEL).astype(F32), (pos % L_SEL).astype(F32))
    for i, ref in enumerate((kc_ref, vc_ref, ks_ref, vs_ref, kw_ref, vw_ref)):
        for g in range(N_KV):
            lo = i * KV_W + g * HEAD_DIM
            piece = zkv[:, lo:lo + HEAD_DIM].astype(ref.dtype)
            if ref is ks_ref:
                piece = jnp.concatenate([piece, blk_onehot, pos_cols], axis=-1)
            elif ref is kw_ref:
                piece = jnp.concatenate([piece, no_onehot, pos_cols], axis=-1)
            elif ref is vs_ref or ref is vw_ref:
                piece = jnp.concatenate([piece, ones_col], axis=-1)
            ref[0, g] = piece
    gate_ref[0] = jax.nn.sigmoid(_dot(u, wg_ref[...]))


def _proj_in(h, g, wa, wq, wkv, wg):
    B, S, D = h.shape
    TM = min(ROW_TILE, S)
    kv_f32 = jax.ShapeDtypeStruct((B, N_KV, S, HEAD_DIM), F32)
    kv_val = jax.ShapeDtypeStruct((B, N_KV, S, LANES), BF16)
    kv_aug = jax.ShapeDtypeStruct((B, N_KV, S, AUG_W), BF16)
    kv_spec = pl.BlockSpec((1, N_KV, TM, HEAD_DIM), lambda b, s: (b, 0, s, 0))
    val_spec = pl.BlockSpec((1, N_KV, TM, LANES), lambda b, s: (b, 0, s, 0))
    aug_spec = pl.BlockSpec((1, N_KV, TM, AUG_W), lambda b, s: (b, 0, s, 0))
    row = lambda w: pl.BlockSpec((1, TM, w), lambda b, s: (b, s, 0))
    return pl.pallas_call(
        _proj_in_kernel,
        grid=(B, S // TM),
        in_specs=[row(D), _full(g.shape), _full(wa.shape), _full(wq.shape), _full(wkv.shape),
                  _full(wg.shape)],
        out_specs=[row(D_REC), row(D_REC),
                   pl.BlockSpec((1, N_HEADS, TM, HEAD_DIM), lambda b, s: (b, 0, s, 0)),
                   kv_spec, kv_spec, aug_spec, val_spec, aug_spec, val_spec, row(LANES)],
        out_shape=[jax.ShapeDtypeStruct((B, S, D_REC), F32),
                   jax.ShapeDtypeStruct((B, S, D_REC), F32),
                   jax.ShapeDtypeStruct((B, N_HEADS, S, HEAD_DIM), BF16),
                   kv_f32, kv_f32, kv_aug, kv_val, kv_aug, kv_val,
                   jax.ShapeDtypeStruct((B, S, LANES), F32)],
        compiler_params=_cparams(("parallel", "parallel")),
        name="proj_in",
    )(h, g, wa, wq, wkv, wg)


def _rglru_kernel(xr_ref, gr_ref, cw_ref, cb_ref, wa_ref, ba_ref, wx_ref, bx_ref, lam_ref,
                  gn_ref, out_ref, ext_ref, a_ref, b_ref, carry_ref):
    TS = xr_ref.shape[1]
    C = xr_ref.shape[2]

    @pl.when(pl.program_id(1) == 0)
    def _():
        ext_ref[0:SUBLANES, :] = jnp.zeros((SUBLANES, C), F32)
        carry_ref[...] = jnp.zeros(carry_ref.shape, F32)

    x = xr_ref[0]
    ext_ref[SUBLANES:, :] = x
    xc = cb_ref[...] + cw_ref[CONV_W - 1:CONV_W, :] * x
    for j in range(CONV_W - 1):
        off = SUBLANES - (CONV_W - 1) + j
        xc = xc + cw_ref[j:j + 1, :] * ext_ref[off:off + TS, :]
    ext_ref[0:SUBLANES, :] = x[TS - SUBLANES:, :]

    xcb = xc.astype(BF16)
    nt = C // MXU_DIM
    ra = jnp.concatenate(
        [_dot(xcb[:, k * MXU_DIM:(k + 1) * MXU_DIM], wa_ref[k]) for k in range(nt)], axis=-1)
    rx = jnp.concatenate(
        [_dot(xcb[:, k * MXU_DIM:(k + 1) * MXU_DIM], wx_ref[k]) for k in range(nt)], axis=-1)
    r = jax.nn.sigmoid(ra + ba_ref[...])
    i = jax.nn.sigmoid(rx + bx_ref[...])
    lam = lam_ref[...]
    softplus_neg_lam = jnp.maximum(-lam, 0.0) + jnp.log1p(jnp.exp(-jnp.abs(lam)))
    log_a = -LRU_C * r * softplus_neg_lam
    a_ref[...] = jnp.exp(log_a)
    b_ref[...] = jnp.sqrt(1.0 - jnp.exp(2.0 * log_a)) * (i * xc)

    row = lax.broadcasted_iota(jnp.int32, (SUBLANES, C), 0)

    def group(gi, carry):
        r0 = pl.multiple_of(gi * SUBLANES, SUBLANES)
        a = a_ref[pl.ds(r0, SUBLANES), :]
        b = b_ref[pl.ds(r0, SUBLANES), :]
        d = 1
        while d < SUBLANES:
            keep = row >= d
            a_s = pltpu.roll(a, d, axis=0)
            b_s = pltpu.roll(b, d, axis=0)
            b = jnp.where(keep, a * b_s, 0.0) + b
            a = jnp.where(keep, a * a_s, a)
            d *= 2
        hcur = b + a * carry
        b_ref[pl.ds(r0, SUBLANES), :] = hcur
        return hcur[SUBLANES - 1:SUBLANES, :]

    carry_ref[...] = lax.fori_loop(0, TS // SUBLANES, group, carry_ref[...])

    y = b_ref[...] * jax.nn.gelu(gr_ref[0])
    out_ref[0] = _rms(y, gn_ref[...]).astype(out_ref.dtype)


def _rglru(xr, gr, cw, cb, wa, ba, wx, bx, lam, gn):
    B, S, C = xr.shape
    TS = min(SCAN_TILE, S)
    row = pl.BlockSpec((1, TS, C), lambda b, s: (b, s, 0))
    args = (cw, cb, wa, ba, wx, bx, lam, gn)
    return pl.pallas_call(
        _rglru_kernel,
        grid=(B, S // TS),
        in_specs=[row, row] + [_full(a.shape) for a in args],
        out_specs=row,
        out_shape=jax.ShapeDtypeStruct((B, S, C), BF16),
        scratch_shapes=[pltpu.VMEM((TS + SUBLANES, C), F32), pltpu.VMEM((TS, C), F32),
                        pltpu.VMEM((TS, C), F32), pltpu.VMEM((1, C), F32)],
        compiler_params=_cparams(("parallel", "arbitrary")),
        name="rglru",
    )(xr, gr, *args)


def _compress_kernel(kc_ref, vc_ref, pek_ref, pev_ref, wk1_ref, bk1_ref, wk2_ref, bk2_ref,
                     wv1_ref, bv1_ref, wv2_ref, bv2_ref, ko_ref, vo_ref):
    def one(x_ref, pe_ref, w1_ref, b1_ref, w2_ref, b2_ref, o_ref, augment):
        x = x_ref[0, 0]
        nb = x.shape[0]
        first = _dot((x + pe_ref[0:1, :]).astype(BF16), w1_ref[0])
        second = _dot((x + pe_ref[1:2, :]).astype(BF16), w1_ref[1])
        hid = jax.nn.gelu(first + pltpu.roll(second, nb - 1, axis=0) + b1_ref[...])
        o = (_dot(hid.astype(BF16), w2_ref[...]) + b2_ref[...]).astype(o_ref.dtype)
        if augment:
            n = lax.broadcasted_iota(jnp.int32, (nb, 1), 0)
            per = L_SEL // STRIDE_CMP
            hi = (n // per * L_SEL).astype(F32)
            lo = (n % per * STRIDE_CMP).astype(F32) + (L_CMP - 1) * 0.5
            o = jnp.concatenate([o, jnp.zeros((nb, HEAD_DIM), BF16), _pos_columns(hi, lo)], axis=-1)
        o_ref[0, 0] = o

    one(kc_ref, pek_ref, wk1_ref, bk1_ref, wk2_ref, bk2_ref, ko_ref, True)
    one(vc_ref, pev_ref, wv1_ref, bv1_ref, wv2_ref, bv2_ref, vo_ref, False)


def _compress(kc, vc, pek, pev, wk1, bk1, wk2, bk2, wv1, bv1, wv2, bv2):
    B, G, NB, W = kc.shape
    blk = pl.BlockSpec((1, 1, NB, W), lambda b, g: (b, g, 0, 0))
    oblk = lambda w: pl.BlockSpec((1, 1, NB, w), lambda b, g: (b, g, 0, 0))
    args = (pek, pev, wk1, bk1, wk2, bk2, wv1, bv1, wv2, bv2)
    oshape = lambda w: jax.ShapeDtypeStruct((B, G, NB, w), BF16)
    return pl.pallas_call(
        _compress_kernel,
        grid=(B, G),
        in_specs=[blk, blk] + [_full(a.shape) for a in args],
        out_specs=[oblk(AUG_W), oblk(HEAD_DIM)],
        out_shape=[oshape(AUG_W), oshape(HEAD_DIM)],
        compiler_params=_cparams(("parallel", "parallel")),
        name="compress",
    )(kc, vc, *args)


def _normalized(acc):
    return acc[:, :HEAD_DIM] * (1.0 / acc[:, HEAD_DIM:HEAD_DIM + 1])


def _selection_rank(score_t):
    NS, TQ = score_t.shape
    sub = lax.broadcasted_iota(jnp.int32, (SUBLANES, TQ), 0)
    groups = [score_t[v * SUBLANES:(v + 1) * SUBLANES, :] for v in range(NS // SUBLANES)]
    ranks = [jnp.zeros((SUBLANES, TQ), F32) for _ in groups]
    for j in range(NS):
        sj = score_t[j:j + 1, :]
        for v, sv in enumerate(groups):
            if j < v * SUBLANES:
                ahead = sj >= sv
            elif j >= (v + 1) * SUBLANES:
                ahead = sj > sv
            else:
                ahead = (sj > sv) | ((sj == sv) & (sub > j - v * SUBLANES))
            ranks[v] = ranks[v] + jnp.where(ahead, 1.0, 0.0)
    return jnp.concatenate(ranks, axis=0)


def _nsa_kernel(q_ref, kcmp_ref, vcmp_ref, ovlt_ref, ks_ref, vs_ref, kw_ref, vw_ref, gate_ref,
                out_ref, s_ref, mx_ref, acc_ref, *, n_top):
    TQ = q_ref.shape[2]
    M = GQA_R * TQ
    NC = kcmp_ref.shape[2]
    S = ks_ref.shape[2]
    NS = S // L_SEL
    g = pl.program_id(1)
    t0 = pl.program_id(2) * TQ
    row = lax.broadcasted_iota(jnp.int32, (M, 1), 0)
    t_i = t0 + (row & (TQ - 1))
    slope = jnp.zeros((M, 1), F32)
    for r in range(GQA_R):
        s_r = jnp.where(g == 0, 2.0 ** -(r + 1), 2.0 ** -(GQA_R + r + 1)).astype(F32)
        slope = jnp.where(row // TQ == r, s_r, slope)
    lane = lax.broadcasted_iota(jnp.int32, (M, LANES), 1)
    slope_cols = jnp.where(lane < 2, slope, 0.0).astype(BF16)
    q4 = jnp.concatenate([q_ref[0, r] for r in range(GQA_R)], axis=0)
    qa0 = jnp.concatenate([q4, jnp.zeros((M, HEAD_DIM), BF16), slope_cols], axis=-1)

    WK = WINDOW + TQ
    start = pl.multiple_of(jnp.maximum(t0 - WINDOW, 0), TQ)
    s = _dot_nt(qa0, kw_ref[0, 0, pl.ds(start, WK), :])
    dw = t_i - (start + lax.broadcasted_iota(jnp.int32, (1, WK), 1))
    s = jnp.where(dw.astype(jnp.uint32) < WINDOW, s, NEG)
    e = jnp.exp((s - jnp.max(s, axis=-1, keepdims=True)).astype(BF16))
    o_w = _normalized(_dot(e, vw_ref[0, 0, pl.ds(start, WK), :]))

    n_i = lax.broadcasted_iota(jnp.int32, (1, NC), 1)
    mask_c = (n_i * STRIDE_CMP + (L_CMP - 1)) <= t_i
    s = jnp.where(mask_c, _dot_nt(qa0, kcmp_ref[0, 0]), NEG)
    e = jnp.where(mask_c, jnp.exp(s - jnp.max(s, axis=-1, keepdims=True)), 0.0)
    p = e * (1.0 / jnp.maximum(jnp.sum(e, axis=-1, keepdims=True), 1e-30))
    o_c = _dot(p.astype(BF16), vcmp_ref[0, 0])
    p_sum = p[0:TQ]
    for r in range(1, GQA_R):
        p_sum = p_sum + p[r * TQ:(r + 1) * TQ]

    imp_t = lax.dot_general(ovlt_ref[...], p_sum, (((1,), (1,)), ((), ())),
                            preferred_element_type=F32, precision=lax.Precision.HIGHEST)
    blk = lax.broadcasted_iota(jnp.int32, (NS, TQ), 0)
    tq = t0 + lax.broadcasted_iota(jnp.int32, (NS, TQ), 1)
    cur = tq // L_SEL
    valid = (blk * L_SEL) <= tq
    forced = valid & ((blk == 0) | (blk == cur) | (blk == cur - 1))
    score_t = jnp.where(forced, BIG, jnp.where(valid, imp_t, NEG))
    unsel_t = jnp.where(_selection_rank(score_t) < n_top, 0.0, 1.0).astype(BF16)
    eye = jnp.where(lax.broadcasted_iota(jnp.int32, (TQ, TQ), 0)
                    == lax.broadcasted_iota(jnp.int32, (TQ, TQ), 1), 1.0, 0.0).astype(BF16)
    bias = (_dot_nt(eye, unsel_t) * NEG).astype(BF16)
    if NS < HEAD_DIM:
        bias = jnp.concatenate([bias, jnp.zeros((TQ, HEAD_DIM - NS), BF16)], axis=-1)
    qa = jnp.concatenate([q4, jnp.concatenate([bias] * GQA_R, axis=0), slope_cols], axis=-1)

    n_lane_tiles = K_TILE // LANES
    jd = t0 // K_TILE

    def lane_fold(x, op):
        out = x[:, 0:LANES]
        for c in range(1, n_lane_tiles):
            out = op(out, x[:, c * LANES:(c + 1) * LANES])
        return out

    def keys(j):
        return ks_ref[0, 0, pl.ds(pl.multiple_of(j * K_TILE, K_TILE), K_TILE), :]

    pos = jd * K_TILE + lax.broadcasted_iota(jnp.int32, (1, K_TILE), 1)
    s = jnp.where(pos <= t_i, _dot_nt(qa, keys(jd)), NEG)
    s_ref[jd] = s
    mx_ref[...] = lane_fold(s, jnp.maximum)

    def qk_tile(j, c):
        s = _dot_nt(qa, keys(j))
        s_ref[j] = s
        mx_ref[...] = jnp.maximum(mx_ref[...], lane_fold(s, jnp.maximum))
        return c

    lax.fori_loop(0, jd, qk_tile, 0)
    mx_ref[...] = jnp.broadcast_to(jnp.max(mx_ref[...], axis=-1, keepdims=True), (M, LANES))
    acc_ref[...] = jnp.zeros((M, LANES), F32)

    def pv_tile(j, c):
        mb = mx_ref[...]
        p = jnp.exp((s_ref[j] - jnp.concatenate([mb] * n_lane_tiles, axis=-1)).astype(BF16))
        k0 = pl.multiple_of(j * K_TILE, K_TILE)
        acc_ref[...] += _dot(p, vs_ref[0, 0, pl.ds(k0, K_TILE), :])
        return c

    lax.fori_loop(0, jd + 1, pv_tile, 0)
    o_s = _normalized(acc_ref[...])

    gl = gate_ref[0]
    gsel = jnp.where(g == 0, gl[:, :LANES // 2], gl[:, LANES // 2:])
    outs = []
    for r in range(GQA_R):
        rows = slice(r * TQ, (r + 1) * TQ)
        outs.append(gsel[:, 3 * r:3 * r + 1] * o_c[rows] + gsel[:, 3 * r + 1:3 * r + 2] * o_s[rows]
                    + gsel[:, 3 * r + 2:3 * r + 3] * o_w[rows])
    out_ref[0] = jnp.concatenate(outs, axis=-1).astype(out_ref.dtype)


def _nsa(q, kcmp, vcmp, ovlt, ks, vs, kw, vw, gates):
    B, _, S, _ = q.shape
    NC = kcmp.shape[2]
    TQ = min(Q_TILE, S)
    assert TQ & (TQ - 1) == 0 and S // L_SEL <= HEAD_DIM and S >= WINDOW + TQ
    n_top = min(N_SEL, S // L_SEL)
    M = GQA_R * TQ
    cmp_spec = lambda w: pl.BlockSpec((1, 1, NC, w), lambda b, g, i: (b, g, 0, 0))
    seq_spec = lambda w: pl.BlockSpec((1, 1, S, w), lambda b, g, i: (b, g, 0, 0))
    return pl.pallas_call(
        functools.partial(_nsa_kernel, n_top=n_top),
        grid=(B, N_KV, S // TQ),
        in_specs=[pl.BlockSpec((1, GQA_R, TQ, HEAD_DIM), lambda b, g, i: (b, g, i, 0)),
                  cmp_spec(AUG_W), cmp_spec(HEAD_DIM), _full(ovlt.shape),
                  seq_spec(AUG_W), seq_spec(LANES), seq_spec(AUG_W), seq_spec(LANES),
                  pl.BlockSpec((1, TQ, LANES), lambda b, g, i: (b, i, 0))],
        out_specs=pl.BlockSpec((1, TQ, GQA_R * HEAD_DIM), lambda b, g, i: (b, i, g)),
        out_shape=jax.ShapeDtypeStruct((B, S, D_ATT), F32),
        scratch_shapes=[pltpu.VMEM((pl.cdiv(S, K_TILE), M, K_TILE), F32),
                        pltpu.VMEM((M, LANES), F32), pltpu.VMEM((M, LANES), F32)],
        compiler_params=_cparams(("parallel", "parallel", "arbitrary")),
        name="nsa",
    )(q, kcmp, vcmp, ovlt, ks, vs, kw, vw, gates)


def _out_proj_kernel(h_ref, rec_ref, att_ref, an_ref, wor_ref, woa_ref, fn_ref, wr_ref, br_ref,
                     h2_ref, u_ref, route_ref):
    att = _rms(att_ref[...], an_ref[...]).astype(BF16)
    h2 = h_ref[...] + _dot(rec_ref[...], wor_ref[...]) + _dot(att, woa_ref[...])
    h2_ref[...] = h2
    u = _rms(h2, fn_ref[...])
    ub = u.astype(BF16)
    u_ref[...] = ub
    lg = _dot(ub, wr_ref[...]) + br_ref[...]
    lane = lax.broadcasted_iota(jnp.int32, lg.shape, 1)
    is_grp = lane < N_GROUPS
    gl = jnp.where(is_grp, lg, NEG)
    gm = jnp.max(gl, axis=-1, keepdims=True)
    p_g = 1.0 / jnp.sum(jnp.where(is_grp, jnp.exp(gl - gm), 0.0), axis=-1, keepdims=True)
    g_idx = jnp.min(jnp.where(gl == gm, lane, LANES), axis=-1, keepdims=True)
    lo = N_GROUPS + EXP_PER_GROUP * g_idx
    in_g = (lane >= lo) & (lane < lo + EXP_PER_GROUP)
    el = jnp.where(in_g, lg, NEG)
    v1 = jnp.max(el, axis=-1, keepdims=True)
    i1 = jnp.min(jnp.where(in_g & (el == v1), lane, LANES), axis=-1, keepdims=True)
    rest = in_g & (lane != i1)
    el2 = jnp.where(rest, lg, NEG)
    v2 = jnp.max(el2, axis=-1, keepdims=True)
    i2 = jnp.min(jnp.where(rest & (el2 == v2), lane, LANES), axis=-1, keepdims=True)
    e2 = jnp.exp(v2 - v1)
    den = 1.0 + e2
    route = jnp.where(lane == 0, (i1 - N_GROUPS).astype(F32),
                      jnp.where(lane == 1, (i2 - N_GROUPS).astype(F32),
                                jnp.where(lane == 2, p_g * (1.0 / den),
                                          jnp.where(lane == 3, p_g * (e2 / den), 0.0))))
    route_ref[...] = route


def _out_proj(h, rec, att, an, wor, woa, fn, wr, br):
    N, D = h.shape
    TM = min(ROW_TILE, N)
    row = lambda w: pl.BlockSpec((TM, w), lambda i: (i, 0))
    args = (an, wor, woa, fn, wr, br)
    return pl.pallas_call(
        _out_proj_kernel,
        grid=(N // TM,),
        in_specs=[row(D), row(D_REC), row(D_ATT)] + [_full(a.shape) for a in args],
        out_specs=[row(D), row(D), row(LANES)],
        out_shape=[jax.ShapeDtypeStruct((N, D), F32), jax.ShapeDtypeStruct((N, D), BF16),
                   jax.ShapeDtypeStruct((N, LANES), F32)],
        compiler_params=_cparams(("parallel",)),
        name="out_proj",
    )(h, rec, att, *args)


def _expert_kernel(ce_ref, xs_ref, wg_ref, wu_ref, wd_ref, y_ref, wgb, wub, wdb):
    c = pl.program_id(0)
    prev = ce_ref[jnp.maximum(c - 1, 0)]

    @pl.when((c == 0) | (ce_ref[c] != prev))
    def _():
        wgb[...] = wg_ref[0, 0].astype(BF16)
        wub[...] = wu_ref[0, 0].astype(BF16)
        wdb[...] = wd_ref[0, 0].astype(BF16)

    xs = xs_ref[...]
    hdn = jax.nn.silu(_dot(xs, wgb[...])) * _dot(xs, wub[...])
    y_ref[...] = _dot(hdn.astype(BF16), wdb[...]).astype(y_ref.dtype)


def _experts(chunk_e, xs, wg, wu, wd, layer):
    P, D = xs.shape
    DE = wg.shape[-1]
    C = EXPERT_CHUNK
    grid_spec = pltpu.PrefetchScalarGridSpec(
        num_scalar_prefetch=1,
        grid=(P // C,),
        in_specs=[pl.BlockSpec((C, D), lambda c, ce: (c, 0)),
                  pl.BlockSpec((1, 1, D, DE), lambda c, ce: (layer, ce[c], 0, 0)),
                  pl.BlockSpec((1, 1, D, DE), lambda c, ce: (layer, ce[c], 0, 0)),
                  pl.BlockSpec((1, 1, DE, D), lambda c, ce: (layer, ce[c], 0, 0))],
        out_specs=pl.BlockSpec((C, D), lambda c, ce: (c, 0)),
        scratch_shapes=[pltpu.VMEM((D, DE), BF16), pltpu.VMEM((D, DE), BF16),
                        pltpu.VMEM((DE, D), BF16)],
    )
    return pl.pallas_call(
        _expert_kernel,
        grid_spec=grid_spec,
        out_shape=jax.ShapeDtypeStruct((P, D), BF16),
        compiler_params=_cparams(("arbitrary",)),
        name="experts",
    )(chunk_e, xs, wg, wu, wd)


def _slot_kernel(route_ref, rank_ref, count_ref, run_ref):
    TM = route_ref.shape[0]

    @pl.when(pl.program_id(0) == 0)
    def _():
        run_ref[...] = jnp.zeros(run_ref.shape, F32)

    x = route_ref[...]
    lane = lax.broadcasted_iota(jnp.int32, (TM, LANES), 1).astype(F32)
    hot0 = jnp.where(lane == x[:, 0:1], 1.0, 0.0)
    hot1 = jnp.where(lane == x[:, 1:2], 1.0, 0.0)
    earlier = jnp.where(lax.broadcasted_iota(jnp.int32, (TM, TM), 0)
                        > lax.broadcasted_iota(jnp.int32, (TM, TM), 1), 1.0, 0.0).astype(BF16)
    before0 = _dot(earlier, hot0.astype(BF16))
    before1 = _dot(earlier, hot1.astype(BF16))
    tot0 = jnp.sum(hot0, axis=0, keepdims=True)
    run = run_ref[...]
    rank0 = jnp.sum(hot0 * (before0 + run), axis=-1, keepdims=True)
    rank1 = jnp.sum(hot1 * (before1 + (run + tot0)), axis=-1, keepdims=True)
    run = run + tot0 + jnp.sum(hot1, axis=0, keepdims=True)
    run_ref[...] = run
    count_ref[...] = jnp.broadcast_to(run, count_ref.shape)
    ranks = jnp.where(lane == 0.0, rank0, jnp.where(lane == 1.0, rank1, 0.0))
    pick = jnp.where(lax.broadcasted_iota(jnp.int32, (SUBLANES, LANES), 0)
                     == lax.broadcasted_iota(jnp.int32, (SUBLANES, LANES), 1), 1.0, 0.0)
    rank_ref[...] = lax.dot_general(pick, ranks, (((1,), (1,)), ((), ())),
                                    preferred_element_type=F32,
                                    precision=lax.Precision.HIGHEST)


def _slots(route):
    N = route.shape[0]
    TM = min(ROW_TILE, N)
    return pl.pallas_call(
        _slot_kernel,
        grid=(N // TM,),
        in_specs=[pl.BlockSpec((TM, LANES), lambda i: (i, 0))],
        out_specs=[pl.BlockSpec((SUBLANES, TM), lambda i: (0, i)),
                   pl.BlockSpec((SUBLANES, LANES), lambda i: (0, 0))],
        out_shape=[jax.ShapeDtypeStruct((SUBLANES, N), F32),
                   jax.ShapeDtypeStruct((SUBLANES, LANES), F32)],
        scratch_shapes=[pltpu.VMEM((1, LANES), F32)],
        compiler_params=_cparams(("arbitrary",)),
        name="slots",
    )(route)


def _dispatch(route, n_tok):
    C = EXPERT_CHUNK
    ranks, totals = _slots(route)
    counts = totals[0, :N_EXPERTS].astype(jnp.int32)
    padded = ((counts + C - 1) // C) * C
    pstarts = jnp.cumsum(padded) - padded
    e0 = route[:, 0].astype(jnp.int32)
    e1 = route[:, 1].astype(jnp.int32)
    dest0 = pstarts[e0] + ranks[0].astype(jnp.int32)
    dest1 = pstarts[e1] + ranks[1].astype(jnp.int32)
    n_chunks = -(-2 * n_tok // C) + N_EXPERTS
    tok = jnp.arange(n_tok, dtype=jnp.int32)
    buf_tok = jnp.zeros((n_chunks * C,), jnp.int32).at[jnp.concatenate([dest0, dest1])].set(
        jnp.concatenate([tok, tok]), unique_indices=True)
    chunk_start = jnp.arange(n_chunks, dtype=jnp.int32) * C
    chunk_e = jnp.minimum(jnp.sum((pstarts + padded)[None, :] <= chunk_start[:, None], axis=1),
                          N_EXPERTS - 1).astype(jnp.int32)
    return buf_tok, chunk_e, dest0, dest1


def _ple_kernel(h_ref, y0_ref, y1_ref, route_ref, p_ref, pn_ref, wg_ref, bg_ref, wu_ref,
                fnorm_ref, o_ref, *, final):
    r = route_ref[...]
    h3 = h_ref[...] + (r[:, 2:3] * y0_ref[...].astype(F32) + r[:, 3:4] * y1_ref[...].astype(F32))
    gate = jax.nn.sigmoid(_dot(_rms(h3, pn_ref[...]).astype(BF16), wg_ref[...]) + bg_ref[...])
    h4 = h3 + _dot(p_ref[0].astype(BF16), wu_ref[...]) * gate
    o_ref[...] = _rms(h4, fnorm_ref[...]) if final else h4


def _ple(h, y0, y1, route, p, layer, pn, wg, bg, wu, fnorm, final):
    N, D = h.shape
    TM = min(ROW_TILE, N)
    row = lambda w: pl.BlockSpec((TM, w), lambda i: (i, 0))
    args = (pn, wg, bg, wu, fnorm)
    return pl.pallas_call(
        functools.partial(_ple_kernel, final=final),
        grid=(N // TM,),
        in_specs=[row(D), row(D), row(D), row(LANES),
                  pl.BlockSpec((1, TM, p.shape[2]), lambda i: (layer, i, 0))]
        + [_full(a.shape) for a in args],
        out_specs=row(D),
        out_shape=jax.ShapeDtypeStruct((N, D), F32),
        compiler_params=_cparams(("parallel",)),
        name="ple",
    )(h, y0, y1, route, p, *args)


def _block_diag_tiles(w):
    nb, k, _ = w.shape
    per = MXU_DIM // k
    tiles = jnp.zeros((nb // per, MXU_DIM, MXU_DIM), w.dtype)
    for b in range(nb):
        t, o = divmod(b, per)
        tiles = tiles.at[t, o * k:(o + 1) * k, o * k:(o + 1) * k].set(w[b])
    return tiles


def _overlap(S):
    n = jnp.arange(S // STRIDE_CMP) * STRIDE_CMP
    m = jnp.arange(S // L_SEL) * L_SEL
    ov = jnp.clip(jnp.minimum(n[:, None] + L_CMP, m[None, :] + L_SEL)
                  - jnp.maximum(n[:, None], m[None, :]), 0)
    return ov.astype(F32) / STRIDE_CMP


def kernel(x, p, mix_norm, w_in, conv_w, conv_b, lru_wa, lru_ba, lru_wx, lru_bx, lru_lambda, cmp_pe_k, cmp_pe_v, cmp_wk1, cmp_bk1, cmp_wk2, cmp_bk2, cmp_wv1, cmp_bv1, cmp_wv2, cmp_bv2, rec_out_norm, att_out_norm, w_out, ffn_norm, w_group_router, b_group_router, w_expert_router, b_expert_router, w_gate_exp, w_up_exp, w_down_exp, ple_norm, w_ple_gate, b_ple_gate, w_ple_up, final_norm):
    B, S, D = x.shape
    depth = w_in.shape[0]
    N = B * S
    half = L_CMP // 2 * HEAD_DIM
    row = lambda v: v.reshape(1, -1)
    ovlt = _overlap(S).T
    p_flat = p.reshape(depth, N, -1)
    o_q = 2 * D_REC
    o_kv = o_q + D_ATT
    o_g = o_kv + 6 * KV_W
    h = x
    for i in range(depth):
        wi = w_in[i]
        wgate = jnp.zeros((D, LANES), F32)
        for g in range(N_KV):
            wgate = wgate.at[:, g * (LANES // 2):g * (LANES // 2) + 3 * GQA_R].set(
                wi[:, o_g + g * 3 * GQA_R:o_g + (g + 1) * 3 * GQA_R])
        w_route = jnp.zeros((D, LANES), F32)
        w_route = w_route.at[:, :N_GROUPS].set(w_group_router[i])
        w_route = w_route.at[:, N_GROUPS:N_GROUPS + N_EXPERTS].set(w_expert_router[i])
        b_route = jnp.zeros((1, LANES), F32)
        b_route = b_route.at[0, :N_GROUPS].set(b_group_router[i])
        b_route = b_route.at[0, N_GROUPS:N_GROUPS + N_EXPERTS].set(b_expert_router[i])

        xr, gr, q, kc, vc, ks, vs, kw, vw, gates = _proj_in(
            h.reshape(B, S, D), row(mix_norm[i]), wi[:, :o_q].astype(BF16),
            wi[:, o_q:o_kv].astype(BF16), wi[:, o_kv:o_g].astype(BF16), wgate.astype(BF16))

        rec = _rglru(xr, gr, conv_w[i], row(conv_b[i]),
                     _block_diag_tiles(lru_wa[i]).astype(BF16), row(lru_ba[i]),
                     _block_diag_tiles(lru_wx[i]).astype(BF16), row(lru_bx[i]),
                     row(lru_lambda[i]), row(rec_out_norm[i]))

        nb = S // STRIDE_CMP
        kcmp, vcmp = _compress(
            kc.reshape(B, N_KV, nb, half), vc.reshape(B, N_KV, nb, half),
            cmp_pe_k[i].reshape(2, half), cmp_pe_v[i].reshape(2, half),
            cmp_wk1[i].reshape(2, half, -1).astype(BF16), row(cmp_bk1[i]),
            cmp_wk2[i].astype(BF16), row(cmp_bk2[i]),
            cmp_wv1[i].reshape(2, half, -1).astype(BF16), row(cmp_bv1[i]),
            cmp_wv2[i].astype(BF16), row(cmp_bv2[i]))

        att = _nsa(q, kcmp, vcmp, ovlt, ks, vs, kw, vw, gates)

        wo = w_out[i].astype(BF16)
        h2, u, route = _out_proj(
            h.reshape(N, D), rec.reshape(N, D_REC), att.reshape(N, D_ATT), row(att_out_norm[i]),
            wo[:D_REC], wo[D_REC:], row(ffn_norm[i]), w_route.astype(BF16), b_route)

        buf_tok, chunk_e, dest0, dest1 = _dispatch(route, N)
        xs = jnp.take(u, buf_tok, axis=0)
        y = _experts(chunk_e, xs, w_gate_exp, w_up_exp, w_down_exp, i)
        y0 = jnp.take(y, dest0, axis=0)
        y1 = jnp.take(y, dest1, axis=0)

        last = i == depth - 1
        h = _ple(h2, y0, y1, route, p_flat, i, row(ple_norm[i]), w_ple_gate[i].astype(BF16),
                 row(b_ple_gate[i]), w_ple_up[i].astype(BF16), row(final_norm), last)
    return h.reshape(B, S, D)
```

```python
import functools

import jax
import jax.numpy as jnp
from jax import lax
from jax.experimental import pallas as pl
from jax.experimental.pallas import tpu as pltpu

F32 = jnp.float32
BF16 = jnp.bfloat16

D_REC = 512
REC_BLOCKS = 8
CONV_W = 4
LRU_C = 8.0
N_HEADS = 8
HEAD_DIM = 64
N_KV = 2
GQA_R = N_HEADS // N_KV
D_ATT = N_HEADS * HEAD_DIM
KV_W = N_KV * HEAD_DIM
L_CMP = 32
STRIDE_CMP = 16
L_SEL = 64
N_SEL = 16
WINDOW = 512
N_GROUPS = 4
EXP_PER_GROUP = 8
N_EXPERTS = N_GROUPS * EXP_PER_GROUP
EPS = 1e-6
NEG = -1e30
BIG = 1e30

LANES = 128
SUBLANES = 8
MXU_DIM = 256
VMEM_LIMIT = 56 * 1024 * 1024

ROW_TILE = 512
SCAN_TILE = 512
Q_TILE = 256
K_TILE = 512
EXPERT_CHUNK = 256
MOE_PIECES = 4


def _cparams(sem):
    return pltpu.CompilerParams(dimension_semantics=sem, vmem_limit_bytes=VMEM_LIMIT)


def _rms(x, g):
    return x * lax.rsqrt(jnp.mean(x * x, axis=-1, keepdims=True) + EPS) * g


def _dot(a, b):
    return jnp.dot(a, b, preferred_element_type=F32)


def _dot_nt(a, b):
    return lax.dot_general(a, b, (((1,), (1,)), ((), ())), preferred_element_type=F32)


def _masked_softmax(s, mask):
    s = jnp.where(mask, s, NEG)
    m = jnp.max(s, axis=-1, keepdims=True)
    e = jnp.where(mask, jnp.exp(s - m), 0.0)
    return e / jnp.maximum(jnp.sum(e, axis=-1, keepdims=True), 1e-30)


def _full(shape):
    n = len(shape)
    return pl.BlockSpec(shape, lambda *_: (0,) * n)


AUG_W = 2 * LANES


def _sel_block_onehot(pos):
    blk = lax.broadcasted_iota(jnp.int32, (pos.shape[0], HEAD_DIM), 1)
    return jnp.where(blk == pos // L_SEL, 1.0, 0.0).astype(BF16)


def _pos_columns(hi, lo):
    lane = lax.broadcasted_iota(jnp.int32, (hi.shape[0], LANES), 1)
    return jnp.where(lane == 0, hi, jnp.where(lane == 1, lo, 0.0)).astype(BF16)


def _proj_in_kernel(h_ref, g_ref, wa_ref, wq_ref, wkv_ref, wg_ref,
                    xr_ref, gr_ref, q_ref, kc_ref, vc_ref, ks_ref, vs_ref, kw_ref, vw_ref,
                    gate_ref):
    u = _rms(h_ref[0], g_ref[...]).astype(BF16)
    za = _dot(u, wa_ref[...])
    xr_ref[0] = za[:, :D_REC]
    gr_ref[0] = za[:, D_REC:]
    zq = (_dot(u, wq_ref[...]) * (HEAD_DIM ** -0.5)).astype(q_ref.dtype)
    for h in range(N_HEADS):
        q_ref[0, h] = zq[:, h * HEAD_DIM:(h + 1) * HEAD_DIM]
    zkv = _dot(u, wkv_ref[...])
    TM = zkv.shape[0]
    pos = pl.program_id(1) * TM + lax.broadcasted_iota(jnp.int32, (TM, 1), 0)
    blk_onehot = _sel_block_onehot(pos)
    no_onehot = jnp.zeros((TM, HEAD_DIM), BF16)
    ones_col = jnp.where(lax.broadcasted_iota(jnp.int32, (TM, HEAD_DIM), 1) == 0,
                         1.0, 0.0).astype(BF16)
    pos_cols = _pos_columns((pos // L_SEL * L_SEL).astype(F32), (pos % L_SEL).astype(F32))
    for i, ref in enumerate((kc_ref, vc_ref, ks_ref, vs_ref, kw_ref, vw_ref)):
        for g in range(N_KV):
            lo = i * KV_W + g * HEAD_DIM
            piece = zkv[:, lo:lo + HEAD_DIM].astype(ref.dtype)
            if ref is ks_ref:
                piece = jnp.concatenate([piece, blk_onehot, pos_cols], axis=-1)
            elif ref is kw_ref:
                piece = jnp.concatenate([piece, no_onehot, pos_cols], axis=-1)
            elif ref is vs_ref or ref is vw_ref:
                piece = jnp.concatenate([piece, ones_col], axis=-1)
            ref[0, g] = piece
    gate_ref[0] = jax.nn.sigmoid(_dot(u, wg_ref[...]))


def _proj_in(h, g, wa, wq, wkv, wg):
    B, S, D = h.shape
    TM = min(ROW_TILE, S)
    kv_f32 = jax.ShapeDtypeStruct((B, N_KV, S, HEAD_DIM), F32)
    kv_val = jax.ShapeDtypeStruct((B, N_KV, S, LANES), BF16)
    kv_aug = jax.ShapeDtypeStruct((B, N_KV, S, AUG_W), BF16)
    kv_spec = pl.BlockSpec((1, N_KV, TM, HEAD_DIM), lambda b, s: (b, 0, s, 0))
    val_spec = pl.BlockSpec((1, N_KV, TM, LANES), lambda b, s: (b, 0, s, 0))
    aug_spec = pl.BlockSpec((1, N_KV, TM, AUG_W), lambda b, s: (b, 0, s, 0))
    row = lambda w: pl.BlockSpec((1, TM, w), lambda b, s: (b, s, 0))
    return pl.pallas_call(
        _proj_in_kernel,
        grid=(B, S // TM),
        in_specs=[row(D), _full(g.shape), _full(wa.shape), _full(wq.shape), _full(wkv.shape),
                  _full(wg.shape)],
        out_specs=[row(D_REC), row(D_REC),
                   pl.BlockSpec((1, N_HEADS, TM, HEAD_DIM), lambda b, s: (b, 0, s, 0)),
                   kv_spec, kv_spec, aug_spec, val_spec, aug_spec, val_spec, row(LANES)],
        out_shape=[jax.ShapeDtypeStruct((B, S, D_REC), F32),
                   jax.ShapeDtypeStruct((B, S, D_REC), F32),
                   jax.ShapeDtypeStruct((B, N_HEADS, S, HEAD_DIM), BF16),
                   kv_f32, kv_f32, kv_aug, kv_val, kv_aug, kv_val,
                   jax.ShapeDtypeStruct((B, S, LANES), F32)],
        compiler_params=_cparams(("parallel", "parallel")),
        name="proj_in",
    )(h, g, wa, wq, wkv, wg)


def _rglru_kernel(xr_ref, gr_ref, cw_ref, cb_ref, wa_ref, ba_ref, wx_ref, bx_ref, lam_ref,
                  gn_ref, out_ref, ext_ref, a_ref, b_ref, carry_ref):
    TS = xr_ref.shape[1]
    C = xr_ref.shape[2]

    @pl.when(pl.program_id(1) == 0)
    def _():
        ext_ref[0:SUBLANES, :] = jnp.zeros((SUBLANES, C), F32)
        carry_ref[...] = jnp.zeros(carry_ref.shape, F32)

    x = xr_ref[0]
    ext_ref[SUBLANES:, :] = x
    xc = cb_ref[...] + cw_ref[CONV_W - 1:CONV_W, :] * x
    for j in range(CONV_W - 1):
        off = SUBLANES - (CONV_W - 1) + j
        xc = xc + cw_ref[j:j + 1, :] * ext_ref[off:off + TS, :]
    ext_ref[0:SUBLANES, :] = x[TS - SUBLANES:, :]

    xcb = xc.astype(BF16)
    nt = C // MXU_DIM
    ra = jnp.concatenate(
        [_dot(xcb[:, k * MXU_DIM:(k + 1) * MXU_DIM], wa_ref[k]) for k in range(nt)], axis=-1)
    rx = jnp.concatenate(
        [_dot(xcb[:, k * MXU_DIM:(k + 1) * MXU_DIM], wx_ref[k]) for k in range(nt)], axis=-1)
    r = jax.nn.sigmoid(ra + ba_ref[...])
    i = jax.nn.sigmoid(rx + bx_ref[...])
    lam = lam_ref[...]
    softplus_neg_lam = jnp.maximum(-lam, 0.0) + jnp.log1p(jnp.exp(-jnp.abs(lam)))
    log_a = -LRU_C * r * softplus_neg_lam
    a_ref[...] = jnp.exp(log_a)
    b_ref[...] = jnp.sqrt(1.0 - jnp.exp(2.0 * log_a)) * (i * xc)

    row = lax.broadcasted_iota(jnp.int32, (SUBLANES, C), 0)

    def group(gi, carry):
        r0 = pl.multiple_of(gi * SUBLANES, SUBLANES)
        a = a_ref[pl.ds(r0, SUBLANES), :]
        b = b_ref[pl.ds(r0, SUBLANES), :]
        d = 1
        while d < SUBLANES:
            keep = row >= d
            a_s = pltpu.roll(a, d, axis=0)
            b_s = pltpu.roll(b, d, axis=0)
            b = jnp.where(keep, a * b_s, 0.0) + b
            a = jnp.where(keep, a * a_s, a)
            d *= 2
        hcur = b + a * carry
        b_ref[pl.ds(r0, SUBLANES), :] = hcur
        return hcur[SUBLANES - 1:SUBLANES, :]

    carry_ref[...] = lax.fori_loop(0, TS // SUBLANES, group, carry_ref[...])

    y = b_ref[...] * jax.nn.gelu(gr_ref[0])
    out_ref[0] = _rms(y, gn_ref[...]).astype(out_ref.dtype)


def _rglru(xr, gr, cw, cb, wa, ba, wx, bx, lam, gn):
    B, S, C = xr.shape
    TS = min(SCAN_TILE, S)
    row = pl.BlockSpec((1, TS, C), lambda b, s: (b, s, 0))
    args = (cw, cb, wa, ba, wx, bx, lam, gn)
    return pl.pallas_call(
        _rglru_kernel,
        grid=(B, S // TS),
        in_specs=[row, row] + [_full(a.shape) for a in args],
        out_specs=row,
        out_shape=jax.ShapeDtypeStruct((B, S, C), BF16),
        scratch_shapes=[pltpu.VMEM((TS + SUBLANES, C), F32), pltpu.VMEM((TS, C), F32),
                        pltpu.VMEM((TS, C), F32), pltpu.VMEM((1, C), F32)],
        compiler_params=_cparams(("parallel", "arbitrary")),
        name="rglru",
    )(xr, gr, *args)


def _compress_kernel(kc_ref, vc_ref, pek_ref, pev_ref, wk1_ref, bk1_ref, wk2_ref, bk2_ref,
                     wv1_ref, bv1_ref, wv2_ref, bv2_ref, ko_ref, vo_ref):
    def one(x_ref, pe_ref, w1_ref, b1_ref, w2_ref, b2_ref, o_ref, augment):
        x = x_ref[0, 0]
        nb = x.shape[0]
        first = _dot((x + pe_ref[0:1, :]).astype(BF16), w1_ref[0])
        second = _dot((x + pe_ref[1:2, :]).astype(BF16), w1_ref[1])
        hid = jax.nn.gelu(first + pltpu.roll(second, nb - 1, axis=0) + b1_ref[...])
        o = (_dot(hid.astype(BF16), w2_ref[...]) + b2_ref[...]).astype(o_ref.dtype)
        if augment:
            n = lax.broadcasted_iota(jnp.int32, (nb, 1), 0)
            per = L_SEL // STRIDE_CMP
            hi = (n // per * L_SEL).astype(F32)
            lo = (n % per * STRIDE_CMP).astype(F32) + (L_CMP - 1) * 0.5
            o = jnp.concatenate([o, jnp.zeros((nb, HEAD_DIM), BF16), _pos_columns(hi, lo)], axis=-1)
        o_ref[0, 0] = o

    one(kc_ref, pek_ref, wk1_ref, bk1_ref, wk2_ref, bk2_ref, ko_ref, True)
    one(vc_ref, pev_ref, wv1_ref, bv1_ref, wv2_ref, bv2_ref, vo_ref, False)


def _compress(kc, vc, pek, pev, wk1, bk1, wk2, bk2, wv1, bv1, wv2, bv2):
    B, G, NB, W = kc.shape
    blk = pl.BlockSpec((1, 1, NB, W), lambda b, g: (b, g, 0, 0))
    oblk = lambda w: pl.BlockSpec((1, 1, NB, w), lambda b, g: (b, g, 0, 0))
    args = (pek, pev, wk1, bk1, wk2, bk2, wv1, bv1, wv2, bv2)
    oshape = lambda w: jax.ShapeDtypeStruct((B, G, NB, w), BF16)
    return pl.pallas_call(
        _compress_kernel,
        grid=(B, G),
        in_specs=[blk, blk] + [_full(a.shape) for a in args],
        out_specs=[oblk(AUG_W), oblk(HEAD_DIM)],
        out_shape=[oshape(AUG_W), oshape(HEAD_DIM)],
        compiler_params=_cparams(("parallel", "parallel")),
        name="compress",
    )(kc, vc, *args)


def _normalized(acc):
    return acc[:, :HEAD_DIM] * (1.0 / acc[:, HEAD_DIM:HEAD_DIM + 1])


def _selection_rank(score_t):
    NS, TQ = score_t.shape
    sub = lax.broadcasted_iota(jnp.int32, (SUBLANES, TQ), 0)
    groups = [score_t[v * SUBLANES:(v + 1) * SUBLANES, :] for v in range(NS // SUBLANES)]
    ranks = [jnp.zeros((SUBLANES, TQ), F32) for _ in groups]
    for j in range(NS):
        sj = score_t[j:j + 1, :]
        for v, sv in enumerate(groups):
            if j < v * SUBLANES:
                ahead = sj >= sv
            elif j >= (v + 1) * SUBLANES:
                ahead = sj > sv
            else:
                ahead = (sj > sv) | ((sj == sv) & (sub > j - v * SUBLANES))
            ranks[v] = ranks[v] + jnp.where(ahead, 1.0, 0.0)
    return jnp.concatenate(ranks, axis=0)


def _nsa_kernel(q_ref, kcmp_ref, vcmp_ref, ovlt_ref, ks_ref, vs_ref, kw_ref, vw_ref, gate_ref,
                out_ref, s_ref, mx_ref, acc_ref, tiles_ref, *, n_top):
    TQ = q_ref.shape[2]
    M = GQA_R * TQ
    NC = kcmp_ref.shape[2]
    S = ks_ref.shape[2]
    NS = S // L_SEL
    g = pl.program_id(1)
    t0 = pl.program_id(2) * TQ
    row = lax.broadcasted_iota(jnp.int32, (M, 1), 0)
    t_i = t0 + (row & (TQ - 1))
    slope = jnp.zeros((M, 1), F32)
    for r in range(GQA_R):
        s_r = jnp.where(g == 0, 2.0 ** -(r + 1), 2.0 ** -(GQA_R + r + 1)).astype(F32)
        slope = jnp.where(row // TQ == r, s_r, slope)
    lane = lax.broadcasted_iota(jnp.int32, (M, LANES), 1)
    slope_cols = jnp.where(lane < 2, slope, 0.0).astype(BF16)
    q4 = jnp.concatenate([q_ref[0, r] for r in range(GQA_R)], axis=0)
    qa0 = jnp.concatenate([q4, jnp.zeros((M, HEAD_DIM), BF16), slope_cols], axis=-1)

    WK = WINDOW + TQ
    start = pl.multiple_of(jnp.maximum(t0 - WINDOW, 0), TQ)
    s = _dot_nt(qa0, kw_ref[0, 0, pl.ds(start, WK), :])
    dw = t_i - (start + lax.broadcasted_iota(jnp.int32, (1, WK), 1))
    s = jnp.where(dw.astype(jnp.uint32) < WINDOW, s, NEG)
    e = jnp.exp((s - jnp.max(s, axis=-1, keepdims=True)).astype(BF16))
    o_w = _normalized(_dot(e, vw_ref[0, 0, pl.ds(start, WK), :]))

    n_i = lax.broadcasted_iota(jnp.int32, (1, NC), 1)
    mask_c = (n_i * STRIDE_CMP + (L_CMP - 1)) <= t_i
    s = jnp.where(mask_c, _dot_nt(qa0, kcmp_ref[0, 0]), NEG)
    e = jnp.where(mask_c, jnp.exp(s - jnp.max(s, axis=-1, keepdims=True)), 0.0)
    p = e * (1.0 / jnp.maximum(jnp.sum(e, axis=-1, keepdims=True), 1e-30))
    o_c = _dot(p.astype(BF16), vcmp_ref[0, 0])
    p_sum = p[0:TQ]
    for r in range(1, GQA_R):
        p_sum = p_sum + p[r * TQ:(r + 1) * TQ]

    imp_t = lax.dot_general(ovlt_ref[...], p_sum, (((1,), (1,)), ((), ())),
                            preferred_element_type=F32, precision=lax.Precision.HIGHEST)
    blk = lax.broadcasted_iota(jnp.int32, (NS, TQ), 0)
    tq = t0 + lax.broadcasted_iota(jnp.int32, (NS, TQ), 1)
    cur = tq // L_SEL
    valid = (blk * L_SEL) <= tq
    forced = valid & ((blk == 0) | (blk == cur) | (blk == cur - 1))
    score_t = jnp.where(forced, BIG, jnp.where(valid, imp_t, NEG))
    unsel_t = jnp.where(_selection_rank(score_t) < n_top, 0.0, 1.0).astype(BF16)
    eye = jnp.where(lax.broadcasted_iota(jnp.int32, (TQ, TQ), 0)
                    == lax.broadcasted_iota(jnp.int32, (TQ, TQ), 1), 1.0, 0.0).astype(BF16)
    bias = (_dot_nt(eye, unsel_t) * NEG).astype(BF16)
    if NS < HEAD_DIM:
        bias = jnp.concatenate([bias, jnp.zeros((TQ, HEAD_DIM - NS), BF16)], axis=-1)
    qa = jnp.concatenate([q4, jnp.concatenate([bias] * GQA_R, axis=0), slope_cols], axis=-1)

    n_lane_tiles = K_TILE // LANES
    jd = t0 // K_TILE

    def lane_fold(x, op):
        out = x[:, 0:LANES]
        for c in range(1, n_lane_tiles):
            out = op(out, x[:, c * LANES:(c + 1) * LANES])
        return out

    def keys(j):
        return ks_ref[0, 0, pl.ds(pl.multiple_of(j * K_TILE, K_TILE), K_TILE), :]

    pos = jd * K_TILE + lax.broadcasted_iota(jnp.int32, (1, K_TILE), 1)
    s = jnp.where(pos <= t_i, _dot_nt(qa, keys(jd)), NEG)
    s_ref[jd] = s
    mx_ref[...] = lane_fold(s, jnp.maximum)

    def two_at_a_time(n, tile_fn):
        def pair(i, c):
            tile_fn(2 * i)
            tile_fn(2 * i + 1)
            return c

        lax.fori_loop(0, n // 2, pair, 0)

        @pl.when(n % 2 == 1)
        def _():
            tile_fn(n - 1)

    blocks_per_tile = K_TILE // L_SEL
    picked = jnp.max(1.0 - unsel_t.astype(F32), axis=1, keepdims=True)
    n_live = jnp.int32(0)
    for j in range(NS // blocks_per_tile):
        hit = jnp.max(picked[j * blocks_per_tile:(j + 1) * blocks_per_tile, :]) > 0.0
        live = hit & (j < jd)

        @pl.when(live)
        def _(j=j, n_live=n_live):
            tiles_ref[n_live] = j

        n_live = n_live + live.astype(jnp.int32)

    def qk_tile(i):
        j = tiles_ref[i]
        s = _dot_nt(qa, keys(j))
        s_ref[j] = s
        mx_ref[...] = jnp.maximum(mx_ref[...], lane_fold(s, jnp.maximum))

    two_at_a_time(n_live, qk_tile)
    mx_ref[...] = jnp.broadcast_to(jnp.max(mx_ref[...], axis=-1, keepdims=True), (M, LANES))

    def pv(j):
        mb = mx_ref[...]
        p = jnp.exp((s_ref[j] - jnp.concatenate([mb] * n_lane_tiles, axis=-1)).astype(BF16))
        k0 = pl.multiple_of(j * K_TILE, K_TILE)
        return _dot(p, vs_ref[0, 0, pl.ds(k0, K_TILE), :])

    acc_ref[...] = pv(jd)

    def pv_tile(i):
        acc_ref[...] += pv(tiles_ref[i])

    two_at_a_time(n_live, pv_tile)
    o_s = _normalized(acc_ref[...])

    gl = gate_ref[0]
    gsel = jnp.where(g == 0, gl[:, :LANES // 2], gl[:, LANES // 2:])
    outs = []
    for r in range(GQA_R):
        rows = slice(r * TQ, (r + 1) * TQ)
        outs.append(gsel[:, 3 * r:3 * r + 1] * o_c[rows] + gsel[:, 3 * r + 1:3 * r + 2] * o_s[rows]
                    + gsel[:, 3 * r + 2:3 * r + 3] * o_w[rows])
    out_ref[0] = jnp.concatenate(outs, axis=-1).astype(out_ref.dtype)


def _nsa(q, kcmp, vcmp, ovlt, ks, vs, kw, vw, gates):
    B, _, S, _ = q.shape
    NC = kcmp.shape[2]
    TQ = min(Q_TILE, S)
    assert TQ & (TQ - 1) == 0 and S // L_SEL <= HEAD_DIM and S >= WINDOW + TQ
    n_top = min(N_SEL, S // L_SEL)
    M = GQA_R * TQ
    cmp_spec = lambda w: pl.BlockSpec((1, 1, NC, w), lambda b, g, i: (b, g, 0, 0))
    seq_spec = lambda w: pl.BlockSpec((1, 1, S, w), lambda b, g, i: (b, g, 0, 0))
    return pl.pallas_call(
        functools.partial(_nsa_kernel, n_top=n_top),
        grid=(B, N_KV, S // TQ),
        in_specs=[pl.BlockSpec((1, GQA_R, TQ, HEAD_DIM), lambda b, g, i: (b, g, i, 0)),
                  cmp_spec(AUG_W), cmp_spec(HEAD_DIM), _full(ovlt.shape),
                  seq_spec(AUG_W), seq_spec(LANES), seq_spec(AUG_W), seq_spec(LANES),
                  pl.BlockSpec((1, TQ, LANES), lambda b, g, i: (b, i, 0))],
        out_specs=pl.BlockSpec((1, TQ, GQA_R * HEAD_DIM), lambda b, g, i: (b, i, g)),
        out_shape=jax.ShapeDtypeStruct((B, S, D_ATT), F32),
        scratch_shapes=[pltpu.VMEM((pl.cdiv(S, K_TILE), M, K_TILE), F32),
                        pltpu.VMEM((M, LANES), F32), pltpu.VMEM((M, LANES), F32),
                        pltpu.SMEM((pl.cdiv(S, K_TILE),), jnp.int32)],
        compiler_params=_cparams(("parallel", "parallel", "arbitrary")),
        name="nsa",
    )(q, kcmp, vcmp, ovlt, ks, vs, kw, vw, gates)


def _out_proj_kernel(h_ref, rec_ref, att_ref, an_ref, wor_ref, woa_ref, fn_ref, wr_ref, br_ref,
                     h2_ref, u_ref, route_ref):
    att = _rms(att_ref[...], an_ref[...]).astype(BF16)
    h2 = h_ref[...] + _dot(rec_ref[...], wor_ref[...]) + _dot(att, woa_ref[...])
    h2_ref[...] = h2
    u = _rms(h2, fn_ref[...])
    ub = u.astype(BF16)
    u_ref[...] = ub
    lg = _dot(ub, wr_ref[...]) + br_ref[...]
    lane = lax.broadcasted_iota(jnp.int32, lg.shape, 1)
    is_grp = lane < N_GROUPS
    gl = jnp.where(is_grp, lg, NEG)
    gm = jnp.max(gl, axis=-1, keepdims=True)
    p_g = 1.0 / jnp.sum(jnp.where(is_grp, jnp.exp(gl - gm), 0.0), axis=-1, keepdims=True)
    g_idx = jnp.min(jnp.where(gl == gm, lane, LANES), axis=-1, keepdims=True)
    lo = N_GROUPS + EXP_PER_GROUP * g_idx
    in_g = (lane >= lo) & (lane < lo + EXP_PER_GROUP)
    el = jnp.where(in_g, lg, NEG)
    v1 = jnp.max(el, axis=-1, keepdims=True)
    i1 = jnp.min(jnp.where(in_g & (el == v1), lane, LANES), axis=-1, keepdims=True)
    rest = in_g & (lane != i1)
    el2 = jnp.where(rest, lg, NEG)
    v2 = jnp.max(el2, axis=-1, keepdims=True)
    i2 = jnp.min(jnp.where(rest & (el2 == v2), lane, LANES), axis=-1, keepdims=True)
    e2 = jnp.exp(v2 - v1)
    den = 1.0 + e2
    route = jnp.where(lane == 0, (i1 - N_GROUPS).astype(F32),
                      jnp.where(lane == 1, (i2 - N_GROUPS).astype(F32),
                                jnp.where(lane == 2, p_g * (1.0 / den),
                                          jnp.where(lane == 3, p_g * (e2 / den), 0.0))))
    route_ref[...] = route


def _out_proj(h, rec, att, an, wor, woa, fn, wr, br):
    N, D = h.shape
    TM = min(ROW_TILE, N)
    row = lambda w: pl.BlockSpec((TM, w), lambda i: (i, 0))
    args = (an, wor, woa, fn, wr, br)
    return pl.pallas_call(
        _out_proj_kernel,
        grid=(N // TM,),
        in_specs=[row(D), row(D_REC), row(D_ATT)] + [_full(a.shape) for a in args],
        out_specs=[row(D), row(D), row(LANES)],
        out_shape=[jax.ShapeDtypeStruct((N, D), F32), jax.ShapeDtypeStruct((N, D), BF16),
                   jax.ShapeDtypeStruct((N, LANES), F32)],
        compiler_params=_cparams(("parallel",)),
        name="out_proj",
    )(h, rec, att, *args)


def _expert_kernel(ce_ref, xs_ref, wg_ref, wu_ref, wd_ref, *rest):
    y_ref, wgb, wub, wdb = rest[-4:]
    c = pl.program_id(0)
    prev = ce_ref[jnp.maximum(c - 1, 0)]

    @pl.when((c == 0) | (ce_ref[c] != prev))
    def _():
        wgb[...] = wg_ref[0, 0].astype(BF16)
        wub[...] = wu_ref[0, 0].astype(BF16)
        wdb[...] = wd_ref[0, 0].astype(BF16)

    xs = xs_ref[...]
    hdn = jax.nn.silu(_dot(xs, wgb[...])) * _dot(xs, wub[...])
    y_ref[...] = _dot(hdn.astype(BF16), wdb[...]).astype(y_ref.dtype)


def _experts(chunk_e, xs, wg, wu, wd, layer, y_prev, piece, n_pieces):
    P, D = xs.shape
    DE = wg.shape[-1]
    C = EXPERT_CHUNK
    first = piece * (P // C)
    in_specs = [pl.BlockSpec((C, D), lambda c, ce: (c, 0)),
                pl.BlockSpec((1, 1, D, DE), lambda c, ce: (layer, ce[c], 0, 0)),
                pl.BlockSpec((1, 1, D, DE), lambda c, ce: (layer, ce[c], 0, 0)),
                pl.BlockSpec((1, 1, DE, D), lambda c, ce: (layer, ce[c], 0, 0))]
    args = [chunk_e, xs, wg, wu, wd]
    aliases = {}
    if y_prev is not None:
        in_specs.append(pl.BlockSpec(memory_space=pl.ANY))
        args.append(y_prev)
        aliases = {len(args) - 1: 0}
    grid_spec = pltpu.PrefetchScalarGridSpec(
        num_scalar_prefetch=1,
        grid=(P // C,),
        in_specs=in_specs,
        out_specs=pl.BlockSpec((C, D), lambda c, ce: (first + c, 0)),
        scratch_shapes=[pltpu.VMEM((D, DE), BF16), pltpu.VMEM((D, DE), BF16),
                        pltpu.VMEM((DE, D), BF16)],
    )
    return pl.pallas_call(
        _expert_kernel,
        grid_spec=grid_spec,
        out_shape=jax.ShapeDtypeStruct((n_pieces * P, D), BF16),
        input_output_aliases=aliases,
        compiler_params=_cparams(("arbitrary",)),
        name="experts",
    )(*args)


def _slot_kernel(route_ref, dest_ref, ends_ref, run_ref):
    TM = route_ref.shape[0]
    phase = pl.program_id(0)
    step = pl.program_id(1)
    x = route_ref[...]
    lane = lax.broadcasted_iota(jnp.int32, (TM, LANES), 1).astype(F32)
    hot0 = jnp.where(lane == x[:, 0:1], 1.0, 0.0)
    hot1 = jnp.where(lane == x[:, 1:2], 1.0, 0.0)
    tot0 = jnp.sum(hot0, axis=0, keepdims=True)
    tot = tot0 + jnp.sum(hot1, axis=0, keepdims=True)

    @pl.when((phase == 0) & (step == 0))
    def _():
        run_ref[...] = jnp.zeros(run_ref.shape, F32)

    @pl.when(phase == 0)
    def _():
        run_ref[...] += tot

    @pl.when((phase == 1) & (step == 0))
    def _():
        counts = run_ref[...].astype(jnp.int32)
        padded = ((counts + (EXPERT_CHUNK - 1)) // EXPERT_CHUNK * EXPERT_CHUNK).astype(F32)
        below = jnp.where(lax.broadcasted_iota(jnp.int32, (LANES, LANES), 0)
                          < lax.broadcasted_iota(jnp.int32, (LANES, LANES), 1), 1.0, 0.0)
        starts = jnp.dot(jnp.broadcast_to(padded, (SUBLANES, LANES)), below,
                         preferred_element_type=F32, precision=lax.Precision.HIGHEST)
        ends_ref[...] = starts + padded
        run_ref[...] = starts[0:1]

    @pl.when(phase == 1)
    def _():
        earlier = jnp.where(lax.broadcasted_iota(jnp.int32, (TM, TM), 0)
                            > lax.broadcasted_iota(jnp.int32, (TM, TM), 1), 1.0, 0.0).astype(BF16)
        before0 = _dot(earlier, hot0.astype(BF16))
        before1 = _dot(earlier, hot1.astype(BF16))
        run = run_ref[...]
        d0 = jnp.sum(hot0 * (before0 + run), axis=-1, keepdims=True)
        d1 = jnp.sum(hot1 * (before1 + (run + tot0)), axis=-1, keepdims=True)
        run_ref[...] = run + tot
        both = jnp.where(lane == 0.0, d0, jnp.where(lane == 1.0, d1, 0.0))
        pick = jnp.where(lax.broadcasted_iota(jnp.int32, (SUBLANES, LANES), 0)
                         == lax.broadcasted_iota(jnp.int32, (SUBLANES, LANES), 1), 1.0, 0.0)
        dest_ref[...] = lax.dot_general(pick, both, (((1,), (1,)), ((), ())),
                                        preferred_element_type=F32,
                                        precision=lax.Precision.HIGHEST)


def _slots(route):
    N = route.shape[0]
    TM = min(ROW_TILE, N)
    return pl.pallas_call(
        _slot_kernel,
        grid=(2, N // TM),
        in_specs=[pl.BlockSpec((TM, LANES), lambda ph, i: (i, 0))],
        out_specs=[pl.BlockSpec((SUBLANES, TM), lambda ph, i: (0, i * ph)),
                   pl.BlockSpec((SUBLANES, LANES), lambda ph, i: (0, 0))],
        out_shape=[jax.ShapeDtypeStruct((SUBLANES, N), F32),
                   jax.ShapeDtypeStruct((SUBLANES, LANES), F32)],
        scratch_shapes=[pltpu.VMEM((1, LANES), F32)],
        compiler_params=_cparams(("arbitrary", "arbitrary")),
        name="slots",
    )(route)


def _dispatch(route, n_tok):
    C = EXPERT_CHUNK
    dest, ends = _slots(route)
    dest0 = dest[0].astype(jnp.int32)
    dest1 = dest[1].astype(jnp.int32)
    n_chunks = -(-2 * n_tok // C) + N_EXPERTS
    tok = jnp.arange(n_tok, dtype=jnp.int32)
    buf_tok = jnp.zeros((n_chunks * C,), jnp.int32).at[jnp.concatenate([dest0, dest1])].set(
        jnp.concatenate([tok, tok]), unique_indices=True)
    chunk_start = jnp.arange(n_chunks, dtype=jnp.int32) * C
    seg_end = ends[0, :N_EXPERTS].astype(jnp.int32)
    chunk_e = jnp.minimum(jnp.sum(seg_end[None, :] <= chunk_start[:, None], axis=1),
                          N_EXPERTS - 1).astype(jnp.int32)
    return buf_tok, chunk_e, dest0, dest1


def _ple_kernel(h_ref, y0_ref, y1_ref, route_ref, p_ref, pn_ref, wg_ref, bg_ref, wu_ref,
                fnorm_ref, o_ref, *, final):
    r = route_ref[...]
    h3 = h_ref[...] + (r[:, 2:3] * y0_ref[...].astype(F32) + r[:, 3:4] * y1_ref[...].astype(F32))
    gate = jax.nn.sigmoid(_dot(_rms(h3, pn_ref[...]).astype(BF16), wg_ref[...]) + bg_ref[...])
    h4 = h3 + _dot(p_ref[0].astype(BF16), wu_ref[...]) * gate
    o_ref[...] = _rms(h4, fnorm_ref[...]) if final else h4


def _ple(h, y0, y1, route, p, layer, pn, wg, bg, wu, fnorm, final):
    N, D = h.shape
    TM = min(ROW_TILE, N)
    row = lambda w: pl.BlockSpec((TM, w), lambda i: (i, 0))
    args = (pn, wg, bg, wu, fnorm)
    return pl.pallas_call(
        functools.partial(_ple_kernel, final=final),
        grid=(N // TM,),
        in_specs=[row(D), row(D), row(D), row(LANES),
                  pl.BlockSpec((1, TM, p.shape[2]), lambda i: (layer, i, 0))]
        + [_full(a.shape) for a in args],
        out_specs=row(D),
        out_shape=jax.ShapeDtypeStruct((N, D), F32),
        compiler_params=_cparams(("parallel",)),
        name="ple",
    )(h, y0, y1, route, p, *args)


def _block_diag_tiles(w):
    nb, k, _ = w.shape
    per = MXU_DIM // k
    tiles = jnp.zeros((nb // per, MXU_DIM, MXU_DIM), w.dtype)
    for b in range(nb):
        t, o = divmod(b, per)
        tiles = tiles.at[t, o * k:(o + 1) * k, o * k:(o + 1) * k].set(w[b])
    return tiles


def _overlap(S):
    n = jnp.arange(S // STRIDE_CMP) * STRIDE_CMP
    m = jnp.arange(S // L_SEL) * L_SEL
    ov = jnp.clip(jnp.minimum(n[:, None] + L_CMP, m[None, :] + L_SEL)
                  - jnp.maximum(n[:, None], m[None, :]), 0)
    return ov.astype(F32) / STRIDE_CMP


def kernel(x, p, mix_norm, w_in, conv_w, conv_b, lru_wa, lru_ba, lru_wx, lru_bx, lru_lambda, cmp_pe_k, cmp_pe_v, cmp_wk1, cmp_bk1, cmp_wk2, cmp_bk2, cmp_wv1, cmp_bv1, cmp_wv2, cmp_bv2, rec_out_norm, att_out_norm, w_out, ffn_norm, w_group_router, b_group_router, w_expert_router, b_expert_router, w_gate_exp, w_up_exp, w_down_exp, ple_norm, w_ple_gate, b_ple_gate, w_ple_up, final_norm):
    B, S, D = x.shape
    depth = w_in.shape[0]
    N = B * S
    half = L_CMP // 2 * HEAD_DIM
    row = lambda v: v.reshape(1, -1)
    ovlt = _overlap(S).T
    p_flat = p.reshape(depth, N, -1)
    o_q = 2 * D_REC
    o_kv = o_q + D_ATT
    o_g = o_kv + 6 * KV_W
    h = x
    for i in range(depth):
        wi = w_in[i]
        wgate = jnp.zeros((D, LANES), F32)
        for g in range(N_KV):
            wgate = wgate.at[:, g * (LANES // 2):g * (LANES // 2) + 3 * GQA_R].set(
                wi[:, o_g + g * 3 * GQA_R:o_g + (g + 1) * 3 * GQA_R])
        w_route = jnp.zeros((D, LANES), F32)
        w_route = w_route.at[:, :N_GROUPS].set(w_group_router[i])
        w_route = w_route.at[:, N_GROUPS:N_GROUPS + N_EXPERTS].set(w_expert_router[i])
        b_route = jnp.zeros((1, LANES), F32)
        b_route = b_route.at[0, :N_GROUPS].set(b_group_router[i])
        b_route = b_route.at[0, N_GROUPS:N_GROUPS + N_EXPERTS].set(b_expert_router[i])

        xr, gr, q, kc, vc, ks, vs, kw, vw, gates = _proj_in(
            h.reshape(B, S, D), row(mix_norm[i]), wi[:, :o_q].astype(BF16),
            wi[:, o_q:o_kv].astype(BF16), wi[:, o_kv:o_g].astype(BF16), wgate.astype(BF16))

        rec = _rglru(xr, gr, conv_w[i], row(conv_b[i]),
                     _block_diag_tiles(lru_wa[i]).astype(BF16), row(lru_ba[i]),
                     _block_diag_tiles(lru_wx[i]).astype(BF16), row(lru_bx[i]),
                     row(lru_lambda[i]), row(rec_out_norm[i]))

        nb = S // STRIDE_CMP
        kcmp, vcmp = _compress(
            kc.reshape(B, N_KV, nb, half), vc.reshape(B, N_KV, nb, half),
            cmp_pe_k[i].reshape(2, half), cmp_pe_v[i].reshape(2, half),
            cmp_wk1[i].reshape(2, half, -1).astype(BF16), row(cmp_bk1[i]),
            cmp_wk2[i].astype(BF16), row(cmp_bk2[i]),
            cmp_wv1[i].reshape(2, half, -1).astype(BF16), row(cmp_bv1[i]),
            cmp_wv2[i].astype(BF16), row(cmp_bv2[i]))

        att = _nsa(q, kcmp, vcmp, ovlt, ks, vs, kw, vw, gates)

        wo = w_out[i].astype(BF16)
        h2, u, route = _out_proj(
            h.reshape(N, D), rec.reshape(N, D_REC), att.reshape(N, D_ATT), row(att_out_norm[i]),
            wo[:D_REC], wo[D_REC:], row(ffn_norm[i]), w_route.astype(BF16), b_route)

        buf_tok, chunk_e, dest0, dest1 = _dispatch(route, N)
        n_chunks = chunk_e.shape[0]
        pieces = MOE_PIECES if n_chunks % MOE_PIECES == 0 else 1
        rows = n_chunks // pieces * EXPERT_CHUNK
        y = None
        for k in range(pieces):
            y = _experts(chunk_e[k * (n_chunks // pieces):(k + 1) * (n_chunks // pieces)],
                         u.at[buf_tok[k * rows:(k + 1) * rows]].get(mode="promise_in_bounds"),
                         w_gate_exp, w_up_exp, w_down_exp, i, y, k, pieces)
        y0 = y.at[dest0].get(mode="promise_in_bounds")
        y1 = y.at[dest1].get(mode="promise_in_bounds")

        last = i == depth - 1
        h = _ple(h2, y0, y1, route, p_flat, i, row(ple_norm[i]), w_ple_gate[i].astype(BF16),
                 row(b_ple_gate[i]), w_ple_up[i].astype(BF16), row(final_norm), last)
    return h.reshape(B, S, D)
```

```python
import functools

import jax
import jax.numpy as jnp
from jax import lax
from jax.experimental import pallas as pl
from jax.experimental.pallas import tpu as pltpu

F32 = jnp.float32
BF16 = jnp.bfloat16

D_REC = 512
REC_BLOCKS = 8
CONV_W = 4
LRU_C = 8.0
N_HEADS = 8
HEAD_DIM = 64
N_KV = 2
GQA_R = N_HEADS // N_KV
D_ATT = N_HEADS * HEAD_DIM
KV_W = N_KV * HEAD_DIM
L_CMP = 32
STRIDE_CMP = 16
L_SEL = 64
N_SEL = 16
WINDOW = 512
N_GROUPS = 4
EXP_PER_GROUP = 8
N_EXPERTS = N_GROUPS * EXP_PER_GROUP
EPS = 1e-6
NEG = -1e30
BIG = 1e30

LANES = 128
SUBLANES = 8
MXU_DIM = 256
VMEM_LIMIT = 56 * 1024 * 1024

ROW_TILE = 512
SCAN_TILE = 512
Q_TILE = 256
K_TILE = 512
EXPERT_CHUNK = 512
MOE_PIECES = 4


def _cparams(sem):
    return pltpu.CompilerParams(dimension_semantics=sem, vmem_limit_bytes=VMEM_LIMIT)


def _rms(x, g):
    return x * lax.rsqrt(jnp.mean(x * x, axis=-1, keepdims=True) + EPS) * g


def _dot(a, b):
    return jnp.dot(a, b, preferred_element_type=F32)


def _dot_nt(a, b):
    return lax.dot_general(a, b, (((1,), (1,)), ((), ())), preferred_element_type=F32)


def _masked_softmax(s, mask):
    s = jnp.where(mask, s, NEG)
    m = jnp.max(s, axis=-1, keepdims=True)
    e = jnp.where(mask, jnp.exp(s - m), 0.0)
    return e / jnp.maximum(jnp.sum(e, axis=-1, keepdims=True), 1e-30)


def _full(shape):
    n = len(shape)
    return pl.BlockSpec(shape, lambda *_: (0,) * n)


AUG_W = 2 * LANES


def _sel_block_onehot(pos):
    blk = lax.broadcasted_iota(jnp.int32, (pos.shape[0], HEAD_DIM), 1)
    return jnp.where(blk == pos // L_SEL, 1.0, 0.0).astype(BF16)


def _pos_columns(hi, lo):
    lane = lax.broadcasted_iota(jnp.int32, (hi.shape[0], LANES), 1)
    return jnp.where(lane == 0, hi, jnp.where(lane == 1, lo, 0.0)).astype(BF16)


def _proj_in_kernel(h_ref, g_ref, wa_ref, wq_ref, wkv_ref, wg_ref,
                    xr_ref, gr_ref, q_ref, kc_ref, vc_ref, ks_ref, vs_ref, kw_ref, vw_ref,
                    gate_ref):
    u = _rms(h_ref[0], g_ref[...]).astype(BF16)
    za = _dot(u, wa_ref[...])
    xr_ref[0] = za[:, :D_REC]
    gr_ref[0] = za[:, D_REC:]
    zq = (_dot(u, wq_ref[...]) * (HEAD_DIM ** -0.5)).astype(q_ref.dtype)
    for h in range(N_HEADS):
        q_ref[0, h] = zq[:, h * HEAD_DIM:(h + 1) * HEAD_DIM]
    zkv = _dot(u, wkv_ref[...])
    TM = zkv.shape[0]
    pos = pl.program_id(1) * TM + lax.broadcasted_iota(jnp.int32, (TM, 1), 0)
    blk_onehot = _sel_block_onehot(pos)
    no_onehot = jnp.zeros((TM, HEAD_DIM), BF16)
    ones_col = jnp.where(lax.broadcasted_iota(jnp.int32, (TM, HEAD_DIM), 1) == 0,
                         1.0, 0.0).astype(BF16)
    pos_cols = _pos_columns((pos // L_SEL * L_SEL).astype(F32), (pos % L_SEL).astype(F32))
    for i, ref in enumerate((kc_ref, vc_ref, ks_ref, vs_ref, kw_ref, vw_ref)):
        for g in range(N_KV):
            lo = i * KV_W + g * HEAD_DIM
            piece = zkv[:, lo:lo + HEAD_DIM].astype(ref.dtype)
            if ref is ks_ref:
                piece = jnp.concatenate([piece, blk_onehot, pos_cols], axis=-1)
            elif ref is kw_ref:
                piece = jnp.concatenate([piece, no_onehot, pos_cols], axis=-1)
            elif ref is vs_ref or ref is vw_ref:
                piece = jnp.concatenate([piece, ones_col], axis=-1)
            ref[0, g] = piece
    gate_ref[0] = jax.nn.sigmoid(_dot(u, wg_ref[...]))


def _proj_in(h, g, wa, wq, wkv, wg):
    B, S, D = h.shape
    TM = min(ROW_TILE, S)
    kv_f32 = jax.ShapeDtypeStruct((B, N_KV, S, HEAD_DIM), F32)
    kv_val = jax.ShapeDtypeStruct((B, N_KV, S, LANES), BF16)
    kv_aug = jax.ShapeDtypeStruct((B, N_KV, S, AUG_W), BF16)
    kv_spec = pl.BlockSpec((1, N_KV, TM, HEAD_DIM), lambda b, s: (b, 0, s, 0))
    val_spec = pl.BlockSpec((1, N_KV, TM, LANES), lambda b, s: (b, 0, s, 0))
    aug_spec = pl.BlockSpec((1, N_KV, TM, AUG_W), lambda b, s: (b, 0, s, 0))
    row = lambda w: pl.BlockSpec((1, TM, w), lambda b, s: (b, s, 0))
    return pl.pallas_call(
        _proj_in_kernel,
        grid=(B, S // TM),
        in_specs=[row(D), _full(g.shape), _full(wa.shape), _full(wq.shape), _full(wkv.shape),
                  _full(wg.shape)],
        out_specs=[row(D_REC), row(D_REC),
                   pl.BlockSpec((1, N_HEADS, TM, HEAD_DIM), lambda b, s: (b, 0, s, 0)),
                   kv_spec, kv_spec, aug_spec, val_spec, aug_spec, val_spec, row(LANES)],
        out_shape=[jax.ShapeDtypeStruct((B, S, D_REC), F32),
                   jax.ShapeDtypeStruct((B, S, D_REC), F32),
                   jax.ShapeDtypeStruct((B, N_HEADS, S, HEAD_DIM), BF16),
                   kv_f32, kv_f32, kv_aug, kv_val, kv_aug, kv_val,
                   jax.ShapeDtypeStruct((B, S, LANES), F32)],
        compiler_params=_cparams(("parallel", "parallel")),
        name="proj_in",
    )(h, g, wa, wq, wkv, wg)


def _rglru_kernel(xr_ref, gr_ref, cw_ref, cb_ref, wa_ref, ba_ref, wx_ref, bx_ref, lam_ref,
                  gn_ref, out_ref, ext_ref, a_ref, b_ref, carry_ref):
    TS = xr_ref.shape[1]
    C = xr_ref.shape[2]

    @pl.when(pl.program_id(1) == 0)
    def _():
        ext_ref[0:SUBLANES, :] = jnp.zeros((SUBLANES, C), F32)
        carry_ref[...] = jnp.zeros(carry_ref.shape, F32)

    x = xr_ref[0]
    ext_ref[SUBLANES:, :] = x
    xc = cb_ref[...] + cw_ref[CONV_W - 1:CONV_W, :] * x
    for j in range(CONV_W - 1):
        off = SUBLANES - (CONV_W - 1) + j
        xc = xc + cw_ref[j:j + 1, :] * ext_ref[off:off + TS, :]
    ext_ref[0:SUBLANES, :] = x[TS - SUBLANES:, :]

    xcb = xc.astype(BF16)
    nt = C // MXU_DIM
    ra = jnp.concatenate(
        [_dot(xcb[:, k * MXU_DIM:(k + 1) * MXU_DIM], wa_ref[k]) for k in range(nt)], axis=-1)
    rx = jnp.concatenate(
        [_dot(xcb[:, k * MXU_DIM:(k + 1) * MXU_DIM], wx_ref[k]) for k in range(nt)], axis=-1)
    r = jax.nn.sigmoid(ra + ba_ref[...])
    i = jax.nn.sigmoid(rx + bx_ref[...])
    lam = lam_ref[...]
    softplus_neg_lam = jnp.maximum(-lam, 0.0) + jnp.log1p(jnp.exp(-jnp.abs(lam)))
    log_a = -LRU_C * r * softplus_neg_lam
    a_ref[...] = jnp.exp(log_a)
    b_ref[...] = jnp.sqrt(1.0 - jnp.exp(2.0 * log_a)) * (i * xc)

    row = lax.broadcasted_iota(jnp.int32, (SUBLANES, C), 0)

    def group(gi, carry):
        r0 = pl.multiple_of(gi * SUBLANES, SUBLANES)
        a = a_ref[pl.ds(r0, SUBLANES), :]
        b = b_ref[pl.ds(r0, SUBLANES), :]
        d = 1
        while d < SUBLANES:
            keep = row >= d
            a_s = pltpu.roll(a, d, axis=0)
            b_s = pltpu.roll(b, d, axis=0)
            b = jnp.where(keep, a * b_s, 0.0) + b
            a = jnp.where(keep, a * a_s, a)
            d *= 2
        hcur = b + a * carry
        b_ref[pl.ds(r0, SUBLANES), :] = hcur
        return hcur[SUBLANES - 1:SUBLANES, :]

    carry_ref[...] = lax.fori_loop(0, TS // SUBLANES, group, carry_ref[...])

    y = b_ref[...] * jax.nn.gelu(gr_ref[0])
    out_ref[0] = _rms(y, gn_ref[...]).astype(out_ref.dtype)


def _rglru(xr, gr, cw, cb, wa, ba, wx, bx, lam, gn):
    B, S, C = xr.shape
    TS = min(SCAN_TILE, S)
    row = pl.BlockSpec((1, TS, C), lambda b, s: (b, s, 0))
    args = (cw, cb, wa, ba, wx, bx, lam, gn)
    return pl.pallas_call(
        _rglru_kernel,
        grid=(B, S // TS),
        in_specs=[row, row] + [_full(a.shape) for a in args],
        out_specs=row,
        out_shape=jax.ShapeDtypeStruct((B, S, C), BF16),
        scratch_shapes=[pltpu.VMEM((TS + SUBLANES, C), F32), pltpu.VMEM((TS, C), F32),
                        pltpu.VMEM((TS, C), F32), pltpu.VMEM((1, C), F32)],
        compiler_params=_cparams(("parallel", "arbitrary")),
        name="rglru",
    )(xr, gr, *args)


def _compress_kernel(kc_ref, vc_ref, pek_ref, pev_ref, wk1_ref, bk1_ref, wk2_ref, bk2_ref,
                     wv1_ref, bv1_ref, wv2_ref, bv2_ref, ko_ref, vo_ref):
    def one(x_ref, pe_ref, w1_ref, b1_ref, w2_ref, b2_ref, o_ref, augment):
        x = x_ref[0, 0]
        nb = x.shape[0]
        first = _dot((x + pe_ref[0:1, :]).astype(BF16), w1_ref[0])
        second = _dot((x + pe_ref[1:2, :]).astype(BF16), w1_ref[1])
        hid = jax.nn.gelu(first + pltpu.roll(second, nb - 1, axis=0) + b1_ref[...])
        o = (_dot(hid.astype(BF16), w2_ref[...]) + b2_ref[...]).astype(o_ref.dtype)
        if augment:
            n = lax.broadcasted_iota(jnp.int32, (nb, 1), 0)
            per = L_SEL // STRIDE_CMP
            hi = (n // per * L_SEL).astype(F32)
            lo = (n % per * STRIDE_CMP).astype(F32) + (L_CMP - 1) * 0.5
            o = jnp.concatenate([o, jnp.zeros((nb, HEAD_DIM), BF16), _pos_columns(hi, lo)], axis=-1)
        o_ref[0, 0] = o

    one(kc_ref, pek_ref, wk1_ref, bk1_ref, wk2_ref, bk2_ref, ko_ref, True)
    one(vc_ref, pev_ref, wv1_ref, bv1_ref, wv2_ref, bv2_ref, vo_ref, False)


def _compress(kc, vc, pek, pev, wk1, bk1, wk2, bk2, wv1, bv1, wv2, bv2):
    B, G, NB, W = kc.shape
    blk = pl.BlockSpec((1, 1, NB, W), lambda b, g: (b, g, 0, 0))
    oblk = lambda w: pl.BlockSpec((1, 1, NB, w), lambda b, g: (b, g, 0, 0))
    args = (pek, pev, wk1, bk1, wk2, bk2, wv1, bv1, wv2, bv2)
    oshape = lambda w: jax.ShapeDtypeStruct((B, G, NB, w), BF16)
    return pl.pallas_call(
        _compress_kernel,
        grid=(B, G),
        in_specs=[blk, blk] + [_full(a.shape) for a in args],
        out_specs=[oblk(AUG_W), oblk(HEAD_DIM)],
        out_shape=[oshape(AUG_W), oshape(HEAD_DIM)],
        compiler_params=_cparams(("parallel", "parallel")),
        name="compress",
    )(kc, vc, *args)


def _normalized(acc):
    return acc[:, :HEAD_DIM] * (1.0 / acc[:, HEAD_DIM:HEAD_DIM + 1])


def _selection_rank(score_t):
    NS, TQ = score_t.shape
    sub = lax.broadcasted_iota(jnp.int32, (SUBLANES, TQ), 0)
    groups = [score_t[v * SUBLANES:(v + 1) * SUBLANES, :] for v in range(NS // SUBLANES)]
    ranks = [jnp.zeros((SUBLANES, TQ), F32) for _ in groups]
    for j in range(NS):
        sj = score_t[j:j + 1, :]
        for v, sv in enumerate(groups):
            if j < v * SUBLANES:
                ahead = sj >= sv
            elif j >= (v + 1) * SUBLANES:
                ahead = sj > sv
            else:
                ahead = (sj > sv) | ((sj == sv) & (sub > j - v * SUBLANES))
            ranks[v] = ranks[v] + jnp.where(ahead, 1.0, 0.0)
    return jnp.concatenate(ranks, axis=0)


def _nsa_kernel(q_ref, kcmp_ref, vcmp_ref, ovlt_ref, ks_ref, vs_ref, kw_ref, vw_ref, gate_ref,
                out_ref, s_ref, mx_ref, acc_ref, part_ref, gs_ref, tiles_ref, *, n_top):
    TQ = q_ref.shape[2]
    M = GQA_R * TQ
    NC = kcmp_ref.shape[2]
    S = ks_ref.shape[2]
    NS = S // L_SEL
    g = pl.program_id(1)
    t0 = pl.program_id(2) * TQ
    row = lax.broadcasted_iota(jnp.int32, (M, 1), 0)
    t_i = t0 + (row & (TQ - 1))
    slope = jnp.zeros((M, 1), F32)
    for r in range(GQA_R):
        s_r = jnp.where(g == 0, 2.0 ** -(r + 1), 2.0 ** -(GQA_R + r + 1)).astype(F32)
        slope = jnp.where(row // TQ == r, s_r, slope)
    lane = lax.broadcasted_iota(jnp.int32, (M, LANES), 1)
    slope_cols = jnp.where(lane < 2, slope, 0.0).astype(BF16)
    q4 = jnp.concatenate([q_ref[0, r] for r in range(GQA_R)], axis=0)
    qa0 = jnp.concatenate([q4, jnp.zeros((M, HEAD_DIM), BF16), slope_cols], axis=-1)

    WK = WINDOW + TQ
    start = pl.multiple_of(jnp.maximum(t0 - WINDOW, 0), TQ)
    s = _dot_nt(qa0, kw_ref[0, 0, pl.ds(start, WK), :])
    dw = t_i - (start + lax.broadcasted_iota(jnp.int32, (1, WK), 1))
    s = jnp.where(dw.astype(jnp.uint32) < WINDOW, s, NEG)
    e = jnp.exp((s - jnp.max(s, axis=-1, keepdims=True)).astype(BF16))
    o_w = _normalized(_dot(e, vw_ref[0, 0, pl.ds(start, WK), :]))

    n_i = lax.broadcasted_iota(jnp.int32, (1, NC), 1)
    mask_c = (n_i * STRIDE_CMP + (L_CMP - 1)) <= t_i
    s = jnp.where(mask_c, _dot_nt(qa0, kcmp_ref[0, 0]), NEG)
    e = jnp.where(mask_c, jnp.exp(s - jnp.max(s, axis=-1, keepdims=True)), 0.0)
    p = e * (1.0 / jnp.maximum(jnp.sum(e, axis=-1, keepdims=True), 1e-30))
    o_c = _dot(p.astype(BF16), vcmp_ref[0, 0])
    gl = gate_ref[0]
    gsel = jnp.where(g == 0, gl[:, :LANES // 2], gl[:, LANES // 2:])

    def gate_col(c):
        return jnp.concatenate([gsel[:, 3 * r + c:3 * r + c + 1] for r in range(GQA_R)], axis=0)

    part_ref[...] = gate_col(0) * o_c + gate_col(2) * o_w
    gs_ref[...] = jnp.broadcast_to(gate_col(1), (M, HEAD_DIM))
    p_sum = p[0:TQ]
    for r in range(1, GQA_R):
        p_sum = p_sum + p[r * TQ:(r + 1) * TQ]

    imp_t = lax.dot_general(ovlt_ref[...], p_sum, (((1,), (1,)), ((), ())),
                            preferred_element_type=F32, precision=lax.Precision.HIGHEST)
    blk = lax.broadcasted_iota(jnp.int32, (NS, TQ), 0)
    tq = t0 + lax.broadcasted_iota(jnp.int32, (NS, TQ), 1)
    cur = tq // L_SEL
    valid = (blk * L_SEL) <= tq
    forced = valid & ((blk == 0) | (blk == cur) | (blk == cur - 1))
    score_t = jnp.where(forced, BIG, jnp.where(valid, imp_t, NEG))
    unsel_t = jnp.where(_selection_rank(score_t) < n_top, 0.0, 1.0).astype(BF16)
    eye = jnp.where(lax.broadcasted_iota(jnp.int32, (TQ, TQ), 0)
                    == lax.broadcasted_iota(jnp.int32, (TQ, TQ), 1), 1.0, 0.0).astype(BF16)
    bias = (_dot_nt(eye, unsel_t) * NEG).astype(BF16)
    if NS < HEAD_DIM:
        bias = jnp.concatenate([bias, jnp.zeros((TQ, HEAD_DIM - NS), BF16)], axis=-1)
    qa = jnp.concatenate([q4, jnp.concatenate([bias] * GQA_R, axis=0), slope_cols], axis=-1)

    n_lane_tiles = K_TILE // LANES
    jd = t0 // K_TILE

    def lane_fold(x, op):
        out = x[:, 0:LANES]
        for c in range(1, n_lane_tiles):
            out = op(out, x[:, c * LANES:(c + 1) * LANES])
        return out

    def keys(j):
        return ks_ref[0, 0, pl.ds(pl.multiple_of(j * K_TILE, K_TILE), K_TILE), :]

    pos = jd * K_TILE + lax.broadcasted_iota(jnp.int32, (1, K_TILE), 1)
    s = jnp.where(pos <= t_i, _dot_nt(qa, keys(jd)), NEG)
    s_ref[jd] = s
    mx_ref[...] = lane_fold(s, jnp.maximum)

    def two_at_a_time(n, tile_fn):
        def pair(i, c):
            tile_fn(2 * i)
            tile_fn(2 * i + 1)
            return c

        lax.fori_loop(0, n // 2, pair, 0)

        @pl.when(n % 2 == 1)
        def _():
            tile_fn(n - 1)

    blocks_per_tile = K_TILE // L_SEL
    picked = jnp.max(1.0 - unsel_t.astype(F32), axis=1, keepdims=True)
    n_tiles = NS // blocks_per_tile
    n_live = jnp.int32(0)
    for j in range(n_tiles):
        hit = jnp.max(picked[j * blocks_per_tile:(j + 1) * blocks_per_tile, :]) > 0.0
        live = hit & (j < jd)
        tiles_ref[jnp.where(live, n_live, n_tiles)] = j
        n_live = n_live + live.astype(jnp.int32)

    def qk_tile(i):
        j = tiles_ref[i]
        s = _dot_nt(qa, keys(j))
        s_ref[j] = s
        mx_ref[...] = jnp.maximum(mx_ref[...], lane_fold(s, jnp.maximum))

    two_at_a_time(n_live, qk_tile)
    mx_ref[...] = jnp.broadcast_to(jnp.max(mx_ref[...], axis=-1, keepdims=True), (M, LANES))

    def pv(j):
        mb = mx_ref[...]
        p = jnp.exp((s_ref[j] - jnp.concatenate([mb] * n_lane_tiles, axis=-1)).astype(BF16))
        k0 = pl.multiple_of(j * K_TILE, K_TILE)
        return _dot(p, vs_ref[0, 0, pl.ds(k0, K_TILE), :])

    acc_ref[...] = pv(jd)

    def pv_tile(i):
        acc_ref[...] += pv(tiles_ref[i])

    two_at_a_time(n_live, pv_tile)
    o_s = _normalized(acc_ref[...])

    o = part_ref[...] + gs_ref[...] * o_s
    out_ref[0] = jnp.concatenate([o[r * TQ:(r + 1) * TQ] for r in range(GQA_R)],
                                 axis=-1).astype(out_ref.dtype)


def _nsa(q, kcmp, vcmp, ovlt, ks, vs, kw, vw, gates):
    B, _, S, _ = q.shape
    NC = kcmp.shape[2]
    TQ = min(Q_TILE, S)
    assert TQ & (TQ - 1) == 0 and S // L_SEL <= HEAD_DIM and S >= WINDOW + TQ
    n_top = min(N_SEL, S // L_SEL)
    M = GQA_R * TQ
    cmp_spec = lambda w: pl.BlockSpec((1, 1, NC, w), lambda b, g, i: (b, g, 0, 0))
    seq_spec = lambda w: pl.BlockSpec((1, 1, S, w), lambda b, g, i: (b, g, 0, 0))
    return pl.pallas_call(
        functools.partial(_nsa_kernel, n_top=n_top),
        grid=(B, N_KV, S // TQ),
        in_specs=[pl.BlockSpec((1, GQA_R, TQ, HEAD_DIM), lambda b, g, i: (b, g, i, 0)),
                  cmp_spec(AUG_W), cmp_spec(HEAD_DIM), _full(ovlt.shape),
                  seq_spec(AUG_W), seq_spec(LANES), seq_spec(AUG_W), seq_spec(LANES),
                  pl.BlockSpec((1, TQ, LANES), lambda b, g, i: (b, i, 0))],
        out_specs=pl.BlockSpec((1, TQ, GQA_R * HEAD_DIM), lambda b, g, i: (b, i, g)),
        out_shape=jax.ShapeDtypeStruct((B, S, D_ATT), F32),
        scratch_shapes=[pltpu.VMEM((pl.cdiv(S, K_TILE), M, K_TILE), F32),
                        pltpu.VMEM((M, LANES), F32), pltpu.VMEM((M, LANES), F32),
                        pltpu.VMEM((M, HEAD_DIM), F32), pltpu.VMEM((M, HEAD_DIM), F32),
                        pltpu.SMEM((pl.cdiv(S, K_TILE) + 1,), jnp.int32)],
        compiler_params=_cparams(("parallel", "parallel", "arbitrary")),
        name="nsa",
    )(q, kcmp, vcmp, ovlt, ks, vs, kw, vw, gates)


def _out_proj_kernel(h_ref, rec_ref, att_ref, an_ref, wor_ref, woa_ref, fn_ref, wr_ref, br_ref,
                     h2_ref, u_ref, route_ref):
    att = _rms(att_ref[...], an_ref[...]).astype(BF16)
    h2 = h_ref[...] + _dot(rec_ref[...], wor_ref[...]) + _dot(att, woa_ref[...])
    h2_ref[...] = h2
    u = _rms(h2, fn_ref[...])
    ub = u.astype(BF16)
    u_ref[...] = ub
    lg = _dot(ub, wr_ref[...]) + br_ref[...]
    lane = lax.broadcasted_iota(jnp.int32, lg.shape, 1)
    is_grp = lane < N_GROUPS
    gl = jnp.where(is_grp, lg, NEG)
    gm = jnp.max(gl, axis=-1, keepdims=True)
    p_g = 1.0 / jnp.sum(jnp.where(is_grp, jnp.exp(gl - gm), 0.0), axis=-1, keepdims=True)
    g_idx = jnp.min(jnp.where(gl == gm, lane, LANES), axis=-1, keepdims=True)
    lo = N_GROUPS + EXP_PER_GROUP * g_idx
    in_g = (lane >= lo) & (lane < lo + EXP_PER_GROUP)
    el = jnp.where(in_g, lg, NEG)
    v1 = jnp.max(el, axis=-1, keepdims=True)
    i1 = jnp.min(jnp.where(in_g & (el == v1), lane, LANES), axis=-1, keepdims=True)
    rest = in_g & (lane != i1)
    el2 = jnp.where(rest, lg, NEG)
    v2 = jnp.max(el2, axis=-1, keepdims=True)
    i2 = jnp.min(jnp.where(rest & (el2 == v2), lane, LANES), axis=-1, keepdims=True)
    e2 = jnp.exp(v2 - v1)
    den = 1.0 + e2
    route = jnp.where(lane == 0, (i1 - N_GROUPS).astype(F32),
                      jnp.where(lane == 1, (i2 - N_GROUPS).astype(F32),
                                jnp.where(lane == 2, p_g * (1.0 / den),
                                          jnp.where(lane == 3, p_g * (e2 / den), 0.0))))
    route_ref[...] = route


def _out_proj(h, rec, att, an, wor, woa, fn, wr, br):
    N, D = h.shape
    TM = min(ROW_TILE, N)
    row = lambda w: pl.BlockSpec((TM, w), lambda i: (i, 0))
    args = (an, wor, woa, fn, wr, br)
    return pl.pallas_call(
        _out_proj_kernel,
        grid=(N // TM,),
        in_specs=[row(D), row(D_REC), row(D_ATT)] + [_full(a.shape) for a in args],
        out_specs=[row(D), row(D), row(LANES)],
        out_shape=[jax.ShapeDtypeStruct((N, D), F32), jax.ShapeDtypeStruct((N, D), BF16),
                   jax.ShapeDtypeStruct((N, LANES), F32)],
        compiler_params=_cparams(("parallel",)),
        name="out_proj",
    )(h, rec, att, *args)


def _expert_kernel(ce_ref, xs_ref, wg_ref, wu_ref, wd_ref, *rest):
    y_ref, wgb, wub, wdb = rest[-4:]
    c = pl.program_id(0)
    prev = ce_ref[jnp.maximum(c - 1, 0)]

    @pl.when((c == 0) | (ce_ref[c] != prev))
    def _():
        wgb[...] = wg_ref[0, 0].astype(BF16)
        wub[...] = wu_ref[0, 0].astype(BF16)
        wdb[...] = wd_ref[0, 0].astype(BF16)

    xs = xs_ref[...]
    hdn = jax.nn.silu(_dot(xs, wgb[...])) * _dot(xs, wub[...])
    y_ref[...] = _dot(hdn.astype(BF16), wdb[...]).astype(y_ref.dtype)


def _experts(chunk_e, xs, wg, wu, wd, layer, y_prev, piece, n_pieces):
    P, D = xs.shape
    DE = wg.shape[-1]
    C = EXPERT_CHUNK
    first = piece * (P // C)
    in_specs = [pl.BlockSpec((C, D), lambda c, ce: (c, 0)),
                pl.BlockSpec((1, 1, D, DE), lambda c, ce: (layer, ce[c], 0, 0)),
                pl.BlockSpec((1, 1, D, DE), lambda c, ce: (layer, ce[c], 0, 0)),
                pl.BlockSpec((1, 1, DE, D), lambda c, ce: (layer, ce[c], 0, 0))]
    args = [chunk_e, xs, wg, wu, wd]
    aliases = {}
    if y_prev is not None:
        in_specs.append(pl.BlockSpec(memory_space=pl.ANY))
        args.append(y_prev)
        aliases = {len(args) - 1: 0}
    grid_spec = pltpu.PrefetchScalarGridSpec(
        num_scalar_prefetch=1,
        grid=(P // C,),
        in_specs=in_specs,
        out_specs=pl.BlockSpec((C, D), lambda c, ce: (first + c, 0)),
        scratch_shapes=[pltpu.VMEM((D, DE), BF16), pltpu.VMEM((D, DE), BF16),
                        pltpu.VMEM((DE, D), BF16)],
    )
    return pl.pallas_call(
        _expert_kernel,
        grid_spec=grid_spec,
        out_shape=jax.ShapeDtypeStruct((n_pieces * P, D), BF16),
        input_output_aliases=aliases,
        compiler_params=_cparams(("arbitrary",)),
        name="experts",
    )(*args)


def _slot_kernel(route_ref, dest_ref, ends_ref, run_ref):
    TM = route_ref.shape[0]
    phase = pl.program_id(0)
    step = pl.program_id(1)
    x = route_ref[...]
    lane = lax.broadcasted_iota(jnp.int32, (TM, LANES), 1).astype(F32)
    hot0 = jnp.where(lane == x[:, 0:1], 1.0, 0.0)
    hot1 = jnp.where(lane == x[:, 1:2], 1.0, 0.0)
    tot0 = jnp.sum(hot0, axis=0, keepdims=True)
    tot = tot0 + jnp.sum(hot1, axis=0, keepdims=True)

    @pl.when((phase == 0) & (step == 0))
    def _():
        run_ref[...] = jnp.zeros(run_ref.shape, F32)

    @pl.when(phase == 0)
    def _():
        run_ref[...] += tot

    @pl.when((phase == 1) & (step == 0))
    def _():
        counts = run_ref[...].astype(jnp.int32)
        padded = ((counts + (EXPERT_CHUNK - 1)) // EXPERT_CHUNK * EXPERT_CHUNK).astype(F32)
        below = jnp.where(lax.broadcasted_iota(jnp.int32, (LANES, LANES), 0)
                          < lax.broadcasted_iota(jnp.int32, (LANES, LANES), 1), 1.0, 0.0)
        starts = jnp.dot(jnp.broadcast_to(padded, (SUBLANES, LANES)), below,
                         preferred_element_type=F32, precision=lax.Precision.HIGHEST)
        ends_ref[...] = starts + padded
        run_ref[...] = starts[0:1]

    @pl.when(phase == 1)
    def _():
        earlier = jnp.where(lax.broadcasted_iota(jnp.int32, (TM, TM), 0)
                            > lax.broadcasted_iota(jnp.int32, (TM, TM), 1), 1.0, 0.0).astype(BF16)
        before0 = _dot(earlier, hot0.astype(BF16))
        before1 = _dot(earlier, hot1.astype(BF16))
        run = run_ref[...]
        d0 = jnp.sum(hot0 * (before0 + run), axis=-1, keepdims=True)
        d1 = jnp.sum(hot1 * (before1 + (run + tot0)), axis=-1, keepdims=True)
        run_ref[...] = run + tot
        both = jnp.where(lane == 0.0, d0, jnp.where(lane == 1.0, d1, 0.0))
        pick = jnp.where(lax.broadcasted_iota(jnp.int32, (SUBLANES, LANES), 0)
                         == lax.broadcasted_iota(jnp.int32, (SUBLANES, LANES), 1), 1.0, 0.0)
        dest_ref[...] = lax.dot_general(pick, both, (((1,), (1,)), ((), ())),
                                        preferred_element_type=F32,
                                        precision=lax.Precision.HIGHEST)


def _slots(route):
    N = route.shape[0]
    TM = min(ROW_TILE, N)
    return pl.pallas_call(
        _slot_kernel,
        grid=(2, N // TM),
        in_specs=[pl.BlockSpec((TM, LANES), lambda ph, i: (i, 0))],
        out_specs=[pl.BlockSpec((SUBLANES, TM), lambda ph, i: (0, i * ph)),
                   pl.BlockSpec((SUBLANES, LANES), lambda ph, i: (0, 0))],
        out_shape=[jax.ShapeDtypeStruct((SUBLANES, N), F32),
                   jax.ShapeDtypeStruct((SUBLANES, LANES), F32)],
        scratch_shapes=[pltpu.VMEM((1, LANES), F32)],
        compiler_params=_cparams(("arbitrary", "arbitrary")),
        name="slots",
    )(route)


def _dispatch(route, n_tok):
    C = EXPERT_CHUNK
    dest, ends = _slots(route)
    dest0 = dest[0].astype(jnp.int32)
    dest1 = dest[1].astype(jnp.int32)
    n_chunks = -(-2 * n_tok // C) + N_EXPERTS
    tok = jnp.arange(n_tok, dtype=jnp.int32)
    buf_tok = jnp.zeros((n_chunks * C,), jnp.int32).at[jnp.concatenate([dest0, dest1])].set(
        jnp.concatenate([tok, tok]), unique_indices=True)
    chunk_start = jnp.arange(n_chunks, dtype=jnp.int32) * C
    seg_end = ends[0, :N_EXPERTS].astype(jnp.int32)
    chunk_e = jnp.minimum(jnp.sum(seg_end[None, :] <= chunk_start[:, None], axis=1),
                          N_EXPERTS - 1).astype(jnp.int32)
    return buf_tok, chunk_e, dest0, dest1


def _ple_kernel(h_ref, y0_ref, y1_ref, route_ref, p_ref, pn_ref, wg_ref, bg_ref, wu_ref,
                fnorm_ref, o_ref, *, final):
    r = route_ref[...]
    h3 = h_ref[...] + (r[:, 2:3] * y0_ref[...].astype(F32) + r[:, 3:4] * y1_ref[...].astype(F32))
    gate = jax.nn.sigmoid(_dot(_rms(h3, pn_ref[...]).astype(BF16), wg_ref[...]) + bg_ref[...])
    h4 = h3 + _dot(p_ref[0].astype(BF16), wu_ref[...]) * gate
    o_ref[...] = _rms(h4, fnorm_ref[...]) if final else h4


def _ple(h, y0, y1, route, p, layer, pn, wg, bg, wu, fnorm, final):
    N, D = h.shape
    TM = min(ROW_TILE, N)
    row = lambda w: pl.BlockSpec((TM, w), lambda i: (i, 0))
    args = (pn, wg, bg, wu, fnorm)
    return pl.pallas_call(
        functools.partial(_ple_kernel, final=final),
        grid=(N // TM,),
        in_specs=[row(D), row(D), row(D), row(LANES),
                  pl.BlockSpec((1, TM, p.shape[2]), lambda i: (layer, i, 0))]
        + [_full(a.shape) for a in args],
        out_specs=row(D),
        out_shape=jax.ShapeDtypeStruct((N, D), F32),
        compiler_params=_cparams(("parallel",)),
        name="ple",
    )(h, y0, y1, route, p, *args)


def _block_diag_tiles(w):
    nb, k, _ = w.shape
    per = MXU_DIM // k
    tiles = jnp.zeros((nb // per, MXU_DIM, MXU_DIM), w.dtype)
    for b in range(nb):
        t, o = divmod(b, per)
        tiles = tiles.at[t, o * k:(o + 1) * k, o * k:(o + 1) * k].set(w[b])
    return tiles


def _overlap(S):
    n = jnp.arange(S // STRIDE_CMP) * STRIDE_CMP
    m = jnp.arange(S // L_SEL) * L_SEL
    ov = jnp.clip(jnp.minimum(n[:, None] + L_CMP, m[None, :] + L_SEL)
                  - jnp.maximum(n[:, None], m[None, :]), 0)
    return ov.astype(F32) / STRIDE_CMP


def kernel(x, p, mix_norm, w_in, conv_w, conv_b, lru_wa, lru_ba, lru_wx, lru_bx, lru_lambda, cmp_pe_k, cmp_pe_v, cmp_wk1, cmp_bk1, cmp_wk2, cmp_bk2, cmp_wv1, cmp_bv1, cmp_wv2, cmp_bv2, rec_out_norm, att_out_norm, w_out, ffn_norm, w_group_router, b_group_router, w_expert_router, b_expert_router, w_gate_exp, w_up_exp, w_down_exp, ple_norm, w_ple_gate, b_ple_gate, w_ple_up, final_norm):
    B, S, D = x.shape
    depth = w_in.shape[0]
    N = B * S
    half = L_CMP // 2 * HEAD_DIM
    row = lambda v: v.reshape(1, -1)
    ovlt = _overlap(S).T
    p_flat = p.reshape(depth, N, -1)
    o_q = 2 * D_REC
    o_kv = o_q + D_ATT
    o_g = o_kv + 6 * KV_W
    h = x
    for i in range(depth):
        wi = w_in[i]
        wgate = jnp.zeros((D, LANES), F32)
        for g in range(N_KV):
            wgate = wgate.at[:, g * (LANES // 2):g * (LANES // 2) + 3 * GQA_R].set(
                wi[:, o_g + g * 3 * GQA_R:o_g + (g + 1) * 3 * GQA_R])
        w_route = jnp.zeros((D, LANES), F32)
        w_route = w_route.at[:, :N_GROUPS].set(w_group_router[i])
        w_route = w_route.at[:, N_GROUPS:N_GROUPS + N_EXPERTS].set(w_expert_router[i])
        b_route = jnp.zeros((1, LANES), F32)
        b_route = b_route.at[0, :N_GROUPS].set(b_group_router[i])
        b_route = b_route.at[0, N_GROUPS:N_GROUPS + N_EXPERTS].set(b_expert_router[i])

        xr, gr, q, kc, vc, ks, vs, kw, vw, gates = _proj_in(
            h.reshape(B, S, D), row(mix_norm[i]), wi[:, :o_q].astype(BF16),
            wi[:, o_q:o_kv].astype(BF16), wi[:, o_kv:o_g].astype(BF16), wgate.astype(BF16))

        rec = _rglru(xr, gr, conv_w[i], row(conv_b[i]),
                     _block_diag_tiles(lru_wa[i]).astype(BF16), row(lru_ba[i]),
                     _block_diag_tiles(lru_wx[i]).astype(BF16), row(lru_bx[i]),
                     row(lru_lambda[i]), row(rec_out_norm[i]))

        nb = S // STRIDE_CMP
        kcmp, vcmp = _compress(
            kc.reshape(B, N_KV, nb, half), vc.reshape(B, N_KV, nb, half),
            cmp_pe_k[i].reshape(2, half), cmp_pe_v[i].reshape(2, half),
            cmp_wk1[i].reshape(2, half, -1).astype(BF16), row(cmp_bk1[i]),
            cmp_wk2[i].astype(BF16), row(cmp_bk2[i]),
            cmp_wv1[i].reshape(2, half, -1).astype(BF16), row(cmp_bv1[i]),
            cmp_wv2[i].astype(BF16), row(cmp_bv2[i]))

        att = _nsa(q, kcmp, vcmp, ovlt, ks, vs, kw, vw, gates)

        wo = w_out[i].astype(BF16)
        h2, u, route = _out_proj(
            h.reshape(N, D), rec.reshape(N, D_REC), att.reshape(N, D_ATT), row(att_out_norm[i]),
            wo[:D_REC], wo[D_REC:], row(ffn_norm[i]), w_route.astype(BF16), b_route)

        buf_tok, chunk_e, dest0, dest1 = _dispatch(route, N)
        n_chunks = chunk_e.shape[0]
        pieces = MOE_PIECES if n_chunks % MOE_PIECES == 0 else 1
        rows = n_chunks // pieces * EXPERT_CHUNK
        y = None
        for k in range(pieces):
            y = _experts(chunk_e[k * (n_chunks // pieces):(k + 1) * (n_chunks // pieces)],
                         u.at[buf_tok[k * rows:(k + 1) * rows]].get(mode="promise_in_bounds"),
                         w_gate_exp, w_up_exp, w_down_exp, i, y, k, pieces)
        y0 = y.at[dest0].get(mode="promise_in_bounds")
        y1 = y.at[dest1].get(mode="promise_in_bounds")

        last = i == depth - 1
        h = _ple(h2, y0, y1, route, p_flat, i, row(ple_norm[i]), w_ple_gate[i].astype(BF16),
                 row(b_ple_gate[i]), w_ple_up[i].astype(BF16), row(final_norm), last)
    return h.reshape(B, S, D)
```

```python
import functools

import jax
import jax.numpy as jnp
from jax import lax
from jax.experimental import pallas as pl
from jax.experimental.pallas import tpu as pltpu

F32 = jnp.float32
BF16 = jnp.bfloat16

D_REC = 512
REC_BLOCKS = 8
CONV_W = 4
LRU_C = 8.0
N_HEADS = 8
HEAD_DIM = 64
N_KV = 2
GQA_R = N_HEADS // N_KV
D_ATT = N_HEADS * HEAD_DIM
KV_W = N_KV * HEAD_DIM
L_CMP = 32
STRIDE_CMP = 16
L_SEL = 64
N_SEL = 16
WINDOW = 512
N_GROUPS = 4
EXP_PER_GROUP = 8
N_EXPERTS = N_GROUPS * EXP_PER_GROUP
EPS = 1e-6
NEG = -1e30
BIG = 1e30

LANES = 128
SUBLANES = 8
MXU_DIM = 256
VMEM_LIMIT = 56 * 1024 * 1024

ROW_TILE = 512
SCAN_TILE = 512
Q_TILE = 256
K_TILE = 512
EXPERT_CHUNK = 512
MOE_PIECES = 8


def _cparams(sem):
    return pltpu.CompilerParams(dimension_semantics=sem, vmem_limit_bytes=VMEM_LIMIT)


def _rms(x, g):
    return x * lax.rsqrt(jnp.mean(x * x, axis=-1, keepdims=True) + EPS) * g


def _dot(a, b):
    return jnp.dot(a, b, preferred_element_type=F32)


def _dot_nt(a, b):
    return lax.dot_general(a, b, (((1,), (1,)), ((), ())), preferred_element_type=F32)


def _masked_softmax(s, mask):
    s = jnp.where(mask, s, NEG)
    m = jnp.max(s, axis=-1, keepdims=True)
    e = jnp.where(mask, jnp.exp(s - m), 0.0)
    return e / jnp.maximum(jnp.sum(e, axis=-1, keepdims=True), 1e-30)


def _full(shape):
    n = len(shape)
    return pl.BlockSpec(shape, lambda *_: (0,) * n)


AUG_W = 2 * LANES


def _sel_block_onehot(pos):
    blk = lax.broadcasted_iota(jnp.int32, (pos.shape[0], HEAD_DIM), 1)
    return jnp.where(blk == pos // L_SEL, 1.0, 0.0).astype(BF16)


def _pos_columns(hi, lo):
    lane = lax.broadcasted_iota(jnp.int32, (hi.shape[0], LANES), 1)
    return jnp.where(lane == 0, hi, jnp.where(lane == 1, lo, 0.0)).astype(BF16)


def _proj_in_kernel(h_ref, g_ref, wa_ref, wq_ref, wkv_ref, wg_ref,
                    xr_ref, gr_ref, q_ref, kc_ref, vc_ref, ks_ref, vs_ref, kw_ref, vw_ref,
                    gate_ref):
    u = _rms(h_ref[0], g_ref[...]).astype(BF16)
    za = _dot(u, wa_ref[...])
    xr_ref[0] = za[:, :D_REC]
    gr_ref[0] = za[:, D_REC:]
    zq = (_dot(u, wq_ref[...]) * (HEAD_DIM ** -0.5)).astype(q_ref.dtype)
    for h in range(N_HEADS):
        q_ref[0, h] = zq[:, h * HEAD_DIM:(h + 1) * HEAD_DIM]
    zkv = _dot(u, wkv_ref[...])
    TM = zkv.shape[0]
    pos = pl.program_id(1) * TM + lax.broadcasted_iota(jnp.int32, (TM, 1), 0)
    blk_onehot = _sel_block_onehot(pos)
    no_onehot = jnp.zeros((TM, HEAD_DIM), BF16)
    ones_col = jnp.where(lax.broadcasted_iota(jnp.int32, (TM, HEAD_DIM), 1) == 0,
                         1.0, 0.0).astype(BF16)
    pos_cols = _pos_columns((pos // L_SEL * L_SEL).astype(F32), (pos % L_SEL).astype(F32))
    for i, ref in enumerate((kc_ref, vc_ref, ks_ref, vs_ref, kw_ref, vw_ref)):
        for g in range(N_KV):
            lo = i * KV_W + g * HEAD_DIM
            piece = zkv[:, lo:lo + HEAD_DIM].astype(ref.dtype)
            if ref is ks_ref:
                piece = jnp.concatenate([piece, blk_onehot, pos_cols], axis=-1)
            elif ref is kw_ref:
                piece = jnp.concatenate([piece, no_onehot, pos_cols], axis=-1)
            elif ref is vs_ref or ref is vw_ref:
                piece = jnp.concatenate([piece, ones_col], axis=-1)
            ref[0, g] = piece
    gate_ref[0] = jax.nn.sigmoid(_dot(u, wg_ref[...]))


def _proj_in(h, g, wa, wq, wkv, wg):
    B, S, D = h.shape
    TM = min(ROW_TILE, S)
    kv_f32 = jax.ShapeDtypeStruct((B, N_KV, S, HEAD_DIM), F32)
    kv_val = jax.ShapeDtypeStruct((B, N_KV, S, LANES), BF16)
    kv_aug = jax.ShapeDtypeStruct((B, N_KV, S, AUG_W), BF16)
    kv_spec = pl.BlockSpec((1, N_KV, TM, HEAD_DIM), lambda b, s: (b, 0, s, 0))
    val_spec = pl.BlockSpec((1, N_KV, TM, LANES), lambda b, s: (b, 0, s, 0))
    aug_spec = pl.BlockSpec((1, N_KV, TM, AUG_W), lambda b, s: (b, 0, s, 0))
    row = lambda w: pl.BlockSpec((1, TM, w), lambda b, s: (b, s, 0))
    return pl.pallas_call(
        _proj_in_kernel,
        grid=(B, S // TM),
        in_specs=[row(D), _full(g.shape), _full(wa.shape), _full(wq.shape), _full(wkv.shape),
                  _full(wg.shape)],
        out_specs=[row(D_REC), row(D_REC),
                   pl.BlockSpec((1, N_HEADS, TM, HEAD_DIM), lambda b, s: (b, 0, s, 0)),
                   kv_spec, kv_spec, aug_spec, val_spec, aug_spec, val_spec, row(LANES)],
        out_shape=[jax.ShapeDtypeStruct((B, S, D_REC), F32),
                   jax.ShapeDtypeStruct((B, S, D_REC), F32),
                   jax.ShapeDtypeStruct((B, N_HEADS, S, HEAD_DIM), BF16),
                   kv_f32, kv_f32, kv_aug, kv_val, kv_aug, kv_val,
                   jax.ShapeDtypeStruct((B, S, LANES), F32)],
        compiler_params=_cparams(("parallel", "parallel")),
        name="proj_in",
    )(h, g, wa, wq, wkv, wg)


def _rglru_kernel(xr_ref, gr_ref, cw_ref, cb_ref, wa_ref, ba_ref, wx_ref, bx_ref, lam_ref,
                  gn_ref, out_ref, ext_ref, a_ref, b_ref, carry_ref):
    TS = xr_ref.shape[1]
    C = xr_ref.shape[2]

    @pl.when(pl.program_id(1) == 0)
    def _():
        ext_ref[0:SUBLANES, :] = jnp.zeros((SUBLANES, C), F32)
        carry_ref[...] = jnp.zeros(carry_ref.shape, F32)

    x = xr_ref[0]
    ext_ref[SUBLANES:, :] = x
    xc = cb_ref[...] + cw_ref[CONV_W - 1:CONV_W, :] * x
    for j in range(CONV_W - 1):
        off = SUBLANES - (CONV_W - 1) + j
        xc = xc + cw_ref[j:j + 1, :] * ext_ref[off:off + TS, :]
    ext_ref[0:SUBLANES, :] = x[TS - SUBLANES:, :]

    xcb = xc.astype(BF16)
    nt = C // MXU_DIM
    ra = jnp.concatenate(
        [_dot(xcb[:, k * MXU_DIM:(k + 1) * MXU_DIM], wa_ref[k]) for k in range(nt)], axis=-1)
    rx = jnp.concatenate(
        [_dot(xcb[:, k * MXU_DIM:(k + 1) * MXU_DIM], wx_ref[k]) for k in range(nt)], axis=-1)
    r = jax.nn.sigmoid(ra + ba_ref[...])
    i = jax.nn.sigmoid(rx + bx_ref[...])
    lam = lam_ref[...]
    softplus_neg_lam = jnp.maximum(-lam, 0.0) + jnp.log1p(jnp.exp(-jnp.abs(lam)))
    log_a = -LRU_C * r * softplus_neg_lam
    a = jnp.exp(log_a)
    one_minus = 1.0 - a * a
    root = jnp.where(one_minus > 0.0, one_minus * lax.rsqrt(one_minus), 0.0)
    a_ref[...] = a
    b_ref[...] = root * (i * xc)

    row = lax.broadcasted_iota(jnp.int32, (SUBLANES, C), 0)

    def group(gi, carry):
        r0 = pl.multiple_of(gi * SUBLANES, SUBLANES)
        a = a_ref[pl.ds(r0, SUBLANES), :]
        b = b_ref[pl.ds(r0, SUBLANES), :]
        d = 1
        while d < SUBLANES:
            keep = row >= d
            a_s = pltpu.roll(a, d, axis=0)
            b_s = pltpu.roll(b, d, axis=0)
            b = jnp.where(keep, a * b_s, 0.0) + b
            a = jnp.where(keep, a * a_s, a)
            d *= 2
        hcur = b + a * carry
        b_ref[pl.ds(r0, SUBLANES), :] = hcur
        return hcur[SUBLANES - 1:SUBLANES, :]

    carry_ref[...] = lax.fori_loop(0, TS // SUBLANES, group, carry_ref[...])

    y = b_ref[...] * jax.nn.gelu(gr_ref[0])
    out_ref[0] = _rms(y, gn_ref[...]).astype(out_ref.dtype)


def _rglru(xr, gr, cw, cb, wa, ba, wx, bx, lam, gn):
    B, S, C = xr.shape
    TS = min(SCAN_TILE, S)
    row = pl.BlockSpec((1, TS, C), lambda b, s: (b, s, 0))
    args = (cw, cb, wa, ba, wx, bx, lam, gn)
    return pl.pallas_call(
        _rglru_kernel,
        grid=(B, S // TS),
        in_specs=[row, row] + [_full(a.shape) for a in args],
        out_specs=row,
        out_shape=jax.ShapeDtypeStruct((B, S, C), BF16),
        scratch_shapes=[pltpu.VMEM((TS + SUBLANES, C), F32), pltpu.VMEM((TS, C), F32),
                        pltpu.VMEM((TS, C), F32), pltpu.VMEM((1, C), F32)],
        compiler_params=_cparams(("parallel", "arbitrary")),
        name="rglru",
    )(xr, gr, *args)


def _compress_kernel(kc_ref, vc_ref, pek_ref, pev_ref, wk1_ref, bk1_ref, wk2_ref, bk2_ref,
                     wv1_ref, bv1_ref, wv2_ref, bv2_ref, ko_ref, vo_ref):
    def one(x_ref, pe_ref, w1_ref, b1_ref, w2_ref, b2_ref, o_ref, augment):
        x = x_ref[0, 0]
        nb = x.shape[0]
        first = _dot((x + pe_ref[0:1, :]).astype(BF16), w1_ref[0])
        second = _dot((x + pe_ref[1:2, :]).astype(BF16), w1_ref[1])
        hid = jax.nn.gelu(first + pltpu.roll(second, nb - 1, axis=0) + b1_ref[...])
        o = (_dot(hid.astype(BF16), w2_ref[...]) + b2_ref[...]).astype(o_ref.dtype)
        if augment:
            n = lax.broadcasted_iota(jnp.int32, (nb, 1), 0)
            per = L_SEL // STRIDE_CMP
            hi = (n // per * L_SEL).astype(F32)
            lo = (n % per * STRIDE_CMP).astype(F32) + (L_CMP - 1) * 0.5
            o = jnp.concatenate([o, jnp.zeros((nb, HEAD_DIM), BF16), _pos_columns(hi, lo)], axis=-1)
        o_ref[0, 0] = o

    one(kc_ref, pek_ref, wk1_ref, bk1_ref, wk2_ref, bk2_ref, ko_ref, True)
    one(vc_ref, pev_ref, wv1_ref, bv1_ref, wv2_ref, bv2_ref, vo_ref, False)


def _compress(kc, vc, pek, pev, wk1, bk1, wk2, bk2, wv1, bv1, wv2, bv2):
    B, G, NB, W = kc.shape
    blk = pl.BlockSpec((1, 1, NB, W), lambda b, g: (b, g, 0, 0))
    oblk = lambda w: pl.BlockSpec((1, 1, NB, w), lambda b, g: (b, g, 0, 0))
    args = (pek, pev, wk1, bk1, wk2, bk2, wv1, bv1, wv2, bv2)
    oshape = lambda w: jax.ShapeDtypeStruct((B, G, NB, w), BF16)
    return pl.pallas_call(
        _compress_kernel,
        grid=(B, G),
        in_specs=[blk, blk] + [_full(a.shape) for a in args],
        out_specs=[oblk(AUG_W), oblk(HEAD_DIM)],
        out_shape=[oshape(AUG_W), oshape(HEAD_DIM)],
        compiler_params=_cparams(("parallel", "parallel")),
        name="compress",
    )(kc, vc, *args)


def _normalized(acc):
    return acc[:, :HEAD_DIM] * (1.0 / acc[:, HEAD_DIM:HEAD_DIM + 1])


def _selection_rank(score_t):
    NS, TQ = score_t.shape
    sub = lax.broadcasted_iota(jnp.int32, (SUBLANES, TQ), 0)
    groups = [score_t[v * SUBLANES:(v + 1) * SUBLANES, :] for v in range(NS // SUBLANES)]
    ranks = [jnp.zeros((SUBLANES, TQ), F32) for _ in groups]
    for j in range(NS):
        sj = score_t[j:j + 1, :]
        for v, sv in enumerate(groups):
            if j < v * SUBLANES:
                ahead = sj >= sv
            elif j >= (v + 1) * SUBLANES:
                ahead = sj > sv
            else:
                ahead = (sj > sv) | ((sj == sv) & (sub > j - v * SUBLANES))
            ranks[v] = ranks[v] + jnp.where(ahead, 1.0, 0.0)
    return jnp.concatenate(ranks, axis=0)


def _nsa_kernel(q_ref, kcmp_ref, vcmp_ref, ovlt_ref, ks_ref, vs_ref, kw_ref, vw_ref, gate_ref,
                out_ref, s_ref, mx_ref, acc_ref, part_ref, gs_ref, tiles_ref, *, n_top):
    TQ = q_ref.shape[2]
    M = GQA_R * TQ
    NC = kcmp_ref.shape[2]
    S = ks_ref.shape[2]
    NS = S // L_SEL
    g = pl.program_id(1)
    t0 = pl.program_id(2) * TQ
    row = lax.broadcasted_iota(jnp.int32, (M, 1), 0)
    t_i = t0 + (row & (TQ - 1))
    slope = jnp.zeros((M, 1), F32)
    for r in range(GQA_R):
        s_r = jnp.where(g == 0, 2.0 ** -(r + 1), 2.0 ** -(GQA_R + r + 1)).astype(F32)
        slope = jnp.where(row // TQ == r, s_r, slope)
    lane = lax.broadcasted_iota(jnp.int32, (M, LANES), 1)
    slope_cols = jnp.where(lane < 2, slope, 0.0).astype(BF16)
    q4 = jnp.concatenate([q_ref[0, r] for r in range(GQA_R)], axis=0)
    qa0 = jnp.concatenate([q4, jnp.zeros((M, HEAD_DIM), BF16), slope_cols], axis=-1)

    WK = WINDOW + TQ
    start = pl.multiple_of(jnp.maximum(t0 - WINDOW, 0), TQ)
    s = _dot_nt(qa0, kw_ref[0, 0, pl.ds(start, WK), :])
    dw = t_i - (start + lax.broadcasted_iota(jnp.int32, (1, WK), 1))
    s = jnp.where(dw.astype(jnp.uint32) < WINDOW, s, NEG)
    e = jnp.exp((s - jnp.max(s, axis=-1, keepdims=True)).astype(BF16))
    o_w = _normalized(_dot(e, vw_ref[0, 0, pl.ds(start, WK), :]))

    n_i = lax.broadcasted_iota(jnp.int32, (1, NC), 1)
    mask_c = (n_i * STRIDE_CMP + (L_CMP - 1)) <= t_i
    s = jnp.where(mask_c, _dot_nt(qa0, kcmp_ref[0, 0]), NEG)
    e = jnp.where(mask_c, jnp.exp(s - jnp.max(s, axis=-1, keepdims=True)), 0.0)
    p = e * (1.0 / jnp.maximum(jnp.sum(e, axis=-1, keepdims=True), 1e-30))
    o_c = _dot(p.astype(BF16), vcmp_ref[0, 0])
    gl = gate_ref[0]
    gsel = jnp.where(g == 0, gl[:, :LANES // 2], gl[:, LANES // 2:])

    def gate_col(c):
        return jnp.concatenate([gsel[:, 3 * r + c:3 * r + c + 1] for r in range(GQA_R)], axis=0)

    part_ref[...] = gate_col(0) * o_c + gate_col(2) * o_w
    gs_ref[...] = jnp.broadcast_to(gate_col(1), (M, HEAD_DIM))
    p_sum = p[0:TQ]
    for r in range(1, GQA_R):
        p_sum = p_sum + p[r * TQ:(r + 1) * TQ]

    imp_t = lax.dot_general(ovlt_ref[...], p_sum, (((1,), (1,)), ((), ())),
                            preferred_element_type=F32, precision=lax.Precision.HIGHEST)
    blk = lax.broadcasted_iota(jnp.int32, (NS, TQ), 0)
    tq = t0 + lax.broadcasted_iota(jnp.int32, (NS, TQ), 1)
    cur = tq // L_SEL
    valid = (blk * L_SEL) <= tq
    forced = valid & ((blk == 0) | (blk == cur) | (blk == cur - 1))
    score_t = jnp.where(forced, BIG, jnp.where(valid, imp_t, NEG))
    unsel_t = jnp.where(_selection_rank(score_t) < n_top, 0.0, 1.0).astype(BF16)
    eye = jnp.where(lax.broadcasted_iota(jnp.int32, (TQ, TQ), 0)
                    == lax.broadcasted_iota(jnp.int32, (TQ, TQ), 1), 1.0, 0.0).astype(BF16)
    bias = (_dot_nt(eye, unsel_t) * NEG).astype(BF16)
    if NS < HEAD_DIM:
        bias = jnp.concatenate([bias, jnp.zeros((TQ, HEAD_DIM - NS), BF16)], axis=-1)
    qa = jnp.concatenate([q4, jnp.concatenate([bias] * GQA_R, axis=0), slope_cols], axis=-1)

    n_lane_tiles = K_TILE // LANES
    jd = t0 // K_TILE

    def lane_fold(x, op):
        out = x[:, 0:LANES]
        for c in range(1, n_lane_tiles):
            out = op(out, x[:, c * LANES:(c + 1) * LANES])
        return out

    def keys(j):
        return ks_ref[0, 0, pl.ds(pl.multiple_of(j * K_TILE, K_TILE), K_TILE), :]

    pos = jd * K_TILE + lax.broadcasted_iota(jnp.int32, (1, K_TILE), 1)
    s = jnp.where(pos <= t_i, _dot_nt(qa, keys(jd)), NEG)
    s_ref[jd] = s
    mx_ref[...] = lane_fold(s, jnp.maximum)

    def two_at_a_time(n, tile_fn):
        def pair(i, c):
            tile_fn(2 * i)
            tile_fn(2 * i + 1)
            return c

        lax.fori_loop(0, n // 2, pair, 0)

        @pl.when(n % 2 == 1)
        def _():
            tile_fn(n - 1)

    blocks_per_tile = K_TILE // L_SEL
    picked = jnp.max(1.0 - unsel_t.astype(F32), axis=1, keepdims=True)
    n_tiles = NS // blocks_per_tile
    n_live = jnp.int32(0)
    for j in range(n_tiles):
        hit = jnp.max(picked[j * blocks_per_tile:(j + 1) * blocks_per_tile, :]) > 0.0
        live = hit & (j < jd)
        tiles_ref[jnp.where(live, n_live, n_tiles)] = j
        n_live = n_live + live.astype(jnp.int32)

    def qk_tile(i):
        j = tiles_ref[i]
        s = _dot_nt(qa, keys(j))
        s_ref[j] = s
        mx_ref[...] = jnp.maximum(mx_ref[...], lane_fold(s, jnp.maximum))

    two_at_a_time(n_live, qk_tile)
    mx_ref[...] = jnp.broadcast_to(jnp.max(mx_ref[...], axis=-1, keepdims=True), (M, LANES))

    def pv(j):
        mb = mx_ref[...]
        p = jnp.exp((s_ref[j] - jnp.concatenate([mb] * n_lane_tiles, axis=-1)).astype(BF16))
        k0 = pl.multiple_of(j * K_TILE, K_TILE)
        return _dot(p, vs_ref[0, 0, pl.ds(k0, K_TILE), :])

    acc_ref[...] = pv(jd)

    def pv_tile(i):
        acc_ref[...] += pv(tiles_ref[i])

    two_at_a_time(n_live, pv_tile)
    o_s = _normalized(acc_ref[...])

    o = part_ref[...] + gs_ref[...] * o_s
    out_ref[0] = jnp.concatenate([o[r * TQ:(r + 1) * TQ] for r in range(GQA_R)],
                                 axis=-1).astype(out_ref.dtype)


def _nsa(q, kcmp, vcmp, ovlt, ks, vs, kw, vw, gates):
    B, _, S, _ = q.shape
    NC = kcmp.shape[2]
    TQ = min(Q_TILE, S)
    assert TQ & (TQ - 1) == 0 and S // L_SEL <= HEAD_DIM and S >= WINDOW + TQ
    n_top = min(N_SEL, S // L_SEL)
    M = GQA_R * TQ
    cmp_spec = lambda w: pl.BlockSpec((1, 1, NC, w), lambda b, g, i: (b, g, 0, 0))
    seq_spec = lambda w: pl.BlockSpec((1, 1, S, w), lambda b, g, i: (b, g, 0, 0))
    return pl.pallas_call(
        functools.partial(_nsa_kernel, n_top=n_top),
        grid=(B, N_KV, S // TQ),
        in_specs=[pl.BlockSpec((1, GQA_R, TQ, HEAD_DIM), lambda b, g, i: (b, g, i, 0)),
                  cmp_spec(AUG_W), cmp_spec(HEAD_DIM), _full(ovlt.shape),
                  seq_spec(AUG_W), seq_spec(LANES), seq_spec(AUG_W), seq_spec(LANES),
                  pl.BlockSpec((1, TQ, LANES), lambda b, g, i: (b, i, 0))],
        out_specs=pl.BlockSpec((1, TQ, GQA_R * HEAD_DIM), lambda b, g, i: (b, i, g)),
        out_shape=jax.ShapeDtypeStruct((B, S, D_ATT), F32),
        scratch_shapes=[pltpu.VMEM((pl.cdiv(S, K_TILE), M, K_TILE), F32),
                        pltpu.VMEM((M, LANES), F32), pltpu.VMEM((M, LANES), F32),
                        pltpu.VMEM((M, HEAD_DIM), F32), pltpu.VMEM((M, HEAD_DIM), F32),
                        pltpu.SMEM((pl.cdiv(S, K_TILE) + 1,), jnp.int32)],
        compiler_params=_cparams(("parallel", "parallel", "arbitrary")),
        name="nsa",
    )(q, kcmp, vcmp, ovlt, ks, vs, kw, vw, gates)


def _out_proj_kernel(h_ref, rec_ref, att_ref, an_ref, wor_ref, woa_ref, fn_ref, wr_ref, br_ref,
                     h2_ref, u_ref, route_ref):
    att = _rms(att_ref[...], an_ref[...]).astype(BF16)
    h2 = h_ref[...] + _dot(rec_ref[...], wor_ref[...]) + _dot(att, woa_ref[...])
    h2_ref[...] = h2
    u = _rms(h2, fn_ref[...])
    ub = u.astype(BF16)
    u_ref[...] = ub
    lg = _dot(ub, wr_ref[...]) + br_ref[...]
    lane = lax.broadcasted_iota(jnp.int32, lg.shape, 1)
    is_grp = lane < N_GROUPS
    gl = jnp.where(is_grp, lg, NEG)
    gm = jnp.max(gl, axis=-1, keepdims=True)
    p_g = 1.0 / jnp.sum(jnp.where(is_grp, jnp.exp(gl - gm), 0.0), axis=-1, keepdims=True)
    g_idx = jnp.min(jnp.where(gl == gm, lane, LANES), axis=-1, keepdims=True)
    lo = N_GROUPS + EXP_PER_GROUP * g_idx
    in_g = (lane >= lo) & (lane < lo + EXP_PER_GROUP)
    el = jnp.where(in_g, lg, NEG)
    v1 = jnp.max(el, axis=-1, keepdims=True)
    i1 = jnp.min(jnp.where(in_g & (el == v1), lane, LANES), axis=-1, keepdims=True)
    rest = in_g & (lane != i1)
    el2 = jnp.where(rest, lg, NEG)
    v2 = jnp.max(el2, axis=-1, keepdims=True)
    i2 = jnp.min(jnp.where(rest & (el2 == v2), lane, LANES), axis=-1, keepdims=True)
    e2 = jnp.exp(v2 - v1)
    den = 1.0 + e2
    route = jnp.where(lane == 0, (i1 - N_GROUPS).astype(F32),
                      jnp.where(lane == 1, (i2 - N_GROUPS).astype(F32),
                                jnp.where(lane == 2, p_g * (1.0 / den),
                                          jnp.where(lane == 3, p_g * (e2 / den), 0.0))))
    route_ref[...] = route


def _out_proj(h, rec, att, an, wor, woa, fn, wr, br):
    N, D = h.shape
    TM = min(ROW_TILE, N)
    row = lambda w: pl.BlockSpec((TM, w), lambda i: (i, 0))
    args = (an, wor, woa, fn, wr, br)
    return pl.pallas_call(
        _out_proj_kernel,
        grid=(N // TM,),
        in_specs=[row(D), row(D_REC), row(D_ATT)] + [_full(a.shape) for a in args],
        out_specs=[row(D), row(D), row(LANES)],
        out_shape=[jax.ShapeDtypeStruct((N, D), F32), jax.ShapeDtypeStruct((N, D), BF16),
                   jax.ShapeDtypeStruct((N, LANES), F32)],
        compiler_params=_cparams(("parallel",)),
        name="out_proj",
    )(h, rec, att, *args)


def _expert_kernel(ce_ref, xs_ref, wg_ref, wu_ref, wd_ref, *rest):
    y_ref, wgb, wub, wdb = rest[-4:]
    c = pl.program_id(0)
    prev = ce_ref[jnp.maximum(c - 1, 0)]

    @pl.when((c == 0) | (ce_ref[c] != prev))
    def _():
        wgb[...] = wg_ref[0, 0].astype(BF16)
        wub[...] = wu_ref[0, 0].astype(BF16)
        wdb[...] = wd_ref[0, 0].astype(BF16)

    xs = xs_ref[...]
    hdn = jax.nn.silu(_dot(xs, wgb[...])) * _dot(xs, wub[...])
    y_ref[...] = _dot(hdn.astype(BF16), wdb[...]).astype(y_ref.dtype)


def _experts(chunk_e, xs, wg, wu, wd, layer, y_prev, piece, n_pieces):
    P, D = xs.shape
    DE = wg.shape[-1]
    C = EXPERT_CHUNK
    first = piece * (P // C)
    in_specs = [pl.BlockSpec((C, D), lambda c, ce: (c, 0)),
                pl.BlockSpec((1, 1, D, DE), lambda c, ce: (layer, ce[c], 0, 0)),
                pl.BlockSpec((1, 1, D, DE), lambda c, ce: (layer, ce[c], 0, 0)),
                pl.BlockSpec((1, 1, DE, D), lambda c, ce: (layer, ce[c], 0, 0))]
    args = [chunk_e, xs, wg, wu, wd]
    aliases = {}
    if y_prev is not None:
        in_specs.append(pl.BlockSpec(memory_space=pl.ANY))
        args.append(y_prev)
        aliases = {len(args) - 1: 0}
    grid_spec = pltpu.PrefetchScalarGridSpec(
        num_scalar_prefetch=1,
        grid=(P // C,),
        in_specs=in_specs,
        out_specs=pl.BlockSpec((C, D), lambda c, ce: (first + c, 0)),
        scratch_shapes=[pltpu.VMEM((D, DE), BF16), pltpu.VMEM((D, DE), BF16),
                        pltpu.VMEM((DE, D), BF16)],
    )
    return pl.pallas_call(
        _expert_kernel,
        grid_spec=grid_spec,
        out_shape=jax.ShapeDtypeStruct((n_pieces * P, D), BF16),
        input_output_aliases=aliases,
        compiler_params=_cparams(("arbitrary",)),
        name="experts",
    )(*args)


def _slot_kernel(route_ref, dest_ref, ends_ref, run_ref):
    TM = route_ref.shape[0]
    phase = pl.program_id(0)
    step = pl.program_id(1)
    x = route_ref[...]
    lane = lax.broadcasted_iota(jnp.int32, (TM, LANES), 1).astype(F32)
    hot0 = jnp.where(lane == x[:, 0:1], 1.0, 0.0)
    hot1 = jnp.where(lane == x[:, 1:2], 1.0, 0.0)
    tot0 = jnp.sum(hot0, axis=0, keepdims=True)
    tot = tot0 + jnp.sum(hot1, axis=0, keepdims=True)

    @pl.when((phase == 0) & (step == 0))
    def _():
        run_ref[...] = jnp.zeros(run_ref.shape, F32)

    @pl.when(phase == 0)
    def _():
        run_ref[...] += tot

    @pl.when((phase == 1) & (step == 0))
    def _():
        counts = run_ref[...].astype(jnp.int32)
        padded = ((counts + (EXPERT_CHUNK - 1)) // EXPERT_CHUNK * EXPERT_CHUNK).astype(F32)
        below = jnp.where(lax.broadcasted_iota(jnp.int32, (LANES, LANES), 0)
                          < lax.broadcasted_iota(jnp.int32, (LANES, LANES), 1), 1.0, 0.0)
        starts = jnp.dot(jnp.broadcast_to(padded, (SUBLANES, LANES)), below,
                         preferred_element_type=F32, precision=lax.Precision.HIGHEST)
        ends_ref[...] = starts + padded
        run_ref[...] = starts[0:1]

    @pl.when(phase == 1)
    def _():
        earlier = jnp.where(lax.broadcasted_iota(jnp.int32, (TM, TM), 0)
                            > lax.broadcasted_iota(jnp.int32, (TM, TM), 1), 1.0, 0.0).astype(BF16)
        before0 = _dot(earlier, hot0.astype(BF16))
        before1 = _dot(earlier, hot1.astype(BF16))
        run = run_ref[...]
        d0 = jnp.sum(hot0 * (before0 + run), axis=-1, keepdims=True)
        d1 = jnp.sum(hot1 * (before1 + (run + tot0)), axis=-1, keepdims=True)
        run_ref[...] = run + tot
        both = jnp.where(lane == 0.0, d0, jnp.where(lane == 1.0, d1, 0.0))
        pick = jnp.where(lax.broadcasted_iota(jnp.int32, (SUBLANES, LANES), 0)
                         == lax.broadcasted_iota(jnp.int32, (SUBLANES, LANES), 1), 1.0, 0.0)
        dest_ref[...] = lax.dot_general(pick, both, (((1,), (1,)), ((), ())),
                                        preferred_element_type=F32,
                                        precision=lax.Precision.HIGHEST)


def _slots(route):
    N = route.shape[0]
    TM = min(ROW_TILE, N)
    return pl.pallas_call(
        _slot_kernel,
        grid=(2, N // TM),
        in_specs=[pl.BlockSpec((TM, LANES), lambda ph, i: (i, 0))],
        out_specs=[pl.BlockSpec((SUBLANES, TM), lambda ph, i: (0, i * ph)),
                   pl.BlockSpec((SUBLANES, LANES), lambda ph, i: (0, 0))],
        out_shape=[jax.ShapeDtypeStruct((SUBLANES, N), F32),
                   jax.ShapeDtypeStruct((SUBLANES, LANES), F32)],
        scratch_shapes=[pltpu.VMEM((1, LANES), F32)],
        compiler_params=_cparams(("arbitrary", "arbitrary")),
        name="slots",
    )(route)


def _dispatch(route, n_tok):
    C = EXPERT_CHUNK
    dest, ends = _slots(route)
    dest0 = dest[0].astype(jnp.int32)
    dest1 = dest[1].astype(jnp.int32)
    n_chunks = -(-2 * n_tok // C) + N_EXPERTS
    tok = jnp.arange(n_tok, dtype=jnp.int32)
    buf_tok = jnp.zeros((n_chunks * C,), jnp.int32).at[jnp.concatenate([dest0, dest1])].set(
        jnp.concatenate([tok, tok]), unique_indices=True)
    chunk_start = jnp.arange(n_chunks, dtype=jnp.int32) * C
    seg_end = ends[0, :N_EXPERTS].astype(jnp.int32)
    chunk_e = jnp.minimum(jnp.sum(seg_end[None, :] <= chunk_start[:, None], axis=1),
                          N_EXPERTS - 1).astype(jnp.int32)
    return buf_tok, chunk_e, dest0, dest1


def _ple_kernel(h_ref, y0_ref, y1_ref, route_ref, p_ref, pn_ref, wg_ref, bg_ref, wu_ref,
                fnorm_ref, o_ref, *, final):
    r = route_ref[...]
    h3 = h_ref[...] + (r[:, 2:3] * y0_ref[...].astype(F32) + r[:, 3:4] * y1_ref[...].astype(F32))
    gate = jax.nn.sigmoid(_dot(_rms(h3, pn_ref[...]).astype(BF16), wg_ref[...]) + bg_ref[...])
    h4 = h3 + _dot(p_ref[0].astype(BF16), wu_ref[...]) * gate
    o_ref[...] = _rms(h4, fnorm_ref[...]) if final else h4


def _ple(h, y0, y1, route, p, layer, pn, wg, bg, wu, fnorm, final):
    N, D = h.shape
    TM = min(ROW_TILE, N)
    row = lambda w: pl.BlockSpec((TM, w), lambda i: (i, 0))
    args = (pn, wg, bg, wu, fnorm)
    return pl.pallas_call(
        functools.partial(_ple_kernel, final=final),
        grid=(N // TM,),
        in_specs=[row(D), row(D), row(D), row(LANES),
                  pl.BlockSpec((1, TM, p.shape[2]), lambda i: (layer, i, 0))]
        + [_full(a.shape) for a in args],
        out_specs=row(D),
        out_shape=jax.ShapeDtypeStruct((N, D), F32),
        compiler_params=_cparams(("parallel",)),
        name="ple",
    )(h, y0, y1, route, p, *args)


def _block_diag_tiles(w):
    nb, k, _ = w.shape
    per = MXU_DIM // k
    tiles = jnp.zeros((nb // per, MXU_DIM, MXU_DIM), w.dtype)
    for b in range(nb):
        t, o = divmod(b, per)
        tiles = tiles.at[t, o * k:(o + 1) * k, o * k:(o + 1) * k].set(w[b])
    return tiles


def _overlap(S):
    n = jnp.arange(S // STRIDE_CMP) * STRIDE_CMP
    m = jnp.arange(S // L_SEL) * L_SEL
    ov = jnp.clip(jnp.minimum(n[:, None] + L_CMP, m[None, :] + L_SEL)
                  - jnp.maximum(n[:, None], m[None, :]), 0)
    return ov.astype(F32) / STRIDE_CMP


def kernel(x, p, mix_norm, w_in, conv_w, conv_b, lru_wa, lru_ba, lru_wx, lru_bx, lru_lambda, cmp_pe_k, cmp_pe_v, cmp_wk1, cmp_bk1, cmp_wk2, cmp_bk2, cmp_wv1, cmp_bv1, cmp_wv2, cmp_bv2, rec_out_norm, att_out_norm, w_out, ffn_norm, w_group_router, b_group_router, w_expert_router, b_expert_router, w_gate_exp, w_up_exp, w_down_exp, ple_norm, w_ple_gate, b_ple_gate, w_ple_up, final_norm):
    B, S, D = x.shape
    depth = w_in.shape[0]
    N = B * S
    half = L_CMP // 2 * HEAD_DIM
    row = lambda v: v.reshape(1, -1)
    ovlt = _overlap(S).T
    p_flat = p.reshape(depth, N, -1)
    o_q = 2 * D_REC
    o_kv = o_q + D_ATT
    o_g = o_kv + 6 * KV_W
    h = x
    for i in range(depth):
        wi = w_in[i]
        wgate = jnp.zeros((D, LANES), F32)
        for g in range(N_KV):
            wgate = wgate.at[:, g * (LANES // 2):g * (LANES // 2) + 3 * GQA_R].set(
                wi[:, o_g + g * 3 * GQA_R:o_g + (g + 1) * 3 * GQA_R])
        w_route = jnp.zeros((D, LANES), F32)
        w_route = w_route.at[:, :N_GROUPS].set(w_group_router[i])
        w_route = w_route.at[:, N_GROUPS:N_GROUPS + N_EXPERTS].set(w_expert_router[i])
        b_route = jnp.zeros((1, LANES), F32)
        b_route = b_route.at[0, :N_GROUPS].set(b_group_router[i])
        b_route = b_route.at[0, N_GROUPS:N_GROUPS + N_EXPERTS].set(b_expert_router[i])

        xr, gr, q, kc, vc, ks, vs, kw, vw, gates = _proj_in(
            h.reshape(B, S, D), row(mix_norm[i]), wi[:, :o_q].astype(BF16),
            wi[:, o_q:o_kv].astype(BF16), wi[:, o_kv:o_g].astype(BF16), wgate.astype(BF16))

        rec = _rglru(xr, gr, conv_w[i], row(conv_b[i]),
                     _block_diag_tiles(lru_wa[i]).astype(BF16), row(lru_ba[i]),
                     _block_diag_tiles(lru_wx[i]).astype(BF16), row(lru_bx[i]),
                     row(lru_lambda[i]), row(rec_out_norm[i]))

        nb = S // STRIDE_CMP
        kcmp, vcmp = _compress(
            kc.reshape(B, N_KV, nb, half), vc.reshape(B, N_KV, nb, half),
            cmp_pe_k[i].reshape(2, half), cmp_pe_v[i].reshape(2, half),
            cmp_wk1[i].reshape(2, half, -1).astype(BF16), row(cmp_bk1[i]),
            cmp_wk2[i].astype(BF16), row(cmp_bk2[i]),
            cmp_wv1[i].reshape(2, half, -1).astype(BF16), row(cmp_bv1[i]),
            cmp_wv2[i].astype(BF16), row(cmp_bv2[i]))

        att = _nsa(q, kcmp, vcmp, ovlt, ks, vs, kw, vw, gates)

        wo = w_out[i].astype(BF16)
        h2, u, route = _out_proj(
            h.reshape(N, D), rec.reshape(N, D_REC), att.reshape(N, D_ATT), row(att_out_norm[i]),
            wo[:D_REC], wo[D_REC:], row(ffn_norm[i]), w_route.astype(BF16), b_route)

        buf_tok, chunk_e, dest0, dest1 = _dispatch(route, N)
        n_chunks = chunk_e.shape[0]
        pieces = MOE_PIECES if n_chunks % MOE_PIECES == 0 else 1
        rows = n_chunks // pieces * EXPERT_CHUNK
        y = None
        for k in range(pieces):
            y = _experts(chunk_e[k * (n_chunks // pieces):(k + 1) * (n_chunks // pieces)],
                         u.at[buf_tok[k * rows:(k + 1) * rows]].get(mode="promise_in_bounds"),
                         w_gate_exp, w_up_exp, w_down_exp, i, y, k, pieces)
        y0 = y.at[dest0].get(mode="promise_in_bounds")
        y1 = y.at[dest1].get(mode="promise_in_bounds")

        last = i == depth - 1
        h = _ple(h2, y0, y1, route, p_flat, i, row(ple_norm[i]), w_ple_gate[i].astype(BF16),
                 row(b_ple_gate[i]), w_ple_up[i].astype(BF16), row(final_norm), last)
    return h.reshape(B, S, D)
```

```python
import functools

import jax
import jax.numpy as jnp
from jax import lax
from jax.experimental import pallas as pl
from jax.experimental.pallas import tpu as pltpu

F32 = jnp.float32
BF16 = jnp.bfloat16

D_REC = 512
REC_BLOCKS = 8
CONV_W = 4
LRU_C = 8.0
N_HEADS = 8
HEAD_DIM = 64
N_KV = 2
GQA_R = N_HEADS // N_KV
D_ATT = N_HEADS * HEAD_DIM
KV_W = N_KV * HEAD_DIM
L_CMP = 32
STRIDE_CMP = 16
L_SEL = 64
N_SEL = 16
WINDOW = 512
N_GROUPS = 4
EXP_PER_GROUP = 8
N_EXPERTS = N_GROUPS * EXP_PER_GROUP
EPS = 1e-6
NEG = -1e30
BIG = 1e30

LANES = 128
SUBLANES = 8
MXU_DIM = 256
VMEM_LIMIT = 56 * 1024 * 1024

ROW_TILE = 512
SCAN_TILE = 512
Q_TILE = 256
K_TILE = 512
EXPERT_CHUNK = 512
MOE_PIECES = 8
PLE_PIECES = 4


def _cparams(sem):
    return pltpu.CompilerParams(dimension_semantics=sem, vmem_limit_bytes=VMEM_LIMIT)


def _rms(x, g):
    return x * lax.rsqrt(jnp.mean(x * x, axis=-1, keepdims=True) + EPS) * g


def _dot(a, b):
    return jnp.dot(a, b, preferred_element_type=F32)


def _dot_nt(a, b):
    return lax.dot_general(a, b, (((1,), (1,)), ((), ())), preferred_element_type=F32)


def _masked_softmax(s, mask):
    s = jnp.where(mask, s, NEG)
    m = jnp.max(s, axis=-1, keepdims=True)
    e = jnp.where(mask, jnp.exp(s - m), 0.0)
    return e / jnp.maximum(jnp.sum(e, axis=-1, keepdims=True), 1e-30)


def _full(shape):
    n = len(shape)
    return pl.BlockSpec(shape, lambda *_: (0,) * n)


AUG_W = 2 * LANES


def _sel_block_onehot(pos):
    blk = lax.broadcasted_iota(jnp.int32, (pos.shape[0], HEAD_DIM), 1)
    return jnp.where(blk == pos // L_SEL, 1.0, 0.0).astype(BF16)


def _pos_columns(hi, lo):
    lane = lax.broadcasted_iota(jnp.int32, (hi.shape[0], LANES), 1)
    return jnp.where(lane == 0, hi, jnp.where(lane == 1, lo, 0.0)).astype(BF16)


def _proj_in_kernel(h_ref, g_ref, wa_ref, wq_ref, wkv_ref, wg_ref,
                    xr_ref, gr_ref, q_ref, kc_ref, vc_ref, ks_ref, vs_ref, kw_ref, vw_ref,
                    gate_ref):
    u = _rms(h_ref[0], g_ref[...]).astype(BF16)
    za = _dot(u, wa_ref[...])
    xr_ref[0] = za[:, :D_REC]
    gr_ref[0] = za[:, D_REC:]
    zq = (_dot(u, wq_ref[...]) * (HEAD_DIM ** -0.5)).astype(q_ref.dtype)
    for h in range(N_HEADS):
        q_ref[0, h] = zq[:, h * HEAD_DIM:(h + 1) * HEAD_DIM]
    zkv = _dot(u, wkv_ref[...])
    TM = zkv.shape[0]
    pos = pl.program_id(1) * TM + lax.broadcasted_iota(jnp.int32, (TM, 1), 0)
    blk_onehot = _sel_block_onehot(pos)
    no_onehot = jnp.zeros((TM, HEAD_DIM), BF16)
    ones_col = jnp.where(lax.broadcasted_iota(jnp.int32, (TM, HEAD_DIM), 1) == 0,
                         1.0, 0.0).astype(BF16)
    pos_cols = _pos_columns((pos // L_SEL * L_SEL).astype(F32), (pos % L_SEL).astype(F32))
    for i, ref in enumerate((kc_ref, vc_ref, ks_ref, vs_ref, kw_ref, vw_ref)):
        for g in range(N_KV):
            lo = i * KV_W + g * HEAD_DIM
            piece = zkv[:, lo:lo + HEAD_DIM].astype(ref.dtype)
            if ref is ks_ref:
                piece = jnp.concatenate([piece, blk_onehot, pos_cols], axis=-1)
            elif ref is kw_ref:
                piece = jnp.concatenate([piece, no_onehot, pos_cols], axis=-1)
            elif ref is vs_ref or ref is vw_ref:
                piece = jnp.concatenate([piece, ones_col], axis=-1)
            ref[0, g] = piece
    gate_ref[0] = jax.nn.sigmoid(_dot(u, wg_ref[...]))


def _proj_in(h, g, wa, wq, wkv, wg):
    B, S, D = h.shape
    TM = min(ROW_TILE, S)
    kv_f32 = jax.ShapeDtypeStruct((B, N_KV, S, HEAD_DIM), F32)
    kv_val = jax.ShapeDtypeStruct((B, N_KV, S, LANES), BF16)
    kv_aug = jax.ShapeDtypeStruct((B, N_KV, S, AUG_W), BF16)
    kv_spec = pl.BlockSpec((1, N_KV, TM, HEAD_DIM), lambda b, s: (b, 0, s, 0))
    val_spec = pl.BlockSpec((1, N_KV, TM, LANES), lambda b, s: (b, 0, s, 0))
    aug_spec = pl.BlockSpec((1, N_KV, TM, AUG_W), lambda b, s: (b, 0, s, 0))
    row = lambda w: pl.BlockSpec((1, TM, w), lambda b, s: (b, s, 0))
    return pl.pallas_call(
        _proj_in_kernel,
        grid=(B, S // TM),
        in_specs=[row(D), _full(g.shape), _full(wa.shape), _full(wq.shape), _full(wkv.shape),
                  _full(wg.shape)],
        out_specs=[row(D_REC), row(D_REC),
                   pl.BlockSpec((1, N_HEADS, TM, HEAD_DIM), lambda b, s: (b, 0, s, 0)),
                   kv_spec, kv_spec, aug_spec, val_spec, aug_spec, val_spec, row(LANES)],
        out_shape=[jax.ShapeDtypeStruct((B, S, D_REC), F32),
                   jax.ShapeDtypeStruct((B, S, D_REC), F32),
                   jax.ShapeDtypeStruct((B, N_HEADS, S, HEAD_DIM), BF16),
                   kv_f32, kv_f32, kv_aug, kv_val, kv_aug, kv_val,
                   jax.ShapeDtypeStruct((B, S, LANES), F32)],
        compiler_params=_cparams(("parallel", "parallel")),
        name="proj_in",
    )(h, g, wa, wq, wkv, wg)


def _rglru_kernel(xr_ref, gr_ref, cw_ref, cb_ref, wa_ref, ba_ref, wx_ref, bx_ref, lam_ref,
                  gn_ref, out_ref, ext_ref, a_ref, b_ref, carry_ref):
    TS = xr_ref.shape[1]
    C = xr_ref.shape[2]

    @pl.when(pl.program_id(1) == 0)
    def _():
        ext_ref[0:SUBLANES, :] = jnp.zeros((SUBLANES, C), F32)
        carry_ref[...] = jnp.zeros(carry_ref.shape, F32)

    x = xr_ref[0]
    ext_ref[SUBLANES:, :] = x
    xc = cb_ref[...] + cw_ref[CONV_W - 1:CONV_W, :] * x
    for j in range(CONV_W - 1):
        off = SUBLANES - (CONV_W - 1) + j
        xc = xc + cw_ref[j:j + 1, :] * ext_ref[off:off + TS, :]
    ext_ref[0:SUBLANES, :] = x[TS - SUBLANES:, :]

    xcb = xc.astype(BF16)
    nt = C // MXU_DIM
    ra = jnp.concatenate(
        [_dot(xcb[:, k * MXU_DIM:(k + 1) * MXU_DIM], wa_ref[k]) for k in range(nt)], axis=-1)
    rx = jnp.concatenate(
        [_dot(xcb[:, k * MXU_DIM:(k + 1) * MXU_DIM], wx_ref[k]) for k in range(nt)], axis=-1)
    r = jax.nn.sigmoid(ra + ba_ref[...])
    i = jax.nn.sigmoid(rx + bx_ref[...])
    lam = lam_ref[...]
    softplus_neg_lam = jnp.maximum(-lam, 0.0) + jnp.log1p(jnp.exp(-jnp.abs(lam)))
    log_a = -LRU_C * r * softplus_neg_lam
    a = jnp.exp(log_a)
    one_minus = 1.0 - a * a
    root = jnp.where(one_minus > 0.0, one_minus * lax.rsqrt(one_minus), 0.0)
    a_ref[...] = a
    b_ref[...] = root * (i * xc)

    row = lax.broadcasted_iota(jnp.int32, (SUBLANES, C), 0)

    def group(gi, carry):
        r0 = pl.multiple_of(gi * SUBLANES, SUBLANES)
        a = a_ref[pl.ds(r0, SUBLANES), :]
        b = b_ref[pl.ds(r0, SUBLANES), :]
        d = 1
        while d < SUBLANES:
            keep = row >= d
            a_s = pltpu.roll(a, d, axis=0)
            b_s = pltpu.roll(b, d, axis=0)
            b = jnp.where(keep, a * b_s, 0.0) + b
            a = jnp.where(keep, a * a_s, a)
            d *= 2
        hcur = b + a * carry
        b_ref[pl.ds(r0, SUBLANES), :] = hcur
        return hcur[SUBLANES - 1:SUBLANES, :]

    carry_ref[...] = lax.fori_loop(0, TS // SUBLANES, group, carry_ref[...])

    y = b_ref[...] * jax.nn.gelu(gr_ref[0])
    out_ref[0] = _rms(y, gn_ref[...]).astype(out_ref.dtype)


def _rglru(xr, gr, cw, cb, wa, ba, wx, bx, lam, gn):
    B, S, C = xr.shape
    TS = min(SCAN_TILE, S)
    row = pl.BlockSpec((1, TS, C), lambda b, s: (b, s, 0))
    args = (cw, cb, wa, ba, wx, bx, lam, gn)
    return pl.pallas_call(
        _rglru_kernel,
        grid=(B, S // TS),
        in_specs=[row, row] + [_full(a.shape) for a in args],
        out_specs=row,
        out_shape=jax.ShapeDtypeStruct((B, S, C), BF16),
        scratch_shapes=[pltpu.VMEM((TS + SUBLANES, C), F32), pltpu.VMEM((TS, C), F32),
                        pltpu.VMEM((TS, C), F32), pltpu.VMEM((1, C), F32)],
        compiler_params=_cparams(("parallel", "arbitrary")),
        name="rglru",
    )(xr, gr, *args)


def _compress_kernel(kc_ref, vc_ref, pek_ref, pev_ref, wk1_ref, bk1_ref, wk2_ref, bk2_ref,
                     wv1_ref, bv1_ref, wv2_ref, bv2_ref, ko_ref, vo_ref):
    def one(x_ref, pe_ref, w1_ref, b1_ref, w2_ref, b2_ref, o_ref, augment):
        x = x_ref[0, 0]
        nb = x.shape[0]
        first = _dot((x + pe_ref[0:1, :]).astype(BF16), w1_ref[0])
        second = _dot((x + pe_ref[1:2, :]).astype(BF16), w1_ref[1])
        hid = jax.nn.gelu(first + pltpu.roll(second, nb - 1, axis=0) + b1_ref[...])
        o = (_dot(hid.astype(BF16), w2_ref[...]) + b2_ref[...]).astype(o_ref.dtype)
        if augment:
            n = lax.broadcasted_iota(jnp.int32, (nb, 1), 0)
            per = L_SEL // STRIDE_CMP
            hi = (n // per * L_SEL).astype(F32)
            lo = (n % per * STRIDE_CMP).astype(F32) + (L_CMP - 1) * 0.5
            o = jnp.concatenate([o, jnp.zeros((nb, HEAD_DIM), BF16), _pos_columns(hi, lo)], axis=-1)
        o_ref[0, 0] = o

    one(kc_ref, pek_ref, wk1_ref, bk1_ref, wk2_ref, bk2_ref, ko_ref, True)
    one(vc_ref, pev_ref, wv1_ref, bv1_ref, wv2_ref, bv2_ref, vo_ref, False)


def _compress(kc, vc, pek, pev, wk1, bk1, wk2, bk2, wv1, bv1, wv2, bv2):
    B, G, NB, W = kc.shape
    blk = pl.BlockSpec((1, 1, NB, W), lambda b, g: (b, g, 0, 0))
    oblk = lambda w: pl.BlockSpec((1, 1, NB, w), lambda b, g: (b, g, 0, 0))
    args = (pek, pev, wk1, bk1, wk2, bk2, wv1, bv1, wv2, bv2)
    oshape = lambda w: jax.ShapeDtypeStruct((B, G, NB, w), BF16)
    return pl.pallas_call(
        _compress_kernel,
        grid=(B, G),
        in_specs=[blk, blk] + [_full(a.shape) for a in args],
        out_specs=[oblk(AUG_W), oblk(HEAD_DIM)],
        out_shape=[oshape(AUG_W), oshape(HEAD_DIM)],
        compiler_params=_cparams(("parallel", "parallel")),
        name="compress",
    )(kc, vc, *args)


def _normalized(acc):
    return acc[:, :HEAD_DIM] * (1.0 / acc[:, HEAD_DIM:HEAD_DIM + 1])


def _selection_rank(score_t):
    NS, TQ = score_t.shape
    sub = lax.broadcasted_iota(jnp.int32, (SUBLANES, TQ), 0)
    groups = [score_t[v * SUBLANES:(v + 1) * SUBLANES, :] for v in range(NS // SUBLANES)]
    ranks = [jnp.zeros((SUBLANES, TQ), F32) for _ in groups]
    for j in range(NS):
        sj = score_t[j:j + 1, :]
        for v, sv in enumerate(groups):
            if j < v * SUBLANES:
                ahead = sj >= sv
            elif j >= (v + 1) * SUBLANES:
                ahead = sj > sv
            else:
                ahead = (sj > sv) | ((sj == sv) & (sub > j - v * SUBLANES))
            ranks[v] = ranks[v] + jnp.where(ahead, 1.0, 0.0)
    return jnp.concatenate(ranks, axis=0)


def _nsa_kernel(q_ref, kcmp_ref, vcmp_ref, ovlt_ref, ks_ref, vs_ref, kw_ref, vw_ref, gate_ref,
                out_ref, s_ref, mx_ref, acc_ref, part_ref, gs_ref, tiles_ref, *, n_top):
    TQ = q_ref.shape[2]
    M = GQA_R * TQ
    NC = kcmp_ref.shape[2]
    S = ks_ref.shape[2]
    NS = S // L_SEL
    g = pl.program_id(1)
    t0 = pl.program_id(2) * TQ
    row = lax.broadcasted_iota(jnp.int32, (M, 1), 0)
    t_i = t0 + (row & (TQ - 1))
    slope = jnp.zeros((M, 1), F32)
    for r in range(GQA_R):
        s_r = jnp.where(g == 0, 2.0 ** -(r + 1), 2.0 ** -(GQA_R + r + 1)).astype(F32)
        slope = jnp.where(row // TQ == r, s_r, slope)
    lane = lax.broadcasted_iota(jnp.int32, (M, LANES), 1)
    slope_cols = jnp.where(lane < 2, slope, 0.0).astype(BF16)
    q4 = jnp.concatenate([q_ref[0, r] for r in range(GQA_R)], axis=0)
    qa0 = jnp.concatenate([q4, jnp.zeros((M, HEAD_DIM), BF16), slope_cols], axis=-1)

    WK = WINDOW + TQ
    start = pl.multiple_of(jnp.maximum(t0 - WINDOW, 0), TQ)
    s = _dot_nt(qa0, kw_ref[0, 0, pl.ds(start, WK), :])
    dw = t_i - (start + lax.broadcasted_iota(jnp.int32, (1, WK), 1))
    s = jnp.where(dw.astype(jnp.uint32) < WINDOW, s, NEG)
    e = jnp.exp((s - jnp.max(s, axis=-1, keepdims=True)).astype(BF16))
    o_w = _normalized(_dot(e, vw_ref[0, 0, pl.ds(start, WK), :]))

    n_i = lax.broadcasted_iota(jnp.int32, (1, NC), 1)
    mask_c = (n_i * STRIDE_CMP + (L_CMP - 1)) <= t_i
    s = jnp.where(mask_c, _dot_nt(qa0, kcmp_ref[0, 0]), NEG)
    e = jnp.where(mask_c, jnp.exp(s - jnp.max(s, axis=-1, keepdims=True)), 0.0)
    p = e * (1.0 / jnp.maximum(jnp.sum(e, axis=-1, keepdims=True), 1e-30))
    o_c = _dot(p.astype(BF16), vcmp_ref[0, 0])
    gl = gate_ref[0]
    gsel = jnp.where(g == 0, gl[:, :LANES // 2], gl[:, LANES // 2:])

    def gate_col(c):
        return jnp.concatenate([gsel[:, 3 * r + c:3 * r + c + 1] for r in range(GQA_R)], axis=0)

    part_ref[...] = gate_col(0) * o_c + gate_col(2) * o_w
    gs_ref[...] = jnp.broadcast_to(gate_col(1), (M, HEAD_DIM))
    p_sum = p[0:TQ]
    for r in range(1, GQA_R):
        p_sum = p_sum + p[r * TQ:(r + 1) * TQ]

    imp_t = lax.dot_general(ovlt_ref[...], p_sum, (((1,), (1,)), ((), ())),
                            preferred_element_type=F32, precision=lax.Precision.HIGHEST)
    blk = lax.broadcasted_iota(jnp.int32, (NS, TQ), 0)
    tq = t0 + lax.broadcasted_iota(jnp.int32, (NS, TQ), 1)
    cur = tq // L_SEL
    valid = (blk * L_SEL) <= tq
    forced = valid & ((blk == 0) | (blk == cur) | (blk == cur - 1))
    score_t = jnp.where(forced, BIG, jnp.where(valid, imp_t, NEG))
    unsel_t = jnp.where(_selection_rank(score_t) < n_top, 0.0, 1.0).astype(BF16)
    eye = jnp.where(lax.broadcasted_iota(jnp.int32, (TQ, TQ), 0)
                    == lax.broadcasted_iota(jnp.int32, (TQ, TQ), 1), 1.0, 0.0).astype(BF16)
    bias = (_dot_nt(eye, unsel_t) * NEG).astype(BF16)
    if NS < HEAD_DIM:
        bias = jnp.concatenate([bias, jnp.zeros((TQ, HEAD_DIM - NS), BF16)], axis=-1)
    qa = jnp.concatenate([q4, jnp.concatenate([bias] * GQA_R, axis=0), slope_cols], axis=-1)

    n_lane_tiles = K_TILE // LANES
    jd = t0 // K_TILE

    def lane_fold(x, op):
        out = x[:, 0:LANES]
        for c in range(1, n_lane_tiles):
            out = op(out, x[:, c * LANES:(c + 1) * LANES])
        return out

    def keys(j):
        return ks_ref[0, 0, pl.ds(pl.multiple_of(j * K_TILE, K_TILE), K_TILE), :]

    pos = jd * K_TILE + lax.broadcasted_iota(jnp.int32, (1, K_TILE), 1)
    s = jnp.where(pos <= t_i, _dot_nt(qa, keys(jd)), NEG)
    s_ref[jd] = s
    mx_ref[...] = lane_fold(s, jnp.maximum)

    def two_at_a_time(n, tile_fn):
        def pair(i, c):
            tile_fn(2 * i)
            tile_fn(2 * i + 1)
            return c

        lax.fori_loop(0, n // 2, pair, 0)

        @pl.when(n % 2 == 1)
        def _():
            tile_fn(n - 1)

    blocks_per_tile = K_TILE // L_SEL
    picked = jnp.max(1.0 - unsel_t.astype(F32), axis=1, keepdims=True)
    n_tiles = NS // blocks_per_tile
    n_live = jnp.int32(0)
    for j in range(n_tiles):
        hit = jnp.max(picked[j * blocks_per_tile:(j + 1) * blocks_per_tile, :]) > 0.0
        live = hit & (j < jd)
        tiles_ref[jnp.where(live, n_live, n_tiles)] = j
        n_live = n_live + live.astype(jnp.int32)

    def qk_tile(i):
        j = tiles_ref[i]
        s = _dot_nt(qa, keys(j))
        s_ref[j] = s
        mx_ref[...] = jnp.maximum(mx_ref[...], lane_fold(s, jnp.maximum))

    two_at_a_time(n_live, qk_tile)
    mx_ref[...] = jnp.broadcast_to(jnp.max(mx_ref[...], axis=-1, keepdims=True), (M, LANES))

    def pv(j):
        mb = mx_ref[...]
        p = jnp.exp((s_ref[j] - jnp.concatenate([mb] * n_lane_tiles, axis=-1)).astype(BF16))
        k0 = pl.multiple_of(j * K_TILE, K_TILE)
        return _dot(p, vs_ref[0, 0, pl.ds(k0, K_TILE), :])

    acc_ref[...] = pv(jd)

    def pv_tile(i):
        acc_ref[...] += pv(tiles_ref[i])

    two_at_a_time(n_live, pv_tile)
    o_s = _normalized(acc_ref[...])

    o = part_ref[...] + gs_ref[...] * o_s
    out_ref[0] = jnp.concatenate([o[r * TQ:(r + 1) * TQ] for r in range(GQA_R)],
                                 axis=-1).astype(out_ref.dtype)


def _nsa(q, kcmp, vcmp, ovlt, ks, vs, kw, vw, gates):
    B, _, S, _ = q.shape
    NC = kcmp.shape[2]
    TQ = min(Q_TILE, S)
    assert TQ & (TQ - 1) == 0 and S // L_SEL <= HEAD_DIM and S >= WINDOW + TQ
    n_top = min(N_SEL, S // L_SEL)
    M = GQA_R * TQ
    cmp_spec = lambda w: pl.BlockSpec((1, 1, NC, w), lambda b, g, i: (b, g, 0, 0))
    seq_spec = lambda w: pl.BlockSpec((1, 1, S, w), lambda b, g, i: (b, g, 0, 0))
    return pl.pallas_call(
        functools.partial(_nsa_kernel, n_top=n_top),
        grid=(B, N_KV, S // TQ),
        in_specs=[pl.BlockSpec((1, GQA_R, TQ, HEAD_DIM), lambda b, g, i: (b, g, i, 0)),
                  cmp_spec(AUG_W), cmp_spec(HEAD_DIM), _full(ovlt.shape),
                  seq_spec(AUG_W), seq_spec(LANES), seq_spec(AUG_W), seq_spec(LANES),
                  pl.BlockSpec((1, TQ, LANES), lambda b, g, i: (b, i, 0))],
        out_specs=pl.BlockSpec((1, TQ, GQA_R * HEAD_DIM), lambda b, g, i: (b, i, g)),
        out_shape=jax.ShapeDtypeStruct((B, S, D_ATT), F32),
        scratch_shapes=[pltpu.VMEM((pl.cdiv(S, K_TILE), M, K_TILE), F32),
                        pltpu.VMEM((M, LANES), F32), pltpu.VMEM((M, LANES), F32),
                        pltpu.VMEM((M, HEAD_DIM), F32), pltpu.VMEM((M, HEAD_DIM), F32),
                        pltpu.SMEM((pl.cdiv(S, K_TILE) + 1,), jnp.int32)],
        compiler_params=_cparams(("parallel", "parallel", "arbitrary")),
        name="nsa",
    )(q, kcmp, vcmp, ovlt, ks, vs, kw, vw, gates)


def _out_proj_kernel(h_ref, rec_ref, att_ref, an_ref, wor_ref, woa_ref, fn_ref, wr_ref, br_ref,
                     h2_ref, u_ref, route_ref, count_ref):
    att = _rms(att_ref[...], an_ref[...]).astype(BF16)
    h2 = h_ref[...] + _dot(rec_ref[...], wor_ref[...]) + _dot(att, woa_ref[...])
    h2_ref[...] = h2
    u = _rms(h2, fn_ref[...])
    ub = u.astype(BF16)
    u_ref[...] = ub
    lg = _dot(ub, wr_ref[...]) + br_ref[...]
    lane = lax.broadcasted_iota(jnp.int32, lg.shape, 1)
    is_grp = lane < N_GROUPS
    gl = jnp.where(is_grp, lg, NEG)
    gm = jnp.max(gl, axis=-1, keepdims=True)
    p_g = 1.0 / jnp.sum(jnp.where(is_grp, jnp.exp(gl - gm), 0.0), axis=-1, keepdims=True)
    g_idx = jnp.min(jnp.where(gl == gm, lane, LANES), axis=-1, keepdims=True)
    lo = N_GROUPS + EXP_PER_GROUP * g_idx
    in_g = (lane >= lo) & (lane < lo + EXP_PER_GROUP)
    el = jnp.where(in_g, lg, NEG)
    v1 = jnp.max(el, axis=-1, keepdims=True)
    i1 = jnp.min(jnp.where(in_g & (el == v1), lane, LANES), axis=-1, keepdims=True)
    rest = in_g & (lane != i1)
    el2 = jnp.where(rest, lg, NEG)
    v2 = jnp.max(el2, axis=-1, keepdims=True)
    i2 = jnp.min(jnp.where(rest & (el2 == v2), lane, LANES), axis=-1, keepdims=True)
    e2 = jnp.exp(v2 - v1)
    den = 1.0 + e2
    route = jnp.where(lane == 0, (i1 - N_GROUPS).astype(F32),
                      jnp.where(lane == 1, (i2 - N_GROUPS).astype(F32),
                                jnp.where(lane == 2, p_g * (1.0 / den),
                                          jnp.where(lane == 3, p_g * (e2 / den), 0.0))))
    route_ref[...] = route
    chosen = jnp.where((lane == i1 - N_GROUPS) | (lane == i2 - N_GROUPS), 1.0, 0.0)
    count_ref[0] = jnp.broadcast_to(jnp.sum(chosen, axis=0, keepdims=True), (SUBLANES, LANES))


def _out_proj(h, rec, att, an, wor, woa, fn, wr, br):
    N, D = h.shape
    TM = min(ROW_TILE, N)
    row = lambda w: pl.BlockSpec((TM, w), lambda i: (i, 0))
    args = (an, wor, woa, fn, wr, br)
    return pl.pallas_call(
        _out_proj_kernel,
        grid=(N // TM,),
        in_specs=[row(D), row(D_REC), row(D_ATT)] + [_full(a.shape) for a in args],
        out_specs=[row(D), row(D), row(LANES),
                   pl.BlockSpec((1, SUBLANES, LANES), lambda i: (i, 0, 0))],
        out_shape=[jax.ShapeDtypeStruct((N, D), F32), jax.ShapeDtypeStruct((N, D), BF16),
                   jax.ShapeDtypeStruct((N, LANES), F32),
                   jax.ShapeDtypeStruct((N // TM, SUBLANES, LANES), F32)],
        compiler_params=_cparams(("parallel",)),
        name="out_proj",
    )(h, rec, att, *args)


def _expert_kernel(ce_ref, xs_ref, wg_ref, wu_ref, wd_ref, *rest):
    y_ref, wgb, wub, wdb = rest[-4:]
    c = pl.program_id(0)
    prev = ce_ref[jnp.maximum(c - 1, 0)]

    @pl.when((c == 0) | (ce_ref[c] != prev))
    def _():
        wgb[...] = wg_ref[0, 0].astype(BF16)
        wub[...] = wu_ref[0, 0].astype(BF16)
        wdb[...] = wd_ref[0, 0].astype(BF16)

    xs = xs_ref[...]
    hdn = jax.nn.silu(_dot(xs, wgb[...])) * _dot(xs, wub[...])
    y_ref[...] = _dot(hdn.astype(BF16), wdb[...]).astype(y_ref.dtype)


def _experts(chunk_e, xs, wg, wu, wd, layer, y_prev, piece, n_pieces):
    P, D = xs.shape
    DE = wg.shape[-1]
    C = EXPERT_CHUNK
    first = piece * (P // C)
    in_specs = [pl.BlockSpec((C, D), lambda c, ce: (c, 0)),
                pl.BlockSpec((1, 1, D, DE), lambda c, ce: (layer, ce[c], 0, 0)),
                pl.BlockSpec((1, 1, D, DE), lambda c, ce: (layer, ce[c], 0, 0)),
                pl.BlockSpec((1, 1, DE, D), lambda c, ce: (layer, ce[c], 0, 0))]
    args = [chunk_e, xs, wg, wu, wd]
    aliases = {}
    if y_prev is not None:
        in_specs.append(pl.BlockSpec(memory_space=pl.ANY))
        args.append(y_prev)
        aliases = {len(args) - 1: 0}
    grid_spec = pltpu.PrefetchScalarGridSpec(
        num_scalar_prefetch=1,
        grid=(P // C,),
        in_specs=in_specs,
        out_specs=pl.BlockSpec((C, D), lambda c, ce: (first + c, 0)),
        scratch_shapes=[pltpu.VMEM((D, DE), BF16), pltpu.VMEM((D, DE), BF16),
                        pltpu.VMEM((DE, D), BF16)],
    )
    return pl.pallas_call(
        _expert_kernel,
        grid_spec=grid_spec,
        out_shape=jax.ShapeDtypeStruct((n_pieces * P, D), BF16),
        input_output_aliases=aliases,
        compiler_params=_cparams(("arbitrary",)),
        name="experts",
    )(*args)


def _slot_kernel(route_ref, count_ref, dest_ref, ends_ref, run_ref):
    TM = route_ref.shape[0]

    @pl.when(pl.program_id(0) == 0)
    def _():
        counts = jnp.sum(count_ref[...], axis=0).astype(jnp.int32)
        padded = ((counts + (EXPERT_CHUNK - 1)) // EXPERT_CHUNK * EXPERT_CHUNK).astype(F32)
        below = jnp.where(lax.broadcasted_iota(jnp.int32, (LANES, LANES), 0)
                          < lax.broadcasted_iota(jnp.int32, (LANES, LANES), 1), 1.0, 0.0)
        starts = jnp.dot(padded, below, preferred_element_type=F32,
                         precision=lax.Precision.HIGHEST)
        ends_ref[...] = starts + padded
        run_ref[...] = starts[0:1]

    x = route_ref[...]
    lane = lax.broadcasted_iota(jnp.int32, (TM, LANES), 1).astype(F32)
    hot0 = jnp.where(lane == x[:, 0:1], 1.0, 0.0)
    hot1 = jnp.where(lane == x[:, 1:2], 1.0, 0.0)
    tot0 = jnp.sum(hot0, axis=0, keepdims=True)
    earlier = jnp.where(lax.broadcasted_iota(jnp.int32, (TM, TM), 0)
                        > lax.broadcasted_iota(jnp.int32, (TM, TM), 1), 1.0, 0.0).astype(BF16)
    before0 = _dot(earlier, hot0.astype(BF16))
    before1 = _dot(earlier, hot1.astype(BF16))
    run = run_ref[...]
    d0 = jnp.sum(hot0 * (before0 + run), axis=-1, keepdims=True)
    d1 = jnp.sum(hot1 * (before1 + (run + tot0)), axis=-1, keepdims=True)
    run_ref[...] = run + tot0 + jnp.sum(hot1, axis=0, keepdims=True)
    both = jnp.where(lane == 0.0, d0, jnp.where(lane == 1.0, d1, 0.0))
    pick = jnp.where(lax.broadcasted_iota(jnp.int32, (SUBLANES, LANES), 0)
                     == lax.broadcasted_iota(jnp.int32, (SUBLANES, LANES), 1), 1.0, 0.0)
    dest_ref[...] = lax.dot_general(pick, both, (((1,), (1,)), ((), ())),
                                    preferred_element_type=F32,
                                    precision=lax.Precision.HIGHEST)


def _slots(route, counts):
    N = route.shape[0]
    TM = min(ROW_TILE, N)
    return pl.pallas_call(
        _slot_kernel,
        grid=(N // TM,),
        in_specs=[pl.BlockSpec((TM, LANES), lambda i: (i, 0)), _full(counts.shape)],
        out_specs=[pl.BlockSpec((SUBLANES, TM), lambda i: (0, i)),
                   pl.BlockSpec((SUBLANES, LANES), lambda i: (0, 0))],
        out_shape=[jax.ShapeDtypeStruct((SUBLANES, N), F32),
                   jax.ShapeDtypeStruct((SUBLANES, LANES), F32)],
        scratch_shapes=[pltpu.VMEM((1, LANES), F32)],
        compiler_params=_cparams(("arbitrary",)),
        name="slots",
    )(route, counts)


def _dispatch(route, counts, n_tok):
    C = EXPERT_CHUNK
    dest, ends = _slots(route, counts)
    dest0 = dest[0].astype(jnp.int32)
    dest1 = dest[1].astype(jnp.int32)
    n_chunks = -(-2 * n_tok // C) + N_EXPERTS
    tok = jnp.arange(n_tok, dtype=jnp.int32)
    buf_tok = jnp.zeros((n_chunks * C,), jnp.int32).at[jnp.concatenate([dest0, dest1])].set(
        jnp.concatenate([tok, tok]), unique_indices=True)
    chunk_start = jnp.arange(n_chunks, dtype=jnp.int32) * C
    seg_end = ends[0, :N_EXPERTS].astype(jnp.int32)
    chunk_e = jnp.minimum(jnp.sum(seg_end[None, :] <= chunk_start[:, None], axis=1),
                          N_EXPERTS - 1).astype(jnp.int32)
    return buf_tok, chunk_e, dest0, dest1


def _ple_kernel(h_ref, y0_ref, y1_ref, route_ref, p_ref, pn_ref, wg_ref, bg_ref, wu_ref,
                fnorm_ref, o_ref, *, final):
    r = route_ref[...]
    h3 = h_ref[...] + (r[:, 2:3] * y0_ref[...].astype(F32) + r[:, 3:4] * y1_ref[...].astype(F32))
    gate = jax.nn.sigmoid(_dot(_rms(h3, pn_ref[...]).astype(BF16), wg_ref[...]) + bg_ref[...])
    h4 = h3 + _dot(p_ref[0].astype(BF16), wu_ref[...]) * gate
    o_ref[...] = _rms(h4, fnorm_ref[...]) if final else h4


def _ple_kernel_piece(h_ref, y0_ref, y1_ref, route_ref, p_ref, pn_ref, wg_ref, bg_ref, wu_ref,
                      fnorm_ref, *rest, final):
    _ple_kernel(h_ref, y0_ref, y1_ref, route_ref, p_ref, pn_ref, wg_ref, bg_ref, wu_ref,
                fnorm_ref, rest[-1], final=final)


def _ple(h, y0, y1, route, p, layer, pn, wg, bg, wu, fnorm, final, out_prev, piece, n_pieces):
    N, D = h.shape
    TM = min(ROW_TILE, N)
    steps = N // n_pieces // TM
    first = piece * steps
    row = lambda w: pl.BlockSpec((TM, w), lambda i: (first + i, 0))
    local = pl.BlockSpec((TM, D), lambda i: (i, 0))
    consts = (pn, wg, bg, wu, fnorm)
    in_specs = [row(D), local, local, row(LANES),
                pl.BlockSpec((1, TM, p.shape[2]), lambda i: (layer, first + i, 0))]
    in_specs += [_full(a.shape) for a in consts]
    args = [h, y0, y1, route, p, *consts]
    aliases = {}
    if out_prev is not None:
        in_specs.append(pl.BlockSpec(memory_space=pl.ANY))
        args.append(out_prev)
        aliases = {len(args) - 1: 0}
    return pl.pallas_call(
        functools.partial(_ple_kernel_piece, final=final),
        grid=(steps,),
        in_specs=in_specs,
        out_specs=row(D),
        out_shape=jax.ShapeDtypeStruct((N, D), F32),
        input_output_aliases=aliases,
        compiler_params=_cparams(("parallel",)),
        name="ple",
    )(*args)


def _block_diag_tiles(w):
    nb, k, _ = w.shape
    per = MXU_DIM // k
    tiles = jnp.zeros((nb // per, MXU_DIM, MXU_DIM), w.dtype)
    for b in range(nb):
        t, o = divmod(b, per)
        tiles = tiles.at[t, o * k:(o + 1) * k, o * k:(o + 1) * k].set(w[b])
    return tiles


def _overlap(S):
    n = jnp.arange(S // STRIDE_CMP) * STRIDE_CMP
    m = jnp.arange(S // L_SEL) * L_SEL
    ov = jnp.clip(jnp.minimum(n[:, None] + L_CMP, m[None, :] + L_SEL)
                  - jnp.maximum(n[:, None], m[None, :]), 0)
    return ov.astype(F32) / STRIDE_CMP


def kernel(x, p, mix_norm, w_in, conv_w, conv_b, lru_wa, lru_ba, lru_wx, lru_bx, lru_lambda, cmp_pe_k, cmp_pe_v, cmp_wk1, cmp_bk1, cmp_wk2, cmp_bk2, cmp_wv1, cmp_bv1, cmp_wv2, cmp_bv2, rec_out_norm, att_out_norm, w_out, ffn_norm, w_group_router, b_group_router, w_expert_router, b_expert_router, w_gate_exp, w_up_exp, w_down_exp, ple_norm, w_ple_gate, b_ple_gate, w_ple_up, final_norm):
    B, S, D = x.shape
    depth = w_in.shape[0]
    N = B * S
    half = L_CMP // 2 * HEAD_DIM
    row = lambda v: v.reshape(1, -1)
    ovlt = _overlap(S).T
    p_flat = p.reshape(depth, N, -1)
    o_q = 2 * D_REC
    o_kv = o_q + D_ATT
    o_g = o_kv + 6 * KV_W
    h = x
    for i in range(depth):
        wi = w_in[i]
        wgate = jnp.zeros((D, LANES), F32)
        for g in range(N_KV):
            wgate = wgate.at[:, g * (LANES // 2):g * (LANES // 2) + 3 * GQA_R].set(
                wi[:, o_g + g * 3 * GQA_R:o_g + (g + 1) * 3 * GQA_R])
        w_route = jnp.zeros((D, LANES), F32)
        w_route = w_route.at[:, :N_GROUPS].set(w_group_router[i])
        w_route = w_route.at[:, N_GROUPS:N_GROUPS + N_EXPERTS].set(w_expert_router[i])
        b_route = jnp.zeros((1, LANES), F32)
        b_route = b_route.at[0, :N_GROUPS].set(b_group_router[i])
        b_route = b_route.at[0, N_GROUPS:N_GROUPS + N_EXPERTS].set(b_expert_router[i])

        xr, gr, q, kc, vc, ks, vs, kw, vw, gates = _proj_in(
            h.reshape(B, S, D), row(mix_norm[i]), wi[:, :o_q].astype(BF16),
            wi[:, o_q:o_kv].astype(BF16), wi[:, o_kv:o_g].astype(BF16), wgate.astype(BF16))

        rec = _rglru(xr, gr, conv_w[i], row(conv_b[i]),
                     _block_diag_tiles(lru_wa[i]).astype(BF16), row(lru_ba[i]),
                     _block_diag_tiles(lru_wx[i]).astype(BF16), row(lru_bx[i]),
                     row(lru_lambda[i]), row(rec_out_norm[i]))

        nb = S // STRIDE_CMP
        kcmp, vcmp = _compress(
            kc.reshape(B, N_KV, nb, half), vc.reshape(B, N_KV, nb, half),
            cmp_pe_k[i].reshape(2, half), cmp_pe_v[i].reshape(2, half),
            cmp_wk1[i].reshape(2, half, -1).astype(BF16), row(cmp_bk1[i]),
            cmp_wk2[i].astype(BF16), row(cmp_bk2[i]),
            cmp_wv1[i].reshape(2, half, -1).astype(BF16), row(cmp_bv1[i]),
            cmp_wv2[i].astype(BF16), row(cmp_bv2[i]))

        att = _nsa(q, kcmp, vcmp, ovlt, ks, vs, kw, vw, gates)

        wo = w_out[i].astype(BF16)
        h2, u, route, counts = _out_proj(
            h.reshape(N, D), rec.reshape(N, D_REC), att.reshape(N, D_ATT), row(att_out_norm[i]),
            wo[:D_REC], wo[D_REC:], row(ffn_norm[i]), w_route.astype(BF16), b_route)

        buf_tok, chunk_e, dest0, dest1 = _dispatch(route, counts, N)
        n_chunks = chunk_e.shape[0]
        pieces = MOE_PIECES if n_chunks % MOE_PIECES == 0 else 1
        rows = n_chunks // pieces * EXPERT_CHUNK
        y = None
        for k in range(pieces):
            y = _experts(chunk_e[k * (n_chunks // pieces):(k + 1) * (n_chunks // pieces)],
                         u.at[buf_tok[k * rows:(k + 1) * rows]].get(mode="promise_in_bounds"),
                         w_gate_exp, w_up_exp, w_down_exp, i, y, k, pieces)
        last = i == depth - 1
        rows = N // PLE_PIECES
        h = None
        for k in range(PLE_PIECES):
            sl = slice(k * rows, (k + 1) * rows)
            h = _ple(h2, y.at[dest0[sl]].get(mode="promise_in_bounds"),
                     y.at[dest1[sl]].get(mode="promise_in_bounds"), route, p_flat, i,
                     row(ple_norm[i]), w_ple_gate[i].astype(BF16), row(b_ple_gate[i]),
                     w_ple_up[i].astype(BF16), row(final_norm), last, h, k, PLE_PIECES)
    return h.reshape(B, S, D)
```

```python
import functools

import jax
import jax.numpy as jnp
from jax import lax
from jax.experimental import pallas as pl
from jax.experimental.pallas import tpu as pltpu

F32 = jnp.float32
BF16 = jnp.bfloat16

D_REC = 512
REC_BLOCKS = 8
CONV_W = 4
LRU_C = 8.0
N_HEADS = 8
HEAD_DIM = 64
N_KV = 2
GQA_R = N_HEADS // N_KV
D_ATT = N_HEADS * HEAD_DIM
KV_W = N_KV * HEAD_DIM
L_CMP = 32
STRIDE_CMP = 16
L_SEL = 64
N_SEL = 16
WINDOW = 512
N_GROUPS = 4
EXP_PER_GROUP = 8
N_EXPERTS = N_GROUPS * EXP_PER_GROUP
EPS = 1e-6
NEG = -1e30
BIG = 1e30

LANES = 128
SUBLANES = 8
MXU_DIM = 256
VMEM_LIMIT = 56 * 1024 * 1024

ROW_TILE = 512
SCAN_TILE = 512
Q_TILE = 256
K_TILE = 512
EXPERT_CHUNK = 512
MOE_PIECES = 16
PLE_PIECES = 8


def _cparams(sem):
    return pltpu.CompilerParams(dimension_semantics=sem, vmem_limit_bytes=VMEM_LIMIT)


def _rms(x, g):
    return x * lax.rsqrt(jnp.mean(x * x, axis=-1, keepdims=True) + EPS) * g


def _dot(a, b):
    return jnp.dot(a, b, preferred_element_type=F32)


def _dot_nt(a, b):
    return lax.dot_general(a, b, (((1,), (1,)), ((), ())), preferred_element_type=F32)


def _masked_softmax(s, mask):
    s = jnp.where(mask, s, NEG)
    m = jnp.max(s, axis=-1, keepdims=True)
    e = jnp.where(mask, jnp.exp(s - m), 0.0)
    return e / jnp.maximum(jnp.sum(e, axis=-1, keepdims=True), 1e-30)


def _full(shape):
    n = len(shape)
    return pl.BlockSpec(shape, lambda *_: (0,) * n)


AUG_W = 2 * LANES


def _sel_block_onehot(pos):
    blk = lax.broadcasted_iota(jnp.int32, (pos.shape[0], HEAD_DIM), 1)
    return jnp.where(blk == pos // L_SEL, 1.0, 0.0).astype(BF16)


def _pos_columns(hi, lo):
    lane = lax.broadcasted_iota(jnp.int32, (hi.shape[0], LANES), 1)
    return jnp.where(lane == 0, hi, jnp.where(lane == 1, lo, 0.0)).astype(BF16)


def _proj_in_kernel(h_ref, g_ref, wa_ref, wq_ref, wkv_ref, wg_ref,
                    xr_ref, gr_ref, q_ref, kc_ref, vc_ref, ks_ref, vs_ref, kw_ref, vw_ref,
                    gate_ref):
    u = _rms(h_ref[0], g_ref[...]).astype(BF16)
    za = _dot(u, wa_ref[...])
    xr_ref[0] = za[:, :D_REC]
    gr_ref[0] = za[:, D_REC:]
    zq = (_dot(u, wq_ref[...]) * (HEAD_DIM ** -0.5)).astype(q_ref.dtype)
    for h in range(N_HEADS):
        q_ref[0, h] = zq[:, h * HEAD_DIM:(h + 1) * HEAD_DIM]
    zkv = _dot(u, wkv_ref[...])
    TM = zkv.shape[0]
    pos = pl.program_id(1) * TM + lax.broadcasted_iota(jnp.int32, (TM, 1), 0)
    blk_onehot = _sel_block_onehot(pos)
    no_onehot = jnp.zeros((TM, HEAD_DIM), BF16)
    ones_col = jnp.where(lax.broadcasted_iota(jnp.int32, (TM, HEAD_DIM), 1) == 0,
                         1.0, 0.0).astype(BF16)
    pos_cols = _pos_columns((pos // L_SEL * L_SEL).astype(F32), (pos % L_SEL).astype(F32))
    for i, ref in enumerate((kc_ref, vc_ref, ks_ref, vs_ref, kw_ref, vw_ref)):
        for g in range(N_KV):
            lo = i * KV_W + g * HEAD_DIM
            piece = zkv[:, lo:lo + HEAD_DIM].astype(ref.dtype)
            if ref is ks_ref:
                piece = jnp.concatenate([piece, blk_onehot, pos_cols], axis=-1)
            elif ref is kw_ref:
                piece = jnp.concatenate([piece, no_onehot, pos_cols], axis=-1)
            elif ref is vs_ref or ref is vw_ref:
                piece = jnp.concatenate([piece, ones_col], axis=-1)
            ref[0, g] = piece
    gate_ref[0] = jax.nn.sigmoid(_dot(u, wg_ref[...]))


def _proj_in(h, g, wa, wq, wkv, wg):
    B, S, D = h.shape
    TM = min(ROW_TILE, S)
    kv_f32 = jax.ShapeDtypeStruct((B, N_KV, S, HEAD_DIM), F32)
    kv_val = jax.ShapeDtypeStruct((B, N_KV, S, LANES), BF16)
    kv_aug = jax.ShapeDtypeStruct((B, N_KV, S, AUG_W), BF16)
    kv_spec = pl.BlockSpec((1, N_KV, TM, HEAD_DIM), lambda b, s: (b, 0, s, 0))
    val_spec = pl.BlockSpec((1, N_KV, TM, LANES), lambda b, s: (b, 0, s, 0))
    aug_spec = pl.BlockSpec((1, N_KV, TM, AUG_W), lambda b, s: (b, 0, s, 0))
    row = lambda w: pl.BlockSpec((1, TM, w), lambda b, s: (b, s, 0))
    return pl.pallas_call(
        _proj_in_kernel,
        grid=(B, S // TM),
        in_specs=[row(D), _full(g.shape), _full(wa.shape), _full(wq.shape), _full(wkv.shape),
                  _full(wg.shape)],
        out_specs=[row(D_REC), row(D_REC),
                   pl.BlockSpec((1, N_HEADS, TM, HEAD_DIM), lambda b, s: (b, 0, s, 0)),
                   kv_spec, kv_spec, aug_spec, val_spec, aug_spec, val_spec, row(LANES)],
        out_shape=[jax.ShapeDtypeStruct((B, S, D_REC), F32),
                   jax.ShapeDtypeStruct((B, S, D_REC), F32),
                   jax.ShapeDtypeStruct((B, N_HEADS, S, HEAD_DIM), BF16),
                   kv_f32, kv_f32, kv_aug, kv_val, kv_aug, kv_val,
                   jax.ShapeDtypeStruct((B, S, LANES), F32)],
        compiler_params=_cparams(("parallel", "parallel")),
        name="proj_in",
    )(h, g, wa, wq, wkv, wg)


def _rglru_kernel(xr_ref, gr_ref, cw_ref, cb_ref, wa_ref, ba_ref, wx_ref, bx_ref, lam_ref,
                  gn_ref, out_ref, ext_ref, a_ref, b_ref, carry_ref):
    TS = xr_ref.shape[1]
    C = xr_ref.shape[2]

    @pl.when(pl.program_id(1) == 0)
    def _():
        ext_ref[0:SUBLANES, :] = jnp.zeros((SUBLANES, C), F32)
        carry_ref[...] = jnp.zeros(carry_ref.shape, F32)

    x = xr_ref[0]
    ext_ref[SUBLANES:, :] = x
    xc = cb_ref[...] + cw_ref[CONV_W - 1:CONV_W, :] * x
    for j in range(CONV_W - 1):
        off = SUBLANES - (CONV_W - 1) + j
        xc = xc + cw_ref[j:j + 1, :] * ext_ref[off:off + TS, :]
    ext_ref[0:SUBLANES, :] = x[TS - SUBLANES:, :]

    xcb = xc.astype(BF16)
    nt = C // MXU_DIM
    ra = jnp.concatenate(
        [_dot(xcb[:, k * MXU_DIM:(k + 1) * MXU_DIM], wa_ref[k]) for k in range(nt)], axis=-1)
    rx = jnp.concatenate(
        [_dot(xcb[:, k * MXU_DIM:(k + 1) * MXU_DIM], wx_ref[k]) for k in range(nt)], axis=-1)
    r = jax.nn.sigmoid(ra + ba_ref[...])
    i = jax.nn.sigmoid(rx + bx_ref[...])
    lam = lam_ref[...]
    softplus_neg_lam = jnp.maximum(-lam, 0.0) + jnp.log1p(jnp.exp(-jnp.abs(lam)))
    log_a = -LRU_C * r * softplus_neg_lam
    a = jnp.exp(log_a)
    one_minus = 1.0 - a * a
    root = jnp.where(one_minus > 0.0, one_minus * lax.rsqrt(one_minus), 0.0)
    a_ref[...] = a
    b_ref[...] = root * (i * xc)

    row = lax.broadcasted_iota(jnp.int32, (SUBLANES, C), 0)

    def group(gi, carry):
        r0 = pl.multiple_of(gi * SUBLANES, SUBLANES)
        a = a_ref[pl.ds(r0, SUBLANES), :]
        b = b_ref[pl.ds(r0, SUBLANES), :]
        d = 1
        while d < SUBLANES:
            keep = row >= d
            a_s = pltpu.roll(a, d, axis=0)
            b_s = pltpu.roll(b, d, axis=0)
            b = jnp.where(keep, a * b_s, 0.0) + b
            a = jnp.where(keep, a * a_s, a)
            d *= 2
        hcur = b + a * carry
        b_ref[pl.ds(r0, SUBLANES), :] = hcur
        return hcur[SUBLANES - 1:SUBLANES, :]

    carry_ref[...] = lax.fori_loop(0, TS // SUBLANES, group, carry_ref[...])

    y = b_ref[...] * jax.nn.gelu(gr_ref[0])
    out_ref[0] = _rms(y, gn_ref[...]).astype(out_ref.dtype)


def _rglru(xr, gr, cw, cb, wa, ba, wx, bx, lam, gn):
    B, S, C = xr.shape
    TS = min(SCAN_TILE, S)
    row = pl.BlockSpec((1, TS, C), lambda b, s: (b, s, 0))
    args = (cw, cb, wa, ba, wx, bx, lam, gn)
    return pl.pallas_call(
        _rglru_kernel,
        grid=(B, S // TS),
        in_specs=[row, row] + [_full(a.shape) for a in args],
        out_specs=row,
        out_shape=jax.ShapeDtypeStruct((B, S, C), BF16),
        scratch_shapes=[pltpu.VMEM((TS + SUBLANES, C), F32), pltpu.VMEM((TS, C), F32),
                        pltpu.VMEM((TS, C), F32), pltpu.VMEM((1, C), F32)],
        compiler_params=_cparams(("parallel", "arbitrary")),
        name="rglru",
    )(xr, gr, *args)


def _compress_kernel(kc_ref, vc_ref, pek_ref, pev_ref, wk1_ref, bk1_ref, wk2_ref, bk2_ref,
                     wv1_ref, bv1_ref, wv2_ref, bv2_ref, ko_ref, vo_ref):
    def one(x_ref, pe_ref, w1_ref, b1_ref, w2_ref, b2_ref, o_ref, augment):
        x = x_ref[0, 0]
        nb = x.shape[0]
        first = _dot((x + pe_ref[0:1, :]).astype(BF16), w1_ref[0])
        second = _dot((x + pe_ref[1:2, :]).astype(BF16), w1_ref[1])
        hid = jax.nn.gelu(first + pltpu.roll(second, nb - 1, axis=0) + b1_ref[...])
        o = (_dot(hid.astype(BF16), w2_ref[...]) + b2_ref[...]).astype(o_ref.dtype)
        if augment:
            n = lax.broadcasted_iota(jnp.int32, (nb, 1), 0)
            per = L_SEL // STRIDE_CMP
            hi = (n // per * L_SEL).astype(F32)
            lo = (n % per * STRIDE_CMP).astype(F32) + (L_CMP - 1) * 0.5
            o = jnp.concatenate([o, jnp.zeros((nb, HEAD_DIM), BF16), _pos_columns(hi, lo)], axis=-1)
        o_ref[0, 0] = o

    one(kc_ref, pek_ref, wk1_ref, bk1_ref, wk2_ref, bk2_ref, ko_ref, True)
    one(vc_ref, pev_ref, wv1_ref, bv1_ref, wv2_ref, bv2_ref, vo_ref, False)


def _compress(kc, vc, pek, pev, wk1, bk1, wk2, bk2, wv1, bv1, wv2, bv2):
    B, G, NB, W = kc.shape
    blk = pl.BlockSpec((1, 1, NB, W), lambda b, g: (b, g, 0, 0))
    oblk = lambda w: pl.BlockSpec((1, 1, NB, w), lambda b, g: (b, g, 0, 0))
    args = (pek, pev, wk1, bk1, wk2, bk2, wv1, bv1, wv2, bv2)
    oshape = lambda w: jax.ShapeDtypeStruct((B, G, NB, w), BF16)
    return pl.pallas_call(
        _compress_kernel,
        grid=(B, G),
        in_specs=[blk, blk] + [_full(a.shape) for a in args],
        out_specs=[oblk(AUG_W), oblk(HEAD_DIM)],
        out_shape=[oshape(AUG_W), oshape(HEAD_DIM)],
        compiler_params=_cparams(("parallel", "parallel")),
        name="compress",
    )(kc, vc, *args)


def _normalized(acc):
    return acc[:, :HEAD_DIM] * (1.0 / acc[:, HEAD_DIM:HEAD_DIM + 1])


def _selection_rank(score_t):
    NS, TQ = score_t.shape
    sub = lax.broadcasted_iota(jnp.int32, (SUBLANES, TQ), 0)
    groups = [score_t[v * SUBLANES:(v + 1) * SUBLANES, :] for v in range(NS // SUBLANES)]
    ranks = [jnp.zeros((SUBLANES, TQ), F32) for _ in groups]
    for j in range(NS):
        sj = score_t[j:j + 1, :]
        for v, sv in enumerate(groups):
            if j < v * SUBLANES:
                ahead = sj >= sv
            elif j >= (v + 1) * SUBLANES:
                ahead = sj > sv
            else:
                ahead = (sj > sv) | ((sj == sv) & (sub > j - v * SUBLANES))
            ranks[v] = ranks[v] + jnp.where(ahead, 1.0, 0.0)
    return jnp.concatenate(ranks, axis=0)


def _nsa_kernel(q_ref, kcmp_ref, vcmp_ref, ovlt_ref, ks_ref, vs_ref, kw_ref, vw_ref, gate_ref,
                out_ref, s_ref, mx_ref, acc_ref, part_ref, gs_ref, tiles_ref, *, n_top):
    TQ = q_ref.shape[2]
    M = GQA_R * TQ
    NC = kcmp_ref.shape[2]
    S = ks_ref.shape[2]
    NS = S // L_SEL
    g = pl.program_id(1)
    t0 = pl.program_id(2) * TQ
    row = lax.broadcasted_iota(jnp.int32, (M, 1), 0)
    t_i = t0 + (row & (TQ - 1))
    slope = jnp.zeros((M, 1), F32)
    for r in range(GQA_R):
        s_r = jnp.where(g == 0, 2.0 ** -(r + 1), 2.0 ** -(GQA_R + r + 1)).astype(F32)
        slope = jnp.where(row // TQ == r, s_r, slope)
    lane = lax.broadcasted_iota(jnp.int32, (M, LANES), 1)
    slope_cols = jnp.where(lane < 2, slope, 0.0).astype(BF16)
    q4 = jnp.concatenate([q_ref[0, r] for r in range(GQA_R)], axis=0)
    qa0 = jnp.concatenate([q4, jnp.zeros((M, HEAD_DIM), BF16), slope_cols], axis=-1)

    WK = WINDOW + TQ
    start = pl.multiple_of(jnp.maximum(t0 - WINDOW, 0), TQ)
    s = _dot_nt(qa0, kw_ref[0, 0, pl.ds(start, WK), :])
    dw = t_i - (start + lax.broadcasted_iota(jnp.int32, (1, WK), 1))
    s = jnp.where(dw.astype(jnp.uint32) < WINDOW, s, NEG)
    e = jnp.exp((s - jnp.max(s, axis=-1, keepdims=True)).astype(BF16))
    o_w = _normalized(_dot(e, vw_ref[0, 0, pl.ds(start, WK), :]))

    n_i = lax.broadcasted_iota(jnp.int32, (1, NC), 1)
    mask_c = (n_i * STRIDE_CMP + (L_CMP - 1)) <= t_i
    s = jnp.where(mask_c, _dot_nt(qa0, kcmp_ref[0, 0]), NEG)
    e = jnp.where(mask_c, jnp.exp(s - jnp.max(s, axis=-1, keepdims=True)), 0.0)
    p = e * (1.0 / jnp.maximum(jnp.sum(e, axis=-1, keepdims=True), 1e-30))
    o_c = _dot(p.astype(BF16), vcmp_ref[0, 0])
    gl = gate_ref[0]
    gsel = jnp.where(g == 0, gl[:, :LANES // 2], gl[:, LANES // 2:])

    def gate_col(c):
        return jnp.concatenate([gsel[:, 3 * r + c:3 * r + c + 1] for r in range(GQA_R)], axis=0)

    part_ref[...] = gate_col(0) * o_c + gate_col(2) * o_w
    gs_ref[...] = jnp.broadcast_to(gate_col(1), (M, HEAD_DIM))
    p_sum = p[0:TQ]
    for r in range(1, GQA_R):
        p_sum = p_sum + p[r * TQ:(r + 1) * TQ]

    imp_t = lax.dot_general(ovlt_ref[...], p_sum, (((1,), (1,)), ((), ())),
                            preferred_element_type=F32, precision=lax.Precision.HIGHEST)
    blk = lax.broadcasted_iota(jnp.int32, (NS, TQ), 0)
    tq = t0 + lax.broadcasted_iota(jnp.int32, (NS, TQ), 1)
    cur = tq // L_SEL
    valid = (blk * L_SEL) <= tq
    forced = valid & ((blk == 0) | (blk == cur) | (blk == cur - 1))
    score_t = jnp.where(forced, BIG, jnp.where(valid, imp_t, NEG))
    unsel_t = jnp.where(_selection_rank(score_t) < n_top, 0.0, 1.0).astype(BF16)
    eye = jnp.where(lax.broadcasted_iota(jnp.int32, (TQ, TQ), 0)
                    == lax.broadcasted_iota(jnp.int32, (TQ, TQ), 1), 1.0, 0.0).astype(BF16)
    bias = (_dot_nt(eye, unsel_t) * NEG).astype(BF16)
    if NS < HEAD_DIM:
        bias = jnp.concatenate([bias, jnp.zeros((TQ, HEAD_DIM - NS), BF16)], axis=-1)
    qa = jnp.concatenate([q4, jnp.concatenate([bias] * GQA_R, axis=0), slope_cols], axis=-1)

    n_lane_tiles = K_TILE // LANES
    jd = t0 // K_TILE

    def lane_fold(x, op):
        out = x[:, 0:LANES]
        for c in range(1, n_lane_tiles):
            out = op(out, x[:, c * LANES:(c + 1) * LANES])
        return out

    def keys(j):
        return ks_ref[0, 0, pl.ds(pl.multiple_of(j * K_TILE, K_TILE), K_TILE), :]

    pos = jd * K_TILE + lax.broadcasted_iota(jnp.int32, (1, K_TILE), 1)
    s = jnp.where(pos <= t_i, _dot_nt(qa, keys(jd)), NEG)
    s_ref[jd] = s
    mx_ref[...] = lane_fold(s, jnp.maximum)

    def two_at_a_time(n, tile_fn):
        def pair(i, c):
            tile_fn(2 * i)
            tile_fn(2 * i + 1)
            return c

        lax.fori_loop(0, n // 2, pair, 0)

        @pl.when(n % 2 == 1)
        def _():
            tile_fn(n - 1)

    blocks_per_tile = K_TILE // L_SEL
    picked = jnp.max(1.0 - unsel_t.astype(F32), axis=1, keepdims=True)
    n_tiles = NS // blocks_per_tile
    n_live = jnp.int32(0)
    for j in range(n_tiles):
        hit = jnp.max(picked[j * blocks_per_tile:(j + 1) * blocks_per_tile, :]) > 0.0
        live = hit & (j < jd)
        tiles_ref[jnp.where(live, n_live, n_tiles)] = j
        n_live = n_live + live.astype(jnp.int32)

    def qk_tile(i):
        j = tiles_ref[i]
        s = _dot_nt(qa, keys(j))
        s_ref[j] = s
        mx_ref[...] = jnp.maximum(mx_ref[...], lane_fold(s, jnp.maximum))

    two_at_a_time(n_live, qk_tile)
    mx_ref[...] = jnp.broadcast_to(jnp.max(mx_ref[...], axis=-1, keepdims=True), (M, LANES))

    def pv(j):
        mb = mx_ref[...]
        p = jnp.exp((s_ref[j] - jnp.concatenate([mb] * n_lane_tiles, axis=-1)).astype(BF16))
        k0 = pl.multiple_of(j * K_TILE, K_TILE)
        return _dot(p, vs_ref[0, 0, pl.ds(k0, K_TILE), :])

    acc_ref[...] = pv(jd)

    def pv_tile(i):
        acc_ref[...] += pv(tiles_ref[i])

    two_at_a_time(n_live, pv_tile)
    o_s = _normalized(acc_ref[...])

    o = part_ref[...] + gs_ref[...] * o_s
    out_ref[0] = jnp.concatenate([o[r * TQ:(r + 1) * TQ] for r in range(GQA_R)],
                                 axis=-1).astype(out_ref.dtype)


def _nsa(q, kcmp, vcmp, ovlt, ks, vs, kw, vw, gates):
    B, _, S, _ = q.shape
    NC = kcmp.shape[2]
    TQ = min(Q_TILE, S)
    assert TQ & (TQ - 1) == 0 and S // L_SEL <= HEAD_DIM and S >= WINDOW + TQ
    n_top = min(N_SEL, S // L_SEL)
    M = GQA_R * TQ
    cmp_spec = lambda w: pl.BlockSpec((1, 1, NC, w), lambda b, g, i: (b, g, 0, 0))
    seq_spec = lambda w: pl.BlockSpec((1, 1, S, w), lambda b, g, i: (b, g, 0, 0))
    return pl.pallas_call(
        functools.partial(_nsa_kernel, n_top=n_top),
        grid=(B, N_KV, S // TQ),
        in_specs=[pl.BlockSpec((1, GQA_R, TQ, HEAD_DIM), lambda b, g, i: (b, g, i, 0)),
                  cmp_spec(AUG_W), cmp_spec(HEAD_DIM), _full(ovlt.shape),
                  seq_spec(AUG_W), seq_spec(LANES), seq_spec(AUG_W), seq_spec(LANES),
                  pl.BlockSpec((1, TQ, LANES), lambda b, g, i: (b, i, 0))],
        out_specs=pl.BlockSpec((1, TQ, GQA_R * HEAD_DIM), lambda b, g, i: (b, i, g)),
        out_shape=jax.ShapeDtypeStruct((B, S, D_ATT), F32),
        scratch_shapes=[pltpu.VMEM((pl.cdiv(S, K_TILE), M, K_TILE), F32),
                        pltpu.VMEM((M, LANES), F32), pltpu.VMEM((M, LANES), F32),
                        pltpu.VMEM((M, HEAD_DIM), F32), pltpu.VMEM((M, HEAD_DIM), F32),
                        pltpu.SMEM((pl.cdiv(S, K_TILE) + 1,), jnp.int32)],
        compiler_params=_cparams(("parallel", "parallel", "arbitrary")),
        name="nsa",
    )(q, kcmp, vcmp, ovlt, ks, vs, kw, vw, gates)


def _out_proj_kernel(h_ref, rec_ref, att_ref, an_ref, wor_ref, woa_ref, fn_ref, wr_ref, br_ref,
                     h2_ref, u_ref, route_ref, count_ref):
    att = _rms(att_ref[...], an_ref[...]).astype(BF16)
    h2 = h_ref[...] + _dot(rec_ref[...], wor_ref[...]) + _dot(att, woa_ref[...])
    h2_ref[...] = h2
    u = _rms(h2, fn_ref[...])
    ub = u.astype(BF16)
    u_ref[...] = ub
    lg = _dot(ub, wr_ref[...]) + br_ref[...]
    lane = lax.broadcasted_iota(jnp.int32, lg.shape, 1)
    is_grp = lane < N_GROUPS
    gl = jnp.where(is_grp, lg, NEG)
    gm = jnp.max(gl, axis=-1, keepdims=True)
    p_g = 1.0 / jnp.sum(jnp.where(is_grp, jnp.exp(gl - gm), 0.0), axis=-1, keepdims=True)
    g_idx = jnp.min(jnp.where(gl == gm, lane, LANES), axis=-1, keepdims=True)
    lo = N_GROUPS + EXP_PER_GROUP * g_idx
    in_g = (lane >= lo) & (lane < lo + EXP_PER_GROUP)
    el = jnp.where(in_g, lg, NEG)
    v1 = jnp.max(el, axis=-1, keepdims=True)
    i1 = jnp.min(jnp.where(in_g & (el == v1), lane, LANES), axis=-1, keepdims=True)
    rest = in_g & (lane != i1)
    el2 = jnp.where(rest, lg, NEG)
    v2 = jnp.max(el2, axis=-1, keepdims=True)
    i2 = jnp.min(jnp.where(rest & (el2 == v2), lane, LANES), axis=-1, keepdims=True)
    e2 = jnp.exp(v2 - v1)
    den = 1.0 + e2
    route = jnp.where(lane == 0, (i1 - N_GROUPS).astype(F32),
                      jnp.where(lane == 1, (i2 - N_GROUPS).astype(F32),
                                jnp.where(lane == 2, p_g * (1.0 / den),
                                          jnp.where(lane == 3, p_g * (e2 / den), 0.0))))
    route_ref[...] = route
    chosen = jnp.where((lane == i1 - N_GROUPS) | (lane == i2 - N_GROUPS), 1.0, 0.0)
    count_ref[0] = jnp.broadcast_to(jnp.sum(chosen, axis=0, keepdims=True), (SUBLANES, LANES))


def _out_proj(h, rec, att, an, wor, woa, fn, wr, br):
    N, D = h.shape
    TM = min(ROW_TILE, N)
    row = lambda w: pl.BlockSpec((TM, w), lambda i: (i, 0))
    args = (an, wor, woa, fn, wr, br)
    return pl.pallas_call(
        _out_proj_kernel,
        grid=(N // TM,),
        in_specs=[row(D), row(D_REC), row(D_ATT)] + [_full(a.shape) for a in args],
        out_specs=[row(D), row(D), row(LANES),
                   pl.BlockSpec((1, SUBLANES, LANES), lambda i: (i, 0, 0))],
        out_shape=[jax.ShapeDtypeStruct((N, D), F32), jax.ShapeDtypeStruct((N, D), BF16),
                   jax.ShapeDtypeStruct((N, LANES), F32),
                   jax.ShapeDtypeStruct((N // TM, SUBLANES, LANES), F32)],
        compiler_params=_cparams(("parallel",)),
        name="out_proj",
    )(h, rec, att, *args)


def _expert_kernel(ce_ref, xs_ref, wg_ref, wu_ref, wd_ref, *rest):
    y_ref, wgb, wub, wdb = rest[-4:]
    c = pl.program_id(0)
    prev = ce_ref[jnp.maximum(c - 1, 0)]

    @pl.when((c == 0) | (ce_ref[c] != prev))
    def _():
        wgb[...] = wg_ref[0, 0].astype(BF16)
        wub[...] = wu_ref[0, 0].astype(BF16)
        wdb[...] = wd_ref[0, 0].astype(BF16)

    xs = xs_ref[...]
    hdn = jax.nn.silu(_dot(xs, wgb[...])) * _dot(xs, wub[...])
    y_ref[...] = _dot(hdn.astype(BF16), wdb[...]).astype(y_ref.dtype)


def _experts(chunk_e, xs, wg, wu, wd, layer, y_prev, piece, n_pieces):
    P, D = xs.shape
    DE = wg.shape[-1]
    C = EXPERT_CHUNK
    first = piece * (P // C)
    in_specs = [pl.BlockSpec((C, D), lambda c, ce: (c, 0)),
                pl.BlockSpec((1, 1, D, DE), lambda c, ce: (layer, ce[c], 0, 0)),
                pl.BlockSpec((1, 1, D, DE), lambda c, ce: (layer, ce[c], 0, 0)),
                pl.BlockSpec((1, 1, DE, D), lambda c, ce: (layer, ce[c], 0, 0))]
    args = [chunk_e, xs, wg, wu, wd]
    aliases = {}
    if y_prev is not None:
        in_specs.append(pl.BlockSpec(memory_space=pl.ANY))
        args.append(y_prev)
        aliases = {len(args) - 1: 0}
    grid_spec = pltpu.PrefetchScalarGridSpec(
        num_scalar_prefetch=1,
        grid=(P // C,),
        in_specs=in_specs,
        out_specs=pl.BlockSpec((C, D), lambda c, ce: (first + c, 0)),
        scratch_shapes=[pltpu.VMEM((D, DE), BF16), pltpu.VMEM((D, DE), BF16),
                        pltpu.VMEM((DE, D), BF16)],
    )
    return pl.pallas_call(
        _expert_kernel,
        grid_spec=grid_spec,
        out_shape=jax.ShapeDtypeStruct((n_pieces * P, D), BF16),
        input_output_aliases=aliases,
        compiler_params=_cparams(("arbitrary",)),
        name="experts",
    )(*args)


def _slot_kernel(route_ref, count_ref, dest_ref, ends_ref, run_ref):
    TM = route_ref.shape[0]

    @pl.when(pl.program_id(0) == 0)
    def _():
        counts = jnp.sum(count_ref[...], axis=0).astype(jnp.int32)
        padded = ((counts + (EXPERT_CHUNK - 1)) // EXPERT_CHUNK * EXPERT_CHUNK).astype(F32)
        below = jnp.where(lax.broadcasted_iota(jnp.int32, (LANES, LANES), 0)
                          < lax.broadcasted_iota(jnp.int32, (LANES, LANES), 1), 1.0, 0.0)
        starts = jnp.dot(padded, below, preferred_element_type=F32,
                         precision=lax.Precision.HIGHEST)
        ends_ref[...] = starts + padded
        run_ref[...] = starts[0:1]

    x = route_ref[...]
    lane = lax.broadcasted_iota(jnp.int32, (TM, LANES), 1).astype(F32)
    hot0 = jnp.where(lane == x[:, 0:1], 1.0, 0.0)
    hot1 = jnp.where(lane == x[:, 1:2], 1.0, 0.0)
    tot0 = jnp.sum(hot0, axis=0, keepdims=True)
    earlier = jnp.where(lax.broadcasted_iota(jnp.int32, (TM, TM), 0)
                        > lax.broadcasted_iota(jnp.int32, (TM, TM), 1), 1.0, 0.0).astype(BF16)
    before0 = _dot(earlier, hot0.astype(BF16))
    before1 = _dot(earlier, hot1.astype(BF16))
    run = run_ref[...]
    d0 = jnp.sum(hot0 * (before0 + run), axis=-1, keepdims=True)
    d1 = jnp.sum(hot1 * (before1 + (run + tot0)), axis=-1, keepdims=True)
    run_ref[...] = run + tot0 + jnp.sum(hot1, axis=0, keepdims=True)
    both = jnp.where(lane == 0.0, d0, jnp.where(lane == 1.0, d1, 0.0))
    pick = jnp.where(lax.broadcasted_iota(jnp.int32, (SUBLANES, LANES), 0)
                     == lax.broadcasted_iota(jnp.int32, (SUBLANES, LANES), 1), 1.0, 0.0)
    dest_ref[...] = lax.dot_general(pick, both, (((1,), (1,)), ((), ())),
                                    preferred_element_type=F32,
                                    precision=lax.Precision.HIGHEST)


def _slots(route, counts):
    N = route.shape[0]
    TM = min(ROW_TILE, N)
    return pl.pallas_call(
        _slot_kernel,
        grid=(N // TM,),
        in_specs=[pl.BlockSpec((TM, LANES), lambda i: (i, 0)), _full(counts.shape)],
        out_specs=[pl.BlockSpec((SUBLANES, TM), lambda i: (0, i)),
                   pl.BlockSpec((SUBLANES, LANES), lambda i: (0, 0))],
        out_shape=[jax.ShapeDtypeStruct((SUBLANES, N), F32),
                   jax.ShapeDtypeStruct((SUBLANES, LANES), F32)],
        scratch_shapes=[pltpu.VMEM((1, LANES), F32)],
        compiler_params=_cparams(("arbitrary",)),
        name="slots",
    )(route, counts)


def _dispatch(route, counts, n_tok):
    C = EXPERT_CHUNK
    dest, ends = _slots(route, counts)
    dest0 = dest[0].astype(jnp.int32)
    dest1 = dest[1].astype(jnp.int32)
    n_chunks = -(-2 * n_tok // C) + N_EXPERTS
    tok = jnp.arange(n_tok, dtype=jnp.int32)
    buf_tok = jnp.zeros((n_chunks * C,), jnp.int32).at[jnp.concatenate([dest0, dest1])].set(
        jnp.concatenate([tok, tok]), unique_indices=True)
    chunk_start = jnp.arange(n_chunks, dtype=jnp.int32) * C
    seg_end = ends[0, :N_EXPERTS].astype(jnp.int32)
    chunk_e = jnp.minimum(jnp.sum(seg_end[None, :] <= chunk_start[:, None], axis=1),
                          N_EXPERTS - 1).astype(jnp.int32)
    return buf_tok, chunk_e, dest0, dest1


def _ple_kernel(h_ref, y0_ref, y1_ref, route_ref, p_ref, pn_ref, wg_ref, bg_ref, wu_ref,
                fnorm_ref, o_ref, *, final):
    r = route_ref[...]
    h3 = h_ref[...] + (r[:, 2:3] * y0_ref[...].astype(F32) + r[:, 3:4] * y1_ref[...].astype(F32))
    gate = jax.nn.sigmoid(_dot(_rms(h3, pn_ref[...]).astype(BF16), wg_ref[...]) + bg_ref[...])
    h4 = h3 + _dot(p_ref[0].astype(BF16), wu_ref[...]) * gate
    o_ref[...] = _rms(h4, fnorm_ref[...]) if final else h4


def _ple_kernel_piece(h_ref, y0_ref, y1_ref, route_ref, p_ref, pn_ref, wg_ref, bg_ref, wu_ref,
                      fnorm_ref, *rest, final):
    _ple_kernel(h_ref, y0_ref, y1_ref, route_ref, p_ref, pn_ref, wg_ref, bg_ref, wu_ref,
                fnorm_ref, rest[-1], final=final)


def _ple(h, y0, y1, route, p, layer, pn, wg, bg, wu, fnorm, final, out_prev, piece, n_pieces):
    N, D = h.shape
    TM = min(ROW_TILE, N)
    steps = N // n_pieces // TM
    first = piece * steps
    row = lambda w: pl.BlockSpec((TM, w), lambda i: (first + i, 0))
    local = pl.BlockSpec((TM, D), lambda i: (i, 0))
    consts = (pn, wg, bg, wu, fnorm)
    in_specs = [row(D), local, local, row(LANES),
                pl.BlockSpec((1, TM, p.shape[2]), lambda i: (layer, first + i, 0))]
    in_specs += [_full(a.shape) for a in consts]
    args = [h, y0, y1, route, p, *consts]
    aliases = {}
    if out_prev is not None:
        in_specs.append(pl.BlockSpec(memory_space=pl.ANY))
        args.append(out_prev)
        aliases = {len(args) - 1: 0}
    return pl.pallas_call(
        functools.partial(_ple_kernel_piece, final=final),
        grid=(steps,),
        in_specs=in_specs,
        out_specs=row(D),
        out_shape=jax.ShapeDtypeStruct((N, D), F32),
        input_output_aliases=aliases,
        compiler_params=_cparams(("parallel",)),
        name="ple",
    )(*args)


def _block_diag_tiles(w):
    nb, k, _ = w.shape
    per = MXU_DIM // k
    tiles = jnp.zeros((nb // per, MXU_DIM, MXU_DIM), w.dtype)
    for b in range(nb):
        t, o = divmod(b, per)
        tiles = tiles.at[t, o * k:(o + 1) * k, o * k:(o + 1) * k].set(w[b])
    return tiles


def _overlap(S):
    n = jnp.arange(S // STRIDE_CMP) * STRIDE_CMP
    m = jnp.arange(S // L_SEL) * L_SEL
    ov = jnp.clip(jnp.minimum(n[:, None] + L_CMP, m[None, :] + L_SEL)
                  - jnp.maximum(n[:, None], m[None, :]), 0)
    return ov.astype(F32) / STRIDE_CMP


def kernel(x, p, mix_norm, w_in, conv_w, conv_b, lru_wa, lru_ba, lru_wx, lru_bx, lru_lambda, cmp_pe_k, cmp_pe_v, cmp_wk1, cmp_bk1, cmp_wk2, cmp_bk2, cmp_wv1, cmp_bv1, cmp_wv2, cmp_bv2, rec_out_norm, att_out_norm, w_out, ffn_norm, w_group_router, b_group_router, w_expert_router, b_expert_router, w_gate_exp, w_up_exp, w_down_exp, ple_norm, w_ple_gate, b_ple_gate, w_ple_up, final_norm):
    B, S, D = x.shape
    depth = w_in.shape[0]
    N = B * S
    half = L_CMP // 2 * HEAD_DIM
    row = lambda v: v.reshape(1, -1)
    ovlt = _overlap(S).T
    p_flat = p.reshape(depth, N, -1)
    o_q = 2 * D_REC
    o_kv = o_q + D_ATT
    o_g = o_kv + 6 * KV_W
    h = x
    for i in range(depth):
        wi = w_in[i]
        wgate = jnp.zeros((D, LANES), F32)
        for g in range(N_KV):
            wgate = wgate.at[:, g * (LANES // 2):g * (LANES // 2) + 3 * GQA_R].set(
                wi[:, o_g + g * 3 * GQA_R:o_g + (g + 1) * 3 * GQA_R])
        w_route = jnp.zeros((D, LANES), F32)
        w_route = w_route.at[:, :N_GROUPS].set(w_group_router[i])
        w_route = w_route.at[:, N_GROUPS:N_GROUPS + N_EXPERTS].set(w_expert_router[i])
        b_route = jnp.zeros((1, LANES), F32)
        b_route = b_route.at[0, :N_GROUPS].set(b_group_router[i])
        b_route = b_route.at[0, N_GROUPS:N_GROUPS + N_EXPERTS].set(b_expert_router[i])

        xr, gr, q, kc, vc, ks, vs, kw, vw, gates = _proj_in(
            h.reshape(B, S, D), row(mix_norm[i]), wi[:, :o_q].astype(BF16),
            wi[:, o_q:o_kv].astype(BF16), wi[:, o_kv:o_g].astype(BF16), wgate.astype(BF16))

        rec = _rglru(xr, gr, conv_w[i], row(conv_b[i]),
                     _block_diag_tiles(lru_wa[i]).astype(BF16), row(lru_ba[i]),
                     _block_diag_tiles(lru_wx[i]).astype(BF16), row(lru_bx[i]),
                     row(lru_lambda[i]), row(rec_out_norm[i]))

        nb = S // STRIDE_CMP
        kcmp, vcmp = _compress(
            kc.reshape(B, N_KV, nb, half), vc.reshape(B, N_KV, nb, half),
            cmp_pe_k[i].reshape(2, half), cmp_pe_v[i].reshape(2, half),
            cmp_wk1[i].reshape(2, half, -1).astype(BF16), row(cmp_bk1[i]),
            cmp_wk2[i].astype(BF16), row(cmp_bk2[i]),
            cmp_wv1[i].reshape(2, half, -1).astype(BF16), row(cmp_bv1[i]),
            cmp_wv2[i].astype(BF16), row(cmp_bv2[i]))

        att = _nsa(q, kcmp, vcmp, ovlt, ks, vs, kw, vw, gates)

        wo = w_out[i].astype(BF16)
        h2, u, route, counts = _out_proj(
            h.reshape(N, D), rec.reshape(N, D_REC), att.reshape(N, D_ATT), row(att_out_norm[i]),
            wo[:D_REC], wo[D_REC:], row(ffn_norm[i]), w_route.astype(BF16), b_route)

        buf_tok, chunk_e, dest0, dest1 = _dispatch(route, counts, N)
        n_chunks = chunk_e.shape[0]
        pieces = MOE_PIECES if n_chunks % MOE_PIECES == 0 else 1
        rows = n_chunks // pieces * EXPERT_CHUNK
        y = None
        for k in range(pieces):
            y = _experts(chunk_e[k * (n_chunks // pieces):(k + 1) * (n_chunks // pieces)],
                         u.at[buf_tok[k * rows:(k + 1) * rows]].get(mode="promise_in_bounds"),
                         w_gate_exp, w_up_exp, w_down_exp, i, y, k, pieces)
        last = i == depth - 1
        ple_pieces = PLE_PIECES if N % (PLE_PIECES * ROW_TILE) == 0 else 1
        rows = N // ple_pieces
        h = None
        for k in range(ple_pieces):
            sl = slice(k * rows, (k + 1) * rows)
            h = _ple(h2, y.at[dest0[sl]].get(mode="promise_in_bounds"),
                     y.at[dest1[sl]].get(mode="promise_in_bounds"), route, p_flat, i,
                     row(ple_norm[i]), w_ple_gate[i].astype(BF16), row(b_ple_gate[i]),
                     w_ple_up[i].astype(BF16), row(final_norm), last, h, k, ple_pieces)
    return h.reshape(B, S, D)
```

```python
import functools

import jax
import jax.numpy as jnp
from jax import lax
from jax.experimental import pallas as pl
from jax.experimental.pallas import tpu as pltpu

F32 = jnp.float32
BF16 = jnp.bfloat16

D_REC = 512
REC_BLOCKS = 8
CONV_W = 4
LRU_C = 8.0
N_HEADS = 8
HEAD_DIM = 64
N_KV = 2
GQA_R = N_HEADS // N_KV
D_ATT = N_HEADS * HEAD_DIM
KV_W = N_KV * HEAD_DIM
L_CMP = 32
STRIDE_CMP = 16
L_SEL = 64
N_SEL = 16
WINDOW = 512
N_GROUPS = 4
EXP_PER_GROUP = 8
N_EXPERTS = N_GROUPS * EXP_PER_GROUP
EPS = 1e-6
NEG = -1e30
BIG = 1e30

LANES = 128
SUBLANES = 8
MXU_DIM = 256
VMEM_LIMIT = 56 * 1024 * 1024

ROW_TILE = 512
SCAN_TILE = 512
Q_TILE = 256
K_TILE = 512
EXPERT_CHUNK = 512
MOE_PIECES = 8
PLE_PIECES = 4


def _cparams(sem):
    return pltpu.CompilerParams(dimension_semantics=sem, vmem_limit_bytes=VMEM_LIMIT)


def _rms(x, g):
    return x * lax.rsqrt(jnp.mean(x * x, axis=-1, keepdims=True) + EPS) * g


def _dot(a, b):
    return jnp.dot(a, b, preferred_element_type=F32)


def _dot_nt(a, b):
    return lax.dot_general(a, b, (((1,), (1,)), ((), ())), preferred_element_type=F32)


def _full(shape):
    n = len(shape)
    return pl.BlockSpec(shape, lambda *_: (0,) * n)


AUG_W = 2 * LANES


def _sel_block_onehot(pos):
    blk = lax.broadcasted_iota(jnp.int32, (pos.shape[0], HEAD_DIM), 1)
    return jnp.where(blk == pos // L_SEL, 1.0, 0.0).astype(BF16)


def _pos_columns(hi, lo):
    lane = lax.broadcasted_iota(jnp.int32, (hi.shape[0], LANES), 1)
    return jnp.where(lane == 0, hi, jnp.where(lane == 1, lo, 0.0)).astype(BF16)


def _proj_in_kernel(h_ref, g_ref, wa_ref, wq_ref, wkv_ref, wg_ref,
                    xr_ref, gr_ref, q_ref, kc_ref, vc_ref, ks_ref, vs_ref, kw_ref, vw_ref,
                    gate_ref):
    u = _rms(h_ref[0], g_ref[...]).astype(BF16)
    za = _dot(u, wa_ref[...])
    xr_ref[0] = za[:, :D_REC]
    gr_ref[0] = za[:, D_REC:]
    zq = (_dot(u, wq_ref[...]) * (HEAD_DIM ** -0.5)).astype(q_ref.dtype)
    for h in range(N_HEADS):
        q_ref[0, h] = zq[:, h * HEAD_DIM:(h + 1) * HEAD_DIM]
    zkv = _dot(u, wkv_ref[...])
    TM = zkv.shape[0]
    pos = pl.program_id(1) * TM + lax.broadcasted_iota(jnp.int32, (TM, 1), 0)
    blk_onehot = _sel_block_onehot(pos)
    no_onehot = jnp.zeros((TM, HEAD_DIM), BF16)
    ones_col = jnp.where(lax.broadcasted_iota(jnp.int32, (TM, HEAD_DIM), 1) == 0,
                         1.0, 0.0).astype(BF16)
    pos_cols = _pos_columns((pos // L_SEL * L_SEL).astype(F32), (pos % L_SEL).astype(F32))
    for i, ref in enumerate((kc_ref, vc_ref, ks_ref, vs_ref, kw_ref, vw_ref)):
        for g in range(N_KV):
            lo = i * KV_W + g * HEAD_DIM
            piece = zkv[:, lo:lo + HEAD_DIM].astype(ref.dtype)
            if ref is ks_ref:
                piece = jnp.concatenate([piece, blk_onehot, pos_cols], axis=-1)
            elif ref is kw_ref:
                piece = jnp.concatenate([piece, no_onehot, pos_cols], axis=-1)
            elif ref is vs_ref or ref is vw_ref:
                piece = jnp.concatenate([piece, ones_col], axis=-1)
            ref[0, g] = piece
    gate_ref[0] = jax.nn.sigmoid(_dot(u, wg_ref[...]))


def _proj_in(h, g, wa, wq, wkv, wg):
    B, S, D = h.shape
    TM = min(ROW_TILE, S)
    kv_f32 = jax.ShapeDtypeStruct((B, N_KV, S, HEAD_DIM), F32)
    kv_val = jax.ShapeDtypeStruct((B, N_KV, S, LANES), BF16)
    kv_aug = jax.ShapeDtypeStruct((B, N_KV, S, AUG_W), BF16)
    kv_spec = pl.BlockSpec((1, N_KV, TM, HEAD_DIM), lambda b, s: (b, 0, s, 0))
    val_spec = pl.BlockSpec((1, N_KV, TM, LANES), lambda b, s: (b, 0, s, 0))
    aug_spec = pl.BlockSpec((1, N_KV, TM, AUG_W), lambda b, s: (b, 0, s, 0))
    row = lambda w: pl.BlockSpec((1, TM, w), lambda b, s: (b, s, 0))
    return pl.pallas_call(
        _proj_in_kernel,
        grid=(B, S // TM),
        in_specs=[row(D), _full(g.shape), _full(wa.shape), _full(wq.shape), _full(wkv.shape),
                  _full(wg.shape)],
        out_specs=[row(D_REC), row(D_REC),
                   pl.BlockSpec((1, N_HEADS, TM, HEAD_DIM), lambda b, s: (b, 0, s, 0)),
                   kv_spec, kv_spec, aug_spec, val_spec, aug_spec, val_spec, row(LANES)],
        out_shape=[jax.ShapeDtypeStruct((B, S, D_REC), F32),
                   jax.ShapeDtypeStruct((B, S, D_REC), F32),
                   jax.ShapeDtypeStruct((B, N_HEADS, S, HEAD_DIM), BF16),
                   kv_f32, kv_f32, kv_aug, kv_val, kv_aug, kv_val,
                   jax.ShapeDtypeStruct((B, S, LANES), F32)],
        compiler_params=_cparams(("parallel", "parallel")),
        name="proj_in",
    )(h, g, wa, wq, wkv, wg)


def _rglru_kernel(xr_ref, gr_ref, cw_ref, cb_ref, wa_ref, ba_ref, wx_ref, bx_ref, lam_ref,
                  gn_ref, out_ref, ext_ref, a_ref, b_ref, carry_ref):
    TS = xr_ref.shape[1]
    C = xr_ref.shape[2]

    @pl.when(pl.program_id(1) == 0)
    def _():
        ext_ref[0:SUBLANES, :] = jnp.zeros((SUBLANES, C), F32)
        carry_ref[...] = jnp.zeros(carry_ref.shape, F32)

    x = xr_ref[0]
    ext_ref[SUBLANES:, :] = x
    xc = cb_ref[...] + cw_ref[CONV_W - 1:CONV_W, :] * x
    for j in range(CONV_W - 1):
        off = SUBLANES - (CONV_W - 1) + j
        xc = xc + cw_ref[j:j + 1, :] * ext_ref[off:off + TS, :]
    ext_ref[0:SUBLANES, :] = x[TS - SUBLANES:, :]

    xcb = xc.astype(BF16)
    nt = C // MXU_DIM
    ra = jnp.concatenate(
        [_dot(xcb[:, k * MXU_DIM:(k + 1) * MXU_DIM], wa_ref[k]) for k in range(nt)], axis=-1)
    rx = jnp.concatenate(
        [_dot(xcb[:, k * MXU_DIM:(k + 1) * MXU_DIM], wx_ref[k]) for k in range(nt)], axis=-1)
    r = jax.nn.sigmoid(ra + ba_ref[...])
    i = jax.nn.sigmoid(rx + bx_ref[...])
    lam = lam_ref[...]
    softplus_neg_lam = jnp.maximum(-lam, 0.0) + jnp.log1p(jnp.exp(-jnp.abs(lam)))
    log_a = -LRU_C * r * softplus_neg_lam
    a = jnp.exp(log_a)
    one_minus = 1.0 - a * a
    root = jnp.where(one_minus > 0.0, one_minus * lax.rsqrt(one_minus), 0.0)
    a_ref[...] = a
    b_ref[...] = root * (i * xc)

    row = lax.broadcasted_iota(jnp.int32, (SUBLANES, C), 0)

    def group(gi, carry):
        r0 = pl.multiple_of(gi * SUBLANES, SUBLANES)
        a = a_ref[pl.ds(r0, SUBLANES), :]
        b = b_ref[pl.ds(r0, SUBLANES), :]
        d = 1
        while d < SUBLANES:
            keep = row >= d
            a_s = pltpu.roll(a, d, axis=0)
            b_s = pltpu.roll(b, d, axis=0)
            b = jnp.where(keep, a * b_s, 0.0) + b
            a = jnp.where(keep, a * a_s, a)
            d *= 2
        hcur = b + a * carry
        b_ref[pl.ds(r0, SUBLANES), :] = hcur
        return hcur[SUBLANES - 1:SUBLANES, :]

    carry_ref[...] = lax.fori_loop(0, TS // SUBLANES, group, carry_ref[...])

    y = b_ref[...] * jax.nn.gelu(gr_ref[0])
    out_ref[0] = _rms(y, gn_ref[...]).astype(out_ref.dtype)


def _rglru(xr, gr, cw, cb, wa, ba, wx, bx, lam, gn):
    B, S, C = xr.shape
    TS = min(SCAN_TILE, S)
    row = pl.BlockSpec((1, TS, C), lambda b, s: (b, s, 0))
    args = (cw, cb, wa, ba, wx, bx, lam, gn)
    return pl.pallas_call(
        _rglru_kernel,
        grid=(B, S // TS),
        in_specs=[row, row] + [_full(a.shape) for a in args],
        out_specs=row,
        out_shape=jax.ShapeDtypeStruct((B, S, C), BF16),
        scratch_shapes=[pltpu.VMEM((TS + SUBLANES, C), F32), pltpu.VMEM((TS, C), F32),
                        pltpu.VMEM((TS, C), F32), pltpu.VMEM((1, C), F32)],
        compiler_params=_cparams(("parallel", "arbitrary")),
        name="rglru",
    )(xr, gr, *args)


def _compress_kernel(kc_ref, vc_ref, pek_ref, pev_ref, wk1_ref, bk1_ref, wk2_ref, bk2_ref,
                     wv1_ref, bv1_ref, wv2_ref, bv2_ref, ko_ref, vo_ref):
    def one(x_ref, pe_ref, w1_ref, b1_ref, w2_ref, b2_ref, o_ref, augment):
        x = x_ref[0, 0]
        nb = x.shape[0]
        first = _dot((x + pe_ref[0:1, :]).astype(BF16), w1_ref[0])
        second = _dot((x + pe_ref[1:2, :]).astype(BF16), w1_ref[1])
        hid = jax.nn.gelu(first + pltpu.roll(second, nb - 1, axis=0) + b1_ref[...])
        o = (_dot(hid.astype(BF16), w2_ref[...]) + b2_ref[...]).astype(o_ref.dtype)
        if augment:
            n = lax.broadcasted_iota(jnp.int32, (nb, 1), 0)
            per = L_SEL // STRIDE_CMP
            hi = (n // per * L_SEL).astype(F32)
            lo = (n % per * STRIDE_CMP).astype(F32) + (L_CMP - 1) * 0.5
            o = jnp.concatenate([o, jnp.zeros((nb, HEAD_DIM), BF16), _pos_columns(hi, lo)], axis=-1)
        o_ref[0, 0] = o

    one(kc_ref, pek_ref, wk1_ref, bk1_ref, wk2_ref, bk2_ref, ko_ref, True)
    one(vc_ref, pev_ref, wv1_ref, bv1_ref, wv2_ref, bv2_ref, vo_ref, False)


def _compress(kc, vc, pek, pev, wk1, bk1, wk2, bk2, wv1, bv1, wv2, bv2):
    B, G, NB, W = kc.shape
    blk = pl.BlockSpec((1, 1, NB, W), lambda b, g: (b, g, 0, 0))
    oblk = lambda w: pl.BlockSpec((1, 1, NB, w), lambda b, g: (b, g, 0, 0))
    args = (pek, pev, wk1, bk1, wk2, bk2, wv1, bv1, wv2, bv2)
    oshape = lambda w: jax.ShapeDtypeStruct((B, G, NB, w), BF16)
    return pl.pallas_call(
        _compress_kernel,
        grid=(B, G),
        in_specs=[blk, blk] + [_full(a.shape) for a in args],
        out_specs=[oblk(AUG_W), oblk(HEAD_DIM)],
        out_shape=[oshape(AUG_W), oshape(HEAD_DIM)],
        compiler_params=_cparams(("parallel", "parallel")),
        name="compress",
    )(kc, vc, *args)


def _normalized(acc):
    return acc[:, :HEAD_DIM] * (1.0 / acc[:, HEAD_DIM:HEAD_DIM + 1])


def _selection_rank(score_t):
    NS, TQ = score_t.shape
    sub = lax.broadcasted_iota(jnp.int32, (SUBLANES, TQ), 0)
    groups = [score_t[v * SUBLANES:(v + 1) * SUBLANES, :] for v in range(NS // SUBLANES)]
    ranks = [jnp.zeros((SUBLANES, TQ), F32) for _ in groups]
    for j in range(NS):
        sj = score_t[j:j + 1, :]
        for v, sv in enumerate(groups):
            if j < v * SUBLANES:
                ahead = sj >= sv
            elif j >= (v + 1) * SUBLANES:
                ahead = sj > sv
            else:
                ahead = (sj > sv) | ((sj == sv) & (sub > j - v * SUBLANES))
            ranks[v] = ranks[v] + jnp.where(ahead, 1.0, 0.0)
    return jnp.concatenate(ranks, axis=0)


def _nsa_kernel(q_ref, kcmp_ref, vcmp_ref, ovlt_ref, ks_ref, vs_ref, kw_ref, vw_ref, gate_ref,
                out_ref, s_ref, mx_ref, acc_ref, part_ref, gs_ref, tiles_ref, *, n_top):
    TQ = q_ref.shape[2]
    M = GQA_R * TQ
    NC = kcmp_ref.shape[2]
    S = ks_ref.shape[2]
    NS = S // L_SEL
    g = pl.program_id(1)
    t0 = pl.program_id(2) * TQ
    row = lax.broadcasted_iota(jnp.int32, (M, 1), 0)
    t_i = t0 + (row & (TQ - 1))
    slope = jnp.zeros((M, 1), F32)
    for r in range(GQA_R):
        s_r = jnp.where(g == 0, 2.0 ** -(r + 1), 2.0 ** -(GQA_R + r + 1)).astype(F32)
        slope = jnp.where(row // TQ == r, s_r, slope)
    lane = lax.broadcasted_iota(jnp.int32, (M, LANES), 1)
    slope_cols = jnp.where(lane < 2, slope, 0.0).astype(BF16)
    q4 = jnp.concatenate([q_ref[0, r] for r in range(GQA_R)], axis=0)
    qa0 = jnp.concatenate([q4, jnp.zeros((M, HEAD_DIM), BF16), slope_cols], axis=-1)

    WK = WINDOW + TQ
    start = pl.multiple_of(jnp.maximum(t0 - WINDOW, 0), TQ)
    s = _dot_nt(qa0, kw_ref[0, 0, pl.ds(start, WK), :])
    dw = t_i - (start + lax.broadcasted_iota(jnp.int32, (1, WK), 1))
    s = jnp.where(dw.astype(jnp.uint32) < WINDOW, s, NEG)
    e = jnp.exp((s - jnp.max(s, axis=-1, keepdims=True)).astype(BF16))
    o_w = _normalized(_dot(e, vw_ref[0, 0, pl.ds(start, WK), :]))

    n_i = lax.broadcasted_iota(jnp.int32, (1, NC), 1)
    mask_c = (n_i * STRIDE_CMP + (L_CMP - 1)) <= t_i
    s = jnp.where(mask_c, _dot_nt(qa0, kcmp_ref[0, 0]), NEG)
    e = jnp.where(mask_c, jnp.exp(s - jnp.max(s, axis=-1, keepdims=True)), 0.0)
    p = e * (1.0 / jnp.maximum(jnp.sum(e, axis=-1, keepdims=True), 1e-30))
    o_c = _dot(p.astype(BF16), vcmp_ref[0, 0])
    gl = gate_ref[0]
    gsel = jnp.where(g == 0, gl[:, :LANES // 2], gl[:, LANES // 2:])

    def gate_col(c):
        return jnp.concatenate([gsel[:, 3 * r + c:3 * r + c + 1] for r in range(GQA_R)], axis=0)

    part_ref[...] = gate_col(0) * o_c + gate_col(2) * o_w
    gs_ref[...] = jnp.broadcast_to(gate_col(1), (M, HEAD_DIM))
    p_sum = p[0:TQ]
    for r in range(1, GQA_R):
        p_sum = p_sum + p[r * TQ:(r + 1) * TQ]

    imp_t = lax.dot_general(ovlt_ref[...], p_sum, (((1,), (1,)), ((), ())),
                            preferred_element_type=F32, precision=lax.Precision.HIGHEST)
    blk = lax.broadcasted_iota(jnp.int32, (NS, TQ), 0)
    tq = t0 + lax.broadcasted_iota(jnp.int32, (NS, TQ), 1)
    cur = tq // L_SEL
    valid = (blk * L_SEL) <= tq
    forced = valid & ((blk == 0) | (blk == cur) | (blk == cur - 1))
    score_t = jnp.where(forced, BIG, jnp.where(valid, imp_t, NEG))
    unsel_t = jnp.where(_selection_rank(score_t) < n_top, 0.0, 1.0).astype(BF16)
    eye = jnp.where(lax.broadcasted_iota(jnp.int32, (TQ, TQ), 0)
                    == lax.broadcasted_iota(jnp.int32, (TQ, TQ), 1), 1.0, 0.0).astype(BF16)
    bias = (_dot_nt(eye, unsel_t) * NEG).astype(BF16)
    if NS < HEAD_DIM:
        bias = jnp.concatenate([bias, jnp.zeros((TQ, HEAD_DIM - NS), BF16)], axis=-1)
    qa = jnp.concatenate([q4, jnp.concatenate([bias] * GQA_R, axis=0), slope_cols], axis=-1)

    n_lane_tiles = K_TILE // LANES
    jd = t0 // K_TILE

    def lane_fold(x, op):
        out = x[:, 0:LANES]
        for c in range(1, n_lane_tiles):
            out = op(out, x[:, c * LANES:(c + 1) * LANES])
        return out

    def keys(j):
        return ks_ref[0, 0, pl.ds(pl.multiple_of(j * K_TILE, K_TILE), K_TILE), :]

    pos = jd * K_TILE + lax.broadcasted_iota(jnp.int32, (1, K_TILE), 1)
    s = jnp.where(pos <= t_i, _dot_nt(qa, keys(jd)), NEG)
    s_ref[jd] = s
    mx_ref[...] = lane_fold(s, jnp.maximum)

    def two_at_a_time(n, tile_fn):
        def pair(i, c):
            tile_fn(2 * i)
            tile_fn(2 * i + 1)
            return c

        lax.fori_loop(0, n // 2, pair, 0)

        @pl.when(n % 2 == 1)
        def _():
            tile_fn(n - 1)

    blocks_per_tile = K_TILE // L_SEL
    picked = jnp.max(1.0 - unsel_t.astype(F32), axis=1, keepdims=True)
    n_tiles = NS // blocks_per_tile
    n_live = jnp.int32(0)
    for j in range(n_tiles):
        hit = jnp.max(picked[j * blocks_per_tile:(j + 1) * blocks_per_tile, :]) > 0.0
        live = hit & (j < jd)
        tiles_ref[jnp.where(live, n_live, n_tiles)] = j
        n_live = n_live + live.astype(jnp.int32)

    def qk_tile(i):
        j = tiles_ref[i]
        s = _dot_nt(qa, keys(j))
        s_ref[j] = s
        mx_ref[...] = jnp.maximum(mx_ref[...], lane_fold(s, jnp.maximum))

    two_at_a_time(n_live, qk_tile)
    mx_ref[...] = jnp.broadcast_to(jnp.max(mx_ref[...], axis=-1, keepdims=True), (M, LANES))

    def pv(j):
        mb = mx_ref[...]
        p = jnp.exp((s_ref[j] - jnp.concatenate([mb] * n_lane_tiles, axis=-1)).astype(BF16))
        k0 = pl.multiple_of(j * K_TILE, K_TILE)
        return _dot(p, vs_ref[0, 0, pl.ds(k0, K_TILE), :])

    acc_ref[...] = pv(jd)

    def pv_tile(i):
        acc_ref[...] += pv(tiles_ref[i])

    two_at_a_time(n_live, pv_tile)
    o_s = _normalized(acc_ref[...])

    o = part_ref[...] + gs_ref[...] * o_s
    out_ref[0] = jnp.concatenate([o[r * TQ:(r + 1) * TQ] for r in range(GQA_R)],
                                 axis=-1).astype(out_ref.dtype)


def _nsa(q, kcmp, vcmp, ovlt, ks, vs, kw, vw, gates):
    B, _, S, _ = q.shape
    NC = kcmp.shape[2]
    TQ = min(Q_TILE, S)
    assert TQ & (TQ - 1) == 0 and S // L_SEL <= HEAD_DIM and S >= WINDOW + TQ
    n_top = min(N_SEL, S // L_SEL)
    M = GQA_R * TQ
    cmp_spec = lambda w: pl.BlockSpec((1, 1, NC, w), lambda b, g, i: (b, g, 0, 0))
    seq_spec = lambda w: pl.BlockSpec((1, 1, S, w), lambda b, g, i: (b, g, 0, 0))
    return pl.pallas_call(
        functools.partial(_nsa_kernel, n_top=n_top),
        grid=(B, N_KV, S // TQ),
        in_specs=[pl.BlockSpec((1, GQA_R, TQ, HEAD_DIM), lambda b, g, i: (b, g, i, 0)),
                  cmp_spec(AUG_W), cmp_spec(HEAD_DIM), _full(ovlt.shape),
                  seq_spec(AUG_W), seq_spec(LANES), seq_spec(AUG_W), seq_spec(LANES),
                  pl.BlockSpec((1, TQ, LANES), lambda b, g, i: (b, i, 0))],
        out_specs=pl.BlockSpec((1, TQ, GQA_R * HEAD_DIM), lambda b, g, i: (b, i, g)),
        out_shape=jax.ShapeDtypeStruct((B, S, D_ATT), F32),
        scratch_shapes=[pltpu.VMEM((pl.cdiv(S, K_TILE), M, K_TILE), F32),
                        pltpu.VMEM((M, LANES), F32), pltpu.VMEM((M, LANES), F32),
                        pltpu.VMEM((M, HEAD_DIM), F32), pltpu.VMEM((M, HEAD_DIM), F32),
                        pltpu.SMEM((pl.cdiv(S, K_TILE) + 1,), jnp.int32)],
        compiler_params=_cparams(("parallel", "parallel", "arbitrary")),
        name="nsa",
    )(q, kcmp, vcmp, ovlt, ks, vs, kw, vw, gates)


def _out_proj_kernel(h_ref, rec_ref, att_ref, an_ref, wor_ref, woa_ref, fn_ref, wr_ref, br_ref,
                     h2_ref, u_ref, route_ref, count_ref):
    att = _rms(att_ref[...], an_ref[...]).astype(BF16)
    h2 = h_ref[...] + _dot(rec_ref[...], wor_ref[...]) + _dot(att, woa_ref[...])
    h2_ref[...] = h2
    u = _rms(h2, fn_ref[...])
    ub = u.astype(BF16)
    u_ref[...] = ub
    lg = _dot(ub, wr_ref[...]) + br_ref[...]
    lane = lax.broadcasted_iota(jnp.int32, lg.shape, 1)
    is_grp = lane < N_GROUPS
    gl = jnp.where(is_grp, lg, NEG)
    gm = jnp.max(gl, axis=-1, keepdims=True)
    p_g = 1.0 / jnp.sum(jnp.where(is_grp, jnp.exp(gl - gm), 0.0), axis=-1, keepdims=True)
    g_idx = jnp.min(jnp.where(gl == gm, lane, LANES), axis=-1, keepdims=True)
    lo = N_GROUPS + EXP_PER_GROUP * g_idx
    in_g = (lane >= lo) & (lane < lo + EXP_PER_GROUP)
    el = jnp.where(in_g, lg, NEG)
    v1 = jnp.max(el, axis=-1, keepdims=True)
    i1 = jnp.min(jnp.where(in_g & (el == v1), lane, LANES), axis=-1, keepdims=True)
    rest = in_g & (lane != i1)
    el2 = jnp.where(rest, lg, NEG)
    v2 = jnp.max(el2, axis=-1, keepdims=True)
    i2 = jnp.min(jnp.where(rest & (el2 == v2), lane, LANES), axis=-1, keepdims=True)
    e2 = jnp.exp(v2 - v1)
    den = 1.0 + e2
    route = jnp.where(lane == 0, (i1 - N_GROUPS).astype(F32),
                      jnp.where(lane == 1, (i2 - N_GROUPS).astype(F32),
                                jnp.where(lane == 2, p_g * (1.0 / den),
                                          jnp.where(lane == 3, p_g * (e2 / den), 0.0))))
    route_ref[...] = route
    chosen = jnp.where((lane == i1 - N_GROUPS) | (lane == i2 - N_GROUPS), 1.0, 0.0)
    count_ref[0] = jnp.broadcast_to(jnp.sum(chosen, axis=0, keepdims=True), (SUBLANES, LANES))


def _out_proj(h, rec, att, an, wor, woa, fn, wr, br):
    N, D = h.shape
    TM = min(ROW_TILE, N)
    row = lambda w: pl.BlockSpec((TM, w), lambda i: (i, 0))
    args = (an, wor, woa, fn, wr, br)
    return pl.pallas_call(
        _out_proj_kernel,
        grid=(N // TM,),
        in_specs=[row(D), row(D_REC), row(D_ATT)] + [_full(a.shape) for a in args],
        out_specs=[row(D), row(D), row(LANES),
                   pl.BlockSpec((1, SUBLANES, LANES), lambda i: (i, 0, 0))],
        out_shape=[jax.ShapeDtypeStruct((N, D), F32), jax.ShapeDtypeStruct((N, D), BF16),
                   jax.ShapeDtypeStruct((N, LANES), F32),
                   jax.ShapeDtypeStruct((N // TM, SUBLANES, LANES), F32)],
        compiler_params=_cparams(("parallel",)),
        name="out_proj",
    )(h, rec, att, *args)


def _expert_kernel(ce_ref, xs_ref, wg_ref, wu_ref, wd_ref, *rest):
    y_ref, wgb, wub, wdb = rest[-4:]
    c = pl.program_id(0)
    prev = ce_ref[jnp.maximum(c - 1, 0)]

    @pl.when((c == 0) | (ce_ref[c] != prev))
    def _():
        wgb[...] = wg_ref[0, 0].astype(BF16)
        wub[...] = wu_ref[0, 0].astype(BF16)
        wdb[...] = wd_ref[0, 0].astype(BF16)

    xs = xs_ref[...]
    hdn = jax.nn.silu(_dot(xs, wgb[...])) * _dot(xs, wub[...])
    y_ref[...] = _dot(hdn.astype(BF16), wdb[...]).astype(y_ref.dtype)


def _experts(chunk_e, xs, wg, wu, wd, layer, y_prev, piece, n_pieces):
    P, D = xs.shape
    DE = wg.shape[-1]
    C = EXPERT_CHUNK
    first = piece * (P // C)
    in_specs = [pl.BlockSpec((C, D), lambda c, ce: (c, 0)),
                pl.BlockSpec((1, 1, D, DE), lambda c, ce: (layer, ce[c], 0, 0)),
                pl.BlockSpec((1, 1, D, DE), lambda c, ce: (layer, ce[c], 0, 0)),
                pl.BlockSpec((1, 1, DE, D), lambda c, ce: (layer, ce[c], 0, 0))]
    args = [chunk_e, xs, wg, wu, wd]
    aliases = {}
    if y_prev is not None:
        in_specs.append(pl.BlockSpec(memory_space=pl.ANY))
        args.append(y_prev)
        aliases = {len(args) - 1: 0}
    grid_spec = pltpu.PrefetchScalarGridSpec(
        num_scalar_prefetch=1,
        grid=(P // C,),
        in_specs=in_specs,
        out_specs=pl.BlockSpec((C, D), lambda c, ce: (first + c, 0)),
        scratch_shapes=[pltpu.VMEM((D, DE), BF16), pltpu.VMEM((D, DE), BF16),
                        pltpu.VMEM((DE, D), BF16)],
    )
    return pl.pallas_call(
        _expert_kernel,
        grid_spec=grid_spec,
        out_shape=jax.ShapeDtypeStruct((n_pieces * P, D), BF16),
        input_output_aliases=aliases,
        compiler_params=_cparams(("arbitrary",)),
        name="experts",
    )(*args)


def _slot_kernel(route_ref, count_ref, dest_ref, ends_ref, run_ref):
    TM = route_ref.shape[0]

    @pl.when(pl.program_id(0) == 0)
    def _():
        counts = jnp.sum(count_ref[...], axis=0).astype(jnp.int32)
        padded = ((counts + (EXPERT_CHUNK - 1)) // EXPERT_CHUNK * EXPERT_CHUNK).astype(F32)
        below = jnp.where(lax.broadcasted_iota(jnp.int32, (LANES, LANES), 0)
                          < lax.broadcasted_iota(jnp.int32, (LANES, LANES), 1), 1.0, 0.0)
        starts = jnp.dot(padded, below, preferred_element_type=F32,
                         precision=lax.Precision.HIGHEST)
        ends_ref[...] = starts + padded
        run_ref[...] = starts[0:1]

    x = route_ref[...]
    lane = lax.broadcasted_iota(jnp.int32, (TM, LANES), 1).astype(F32)
    hot0 = jnp.where(lane == x[:, 0:1], 1.0, 0.0)
    hot1 = jnp.where(lane == x[:, 1:2], 1.0, 0.0)
    tot0 = jnp.sum(hot0, axis=0, keepdims=True)
    earlier = jnp.where(lax.broadcasted_iota(jnp.int32, (TM, TM), 0)
                        > lax.broadcasted_iota(jnp.int32, (TM, TM), 1), 1.0, 0.0).astype(BF16)
    before0 = _dot(earlier, hot0.astype(BF16))
    before1 = _dot(earlier, hot1.astype(BF16))
    run = run_ref[...]
    d0 = jnp.sum(hot0 * (before0 + run), axis=-1, keepdims=True)
    d1 = jnp.sum(hot1 * (before1 + (run + tot0)), axis=-1, keepdims=True)
    run_ref[...] = run + tot0 + jnp.sum(hot1, axis=0, keepdims=True)
    both = jnp.where(lane == 0.0, d0, jnp.where(lane == 1.0, d1, 0.0))
    pick = jnp.where(lax.broadcasted_iota(jnp.int32, (SUBLANES, LANES), 0)
                     == lax.broadcasted_iota(jnp.int32, (SUBLANES, LANES), 1), 1.0, 0.0)
    dest_ref[...] = lax.dot_general(pick, both, (((1,), (1,)), ((), ())),
                                    preferred_element_type=F32,
                                    precision=lax.Precision.HIGHEST)


def _slots(route, counts):
    N = route.shape[0]
    TM = min(ROW_TILE, N)
    return pl.pallas_call(
        _slot_kernel,
        grid=(N // TM,),
        in_specs=[pl.BlockSpec((TM, LANES), lambda i: (i, 0)), _full(counts.shape)],
        out_specs=[pl.BlockSpec((SUBLANES, TM), lambda i: (0, i)),
                   pl.BlockSpec((SUBLANES, LANES), lambda i: (0, 0))],
        out_shape=[jax.ShapeDtypeStruct((SUBLANES, N), F32),
                   jax.ShapeDtypeStruct((SUBLANES, LANES), F32)],
        scratch_shapes=[pltpu.VMEM((1, LANES), F32)],
        compiler_params=_cparams(("arbitrary",)),
        name="slots",
    )(route, counts)


def _dispatch(route, counts, n_tok):
    C = EXPERT_CHUNK
    dest, ends = _slots(route, counts)
    dest0 = dest[0].astype(jnp.int32)
    dest1 = dest[1].astype(jnp.int32)
    n_chunks = -(-2 * n_tok // C) + N_EXPERTS
    tok = jnp.arange(n_tok, dtype=jnp.int32)
    buf_tok = jnp.zeros((n_chunks * C,), jnp.int32).at[jnp.concatenate([dest0, dest1])].set(
        jnp.concatenate([tok, tok]), unique_indices=True)
    chunk_start = jnp.arange(n_chunks, dtype=jnp.int32) * C
    seg_end = ends[0, :N_EXPERTS].astype(jnp.int32)
    chunk_e = jnp.minimum(jnp.sum(seg_end[None, :] <= chunk_start[:, None], axis=1),
                          N_EXPERTS - 1).astype(jnp.int32)
    return buf_tok, chunk_e, dest0, dest1


def _ple_kernel(h_ref, y0_ref, y1_ref, route_ref, p_ref, pn_ref, wg_ref, bg_ref, wu_ref,
                fnorm_ref, o_ref, *, final):
    r = route_ref[...]
    h3 = h_ref[...] + (r[:, 2:3] * y0_ref[...].astype(F32) + r[:, 3:4] * y1_ref[...].astype(F32))
    gate = jax.nn.sigmoid(_dot(_rms(h3, pn_ref[...]).astype(BF16), wg_ref[...]) + bg_ref[...])
    h4 = h3 + _dot(p_ref[0].astype(BF16), wu_ref[...]) * gate
    o_ref[...] = _rms(h4, fnorm_ref[...]) if final else h4


def _ple_kernel_piece(h_ref, y0_ref, y1_ref, route_ref, p_ref, pn_ref, wg_ref, bg_ref, wu_ref,
                      fnorm_ref, *rest, final):
    _ple_kernel(h_ref, y0_ref, y1_ref, route_ref, p_ref, pn_ref, wg_ref, bg_ref, wu_ref,
                fnorm_ref, rest[-1], final=final)


def _ple(h, y0, y1, route, p, layer, pn, wg, bg, wu, fnorm, final, out_prev, piece, n_pieces):
    N, D = h.shape
    TM = min(ROW_TILE, N)
    steps = N // n_pieces // TM
    first = piece * steps
    row = lambda w: pl.BlockSpec((TM, w), lambda i: (first + i, 0))
    local = pl.BlockSpec((TM, D), lambda i: (i, 0))
    consts = (pn, wg, bg, wu, fnorm)
    in_specs = [row(D), local, local, row(LANES),
                pl.BlockSpec((1, TM, p.shape[2]), lambda i: (layer, first + i, 0))]
    in_specs += [_full(a.shape) for a in consts]
    args = [h, y0, y1, route, p, *consts]
    aliases = {}
    if out_prev is not None:
        in_specs.append(pl.BlockSpec(memory_space=pl.ANY))
        args.append(out_prev)
        aliases = {len(args) - 1: 0}
    return pl.pallas_call(
        functools.partial(_ple_kernel_piece, final=final),
        grid=(steps,),
        in_specs=in_specs,
        out_specs=row(D),
        out_shape=jax.ShapeDtypeStruct((N, D), F32),
        input_output_aliases=aliases,
        compiler_params=_cparams(("parallel",)),
        name="ple",
    )(*args)


def _block_diag_tiles(w):
    nb, k, _ = w.shape
    per = MXU_DIM // k
    tiles = jnp.zeros((nb // per, MXU_DIM, MXU_DIM), w.dtype)
    for b in range(nb):
        t, o = divmod(b, per)
        tiles = tiles.at[t, o * k:(o + 1) * k, o * k:(o + 1) * k].set(w[b])
    return tiles


def _overlap(S):
    n = jnp.arange(S // STRIDE_CMP) * STRIDE_CMP
    m = jnp.arange(S // L_SEL) * L_SEL
    ov = jnp.clip(jnp.minimum(n[:, None] + L_CMP, m[None, :] + L_SEL)
                  - jnp.maximum(n[:, None], m[None, :]), 0)
    return ov.astype(F32) / STRIDE_CMP


def kernel(x, p, mix_norm, w_in, conv_w, conv_b, lru_wa, lru_ba, lru_wx, lru_bx, lru_lambda, cmp_pe_k, cmp_pe_v, cmp_wk1, cmp_bk1, cmp_wk2, cmp_bk2, cmp_wv1, cmp_bv1, cmp_wv2, cmp_bv2, rec_out_norm, att_out_norm, w_out, ffn_norm, w_group_router, b_group_router, w_expert_router, b_expert_router, w_gate_exp, w_up_exp, w_down_exp, ple_norm, w_ple_gate, b_ple_gate, w_ple_up, final_norm):
    B, S, D = x.shape
    depth = w_in.shape[0]
    N = B * S
    half = L_CMP // 2 * HEAD_DIM
    row = lambda v: v.reshape(1, -1)
    ovlt = _overlap(S).T
    p_flat = p.reshape(depth, N, -1)
    o_q = 2 * D_REC
    o_kv = o_q + D_ATT
    o_g = o_kv + 6 * KV_W
    h = x
    for i in range(depth):
        wi = w_in[i]
        wgate = jnp.zeros((D, LANES), F32)
        for g in range(N_KV):
            wgate = wgate.at[:, g * (LANES // 2):g * (LANES // 2) + 3 * GQA_R].set(
                wi[:, o_g + g * 3 * GQA_R:o_g + (g + 1) * 3 * GQA_R])
        w_route = jnp.zeros((D, LANES), F32)
        w_route = w_route.at[:, :N_GROUPS].set(w_group_router[i])
        w_route = w_route.at[:, N_GROUPS:N_GROUPS + N_EXPERTS].set(w_expert_router[i])
        b_route = jnp.zeros((1, LANES), F32)
        b_route = b_route.at[0, :N_GROUPS].set(b_group_router[i])
        b_route = b_route.at[0, N_GROUPS:N_GROUPS + N_EXPERTS].set(b_expert_router[i])

        xr, gr, q, kc, vc, ks, vs, kw, vw, gates = _proj_in(
            h.reshape(B, S, D), row(mix_norm[i]), wi[:, :o_q].astype(BF16),
            wi[:, o_q:o_kv].astype(BF16), wi[:, o_kv:o_g].astype(BF16), wgate.astype(BF16))

        rec = _rglru(xr, gr, conv_w[i], row(conv_b[i]),
                     _block_diag_tiles(lru_wa[i]).astype(BF16), row(lru_ba[i]),
                     _block_diag_tiles(lru_wx[i]).astype(BF16), row(lru_bx[i]),
                     row(lru_lambda[i]), row(rec_out_norm[i]))

        nb = S // STRIDE_CMP
        kcmp, vcmp = _compress(
            kc.reshape(B, N_KV, nb, half), vc.reshape(B, N_KV, nb, half),
            cmp_pe_k[i].reshape(2, half), cmp_pe_v[i].reshape(2, half),
            cmp_wk1[i].reshape(2, half, -1).astype(BF16), row(cmp_bk1[i]),
            cmp_wk2[i].astype(BF16), row(cmp_bk2[i]),
            cmp_wv1[i].reshape(2, half, -1).astype(BF16), row(cmp_bv1[i]),
            cmp_wv2[i].astype(BF16), row(cmp_bv2[i]))

        att = _nsa(q, kcmp, vcmp, ovlt, ks, vs, kw, vw, gates)

        wo = w_out[i].astype(BF16)
        h2, u, route, counts = _out_proj(
            h.reshape(N, D), rec.reshape(N, D_REC), att.reshape(N, D_ATT), row(att_out_norm[i]),
            wo[:D_REC], wo[D_REC:], row(ffn_norm[i]), w_route.astype(BF16), b_route)

        buf_tok, chunk_e, dest0, dest1 = _dispatch(route, counts, N)
        n_chunks = chunk_e.shape[0]
        pieces = MOE_PIECES if n_chunks % MOE_PIECES == 0 else 1
        rows = n_chunks // pieces * EXPERT_CHUNK
        y = None
        for k in range(pieces):
            y = _experts(chunk_e[k * (n_chunks // pieces):(k + 1) * (n_chunks // pieces)],
                         u.at[buf_tok[k * rows:(k + 1) * rows]].get(mode="promise_in_bounds"),
                         w_gate_exp, w_up_exp, w_down_exp, i, y, k, pieces)
        last = i == depth - 1
        ple_pieces = PLE_PIECES if N % (PLE_PIECES * ROW_TILE) == 0 else 1
        rows = N // ple_pieces
        h = None
        for k in range(ple_pieces):
            sl = slice(k * rows, (k + 1) * rows)
            h = _ple(h2, y.at[dest0[sl]].get(mode="promise_in_bounds"),
                     y.at[dest1[sl]].get(mode="promise_in_bounds"), route, p_flat, i,
                     row(ple_norm[i]), w_ple_gate[i].astype(BF16), row(b_ple_gate[i]),
                     w_ple_up[i].astype(BF16), row(final_norm), last, h, k, ple_pieces)
    return h.reshape(B, S, D)
```

```python
import functools

import jax
import jax.numpy as jnp
from jax import lax
from jax.experimental import pallas as pl
from jax.experimental.pallas import tpu as pltpu

F32 = jnp.float32
BF16 = jnp.bfloat16

D_REC = 512
REC_BLOCKS = 8
CONV_W = 4
LRU_C = 8.0
N_HEADS = 8
HEAD_DIM = 64
N_KV = 2
GQA_R = N_HEADS // N_KV
D_ATT = N_HEADS * HEAD_DIM
KV_W = N_KV * HEAD_DIM
L_CMP = 32
STRIDE_CMP = 16
L_SEL = 64
N_SEL = 16
WINDOW = 512
N_GROUPS = 4
EXP_PER_GROUP = 8
N_EXPERTS = N_GROUPS * EXP_PER_GROUP
EPS = 1e-6
NEG = -1e30
BIG = 1e30

LANES = 128
SUBLANES = 8
MXU_DIM = 256
VMEM_LIMIT = 56 * 1024 * 1024

ROW_TILE = 1024
SCAN_TILE = 512
Q_TILE = 256
K_TILE = 512
EXPERT_CHUNK = 512
MOE_PIECES = 8
PLE_PIECES = 4


def _cparams(sem):
    return pltpu.CompilerParams(dimension_semantics=sem, vmem_limit_bytes=VMEM_LIMIT)


def _rms(x, g):
    return x * lax.rsqrt(jnp.mean(x * x, axis=-1, keepdims=True) + EPS) * g


def _dot(a, b):
    return jnp.dot(a, b, preferred_element_type=F32)


def _dot_nt(a, b):
    return lax.dot_general(a, b, (((1,), (1,)), ((), ())), preferred_element_type=F32)


def _full(shape):
    n = len(shape)
    return pl.BlockSpec(shape, lambda *_: (0,) * n)


AUG_W = 2 * LANES


def _sel_block_onehot(pos):
    blk = lax.broadcasted_iota(jnp.int32, (pos.shape[0], HEAD_DIM), 1)
    return jnp.where(blk == pos // L_SEL, 1.0, 0.0).astype(BF16)


def _pos_columns(hi, lo):
    lane = lax.broadcasted_iota(jnp.int32, (hi.shape[0], LANES), 1)
    return jnp.where(lane == 0, hi, jnp.where(lane == 1, lo, 0.0)).astype(BF16)


def _proj_in_kernel(h_ref, g_ref, wa_ref, wq_ref, wkv_ref, wg_ref,
                    xr_ref, gr_ref, q_ref, kc_ref, vc_ref, ks_ref, vs_ref, kw_ref, vw_ref,
                    gate_ref):
    u = _rms(h_ref[0], g_ref[...]).astype(BF16)
    za = _dot(u, wa_ref[...])
    xr_ref[0] = za[:, :D_REC]
    gr_ref[0] = za[:, D_REC:]
    zq = (_dot(u, wq_ref[...]) * (HEAD_DIM ** -0.5)).astype(q_ref.dtype)
    for h in range(N_HEADS):
        q_ref[0, h] = zq[:, h * HEAD_DIM:(h + 1) * HEAD_DIM]
    zkv = _dot(u, wkv_ref[...])
    TM = zkv.shape[0]
    pos = pl.program_id(1) * TM + lax.broadcasted_iota(jnp.int32, (TM, 1), 0)
    blk_onehot = _sel_block_onehot(pos)
    no_onehot = jnp.zeros((TM, HEAD_DIM), BF16)
    ones_col = jnp.where(lax.broadcasted_iota(jnp.int32, (TM, HEAD_DIM), 1) == 0,
                         1.0, 0.0).astype(BF16)
    pos_cols = _pos_columns((pos // L_SEL * L_SEL).astype(F32), (pos % L_SEL).astype(F32))
    for i, ref in enumerate((kc_ref, vc_ref, ks_ref, vs_ref, kw_ref, vw_ref)):
        for g in range(N_KV):
            lo = i * KV_W + g * HEAD_DIM
            piece = zkv[:, lo:lo + HEAD_DIM].astype(ref.dtype)
            if ref is ks_ref:
                piece = jnp.concatenate([piece, blk_onehot, pos_cols], axis=-1)
            elif ref is kw_ref:
                piece = jnp.concatenate([piece, no_onehot, pos_cols], axis=-1)
            elif ref is vs_ref or ref is vw_ref:
                piece = jnp.concatenate([piece, ones_col], axis=-1)
            ref[0, g] = piece
    gate_ref[0] = jax.nn.sigmoid(_dot(u, wg_ref[...]))


def _proj_in(h, g, wa, wq, wkv, wg):
    B, S, D = h.shape
    TM = min(ROW_TILE, S)
    kv_f32 = jax.ShapeDtypeStruct((B, N_KV, S, HEAD_DIM), F32)
    kv_val = jax.ShapeDtypeStruct((B, N_KV, S, LANES), BF16)
    kv_aug = jax.ShapeDtypeStruct((B, N_KV, S, AUG_W), BF16)
    kv_spec = pl.BlockSpec((1, N_KV, TM, HEAD_DIM), lambda b, s: (b, 0, s, 0))
    val_spec = pl.BlockSpec((1, N_KV, TM, LANES), lambda b, s: (b, 0, s, 0))
    aug_spec = pl.BlockSpec((1, N_KV, TM, AUG_W), lambda b, s: (b, 0, s, 0))
    row = lambda w: pl.BlockSpec((1, TM, w), lambda b, s: (b, s, 0))
    return pl.pallas_call(
        _proj_in_kernel,
        grid=(B, S // TM),
        in_specs=[row(D), _full(g.shape), _full(wa.shape), _full(wq.shape), _full(wkv.shape),
                  _full(wg.shape)],
        out_specs=[row(D_REC), row(D_REC),
                   pl.BlockSpec((1, N_HEADS, TM, HEAD_DIM), lambda b, s: (b, 0, s, 0)),
                   kv_spec, kv_spec, aug_spec, val_spec, aug_spec, val_spec, row(LANES)],
        out_shape=[jax.ShapeDtypeStruct((B, S, D_REC), F32),
                   jax.ShapeDtypeStruct((B, S, D_REC), F32),
                   jax.ShapeDtypeStruct((B, N_HEADS, S, HEAD_DIM), BF16),
                   kv_f32, kv_f32, kv_aug, kv_val, kv_aug, kv_val,
                   jax.ShapeDtypeStruct((B, S, LANES), F32)],
        compiler_params=_cparams(("parallel", "parallel")),
        name="proj_in",
    )(h, g, wa, wq, wkv, wg)


def _rglru_kernel(xr_ref, gr_ref, cw_ref, cb_ref, wa_ref, ba_ref, wx_ref, bx_ref, lam_ref,
                  gn_ref, out_ref, ext_ref, a_ref, b_ref, carry_ref):
    TS = xr_ref.shape[1]
    C = xr_ref.shape[2]

    @pl.when(pl.program_id(1) == 0)
    def _():
        ext_ref[0:SUBLANES, :] = jnp.zeros((SUBLANES, C), F32)
        carry_ref[...] = jnp.zeros(carry_ref.shape, F32)

    x = xr_ref[0]
    ext_ref[SUBLANES:, :] = x
    xc = cb_ref[...] + cw_ref[CONV_W - 1:CONV_W, :] * x
    for j in range(CONV_W - 1):
        off = SUBLANES - (CONV_W - 1) + j
        xc = xc + cw_ref[j:j + 1, :] * ext_ref[off:off + TS, :]
    ext_ref[0:SUBLANES, :] = x[TS - SUBLANES:, :]

    xcb = xc.astype(BF16)
    nt = C // MXU_DIM
    ra = jnp.concatenate(
        [_dot(xcb[:, k * MXU_DIM:(k + 1) * MXU_DIM], wa_ref[k]) for k in range(nt)], axis=-1)
    rx = jnp.concatenate(
        [_dot(xcb[:, k * MXU_DIM:(k + 1) * MXU_DIM], wx_ref[k]) for k in range(nt)], axis=-1)
    r = jax.nn.sigmoid(ra + ba_ref[...])
    i = jax.nn.sigmoid(rx + bx_ref[...])
    lam = lam_ref[...]
    softplus_neg_lam = jnp.maximum(-lam, 0.0) + jnp.log1p(jnp.exp(-jnp.abs(lam)))
    log_a = -LRU_C * r * softplus_neg_lam
    a = jnp.exp(log_a)
    one_minus = 1.0 - a * a
    root = jnp.where(one_minus > 0.0, one_minus * lax.rsqrt(one_minus), 0.0)
    a_ref[...] = a
    b_ref[...] = root * (i * xc)

    row = lax.broadcasted_iota(jnp.int32, (SUBLANES, C), 0)

    def group(gi, carry):
        r0 = pl.multiple_of(gi * SUBLANES, SUBLANES)
        a = a_ref[pl.ds(r0, SUBLANES), :]
        b = b_ref[pl.ds(r0, SUBLANES), :]
        d = 1
        while d < SUBLANES:
            keep = row >= d
            a_s = pltpu.roll(a, d, axis=0)
            b_s = pltpu.roll(b, d, axis=0)
            b = jnp.where(keep, a * b_s, 0.0) + b
            a = jnp.where(keep, a * a_s, a)
            d *= 2
        hcur = b + a * carry
        b_ref[pl.ds(r0, SUBLANES), :] = hcur
        return hcur[SUBLANES - 1:SUBLANES, :]

    carry_ref[...] = lax.fori_loop(0, TS // SUBLANES, group, carry_ref[...])

    y = b_ref[...] * jax.nn.gelu(gr_ref[0])
    out_ref[0] = _rms(y, gn_ref[...]).astype(out_ref.dtype)


def _rglru(xr, gr, cw, cb, wa, ba, wx, bx, lam, gn):
    B, S, C = xr.shape
    TS = min(SCAN_TILE, S)
    row = pl.BlockSpec((1, TS, C), lambda b, s: (b, s, 0))
    args = (cw, cb, wa, ba, wx, bx, lam, gn)
    return pl.pallas_call(
        _rglru_kernel,
        grid=(B, S // TS),
        in_specs=[row, row] + [_full(a.shape) for a in args],
        out_specs=row,
        out_shape=jax.ShapeDtypeStruct((B, S, C), BF16),
        scratch_shapes=[pltpu.VMEM((TS + SUBLANES, C), F32), pltpu.VMEM((TS, C), F32),
                        pltpu.VMEM((TS, C), F32), pltpu.VMEM((1, C), F32)],
        compiler_params=_cparams(("parallel", "arbitrary")),
        name="rglru",
    )(xr, gr, *args)


def _compress_kernel(kc_ref, vc_ref, pek_ref, pev_ref, wk1_ref, bk1_ref, wk2_ref, bk2_ref,
                     wv1_ref, bv1_ref, wv2_ref, bv2_ref, ko_ref, vo_ref):
    def one(x_ref, pe_ref, w1_ref, b1_ref, w2_ref, b2_ref, o_ref, augment):
        x = x_ref[0, 0]
        nb = x.shape[0]
        first = _dot((x + pe_ref[0:1, :]).astype(BF16), w1_ref[0])
        second = _dot((x + pe_ref[1:2, :]).astype(BF16), w1_ref[1])
        hid = jax.nn.gelu(first + pltpu.roll(second, nb - 1, axis=0) + b1_ref[...])
        o = (_dot(hid.astype(BF16), w2_ref[...]) + b2_ref[...]).astype(o_ref.dtype)
        if augment:
            n = lax.broadcasted_iota(jnp.int32, (nb, 1), 0)
            per = L_SEL // STRIDE_CMP
            hi = (n // per * L_SEL).astype(F32)
            lo = (n % per * STRIDE_CMP).astype(F32) + (L_CMP - 1) * 0.5
            o = jnp.concatenate([o, jnp.zeros((nb, HEAD_DIM), BF16), _pos_columns(hi, lo)], axis=-1)
        o_ref[0, 0] = o

    one(kc_ref, pek_ref, wk1_ref, bk1_ref, wk2_ref, bk2_ref, ko_ref, True)
    one(vc_ref, pev_ref, wv1_ref, bv1_ref, wv2_ref, bv2_ref, vo_ref, False)


def _compress(kc, vc, pek, pev, wk1, bk1, wk2, bk2, wv1, bv1, wv2, bv2):
    B, G, NB, W = kc.shape
    blk = pl.BlockSpec((1, 1, NB, W), lambda b, g: (b, g, 0, 0))
    oblk = lambda w: pl.BlockSpec((1, 1, NB, w), lambda b, g: (b, g, 0, 0))
    args = (pek, pev, wk1, bk1, wk2, bk2, wv1, bv1, wv2, bv2)
    oshape = lambda w: jax.ShapeDtypeStruct((B, G, NB, w), BF16)
    return pl.pallas_call(
        _compress_kernel,
        grid=(B, G),
        in_specs=[blk, blk] + [_full(a.shape) for a in args],
        out_specs=[oblk(AUG_W), oblk(HEAD_DIM)],
        out_shape=[oshape(AUG_W), oshape(HEAD_DIM)],
        compiler_params=_cparams(("parallel", "parallel")),
        name="compress",
    )(kc, vc, *args)


def _normalized(acc):
    return acc[:, :HEAD_DIM] * (1.0 / acc[:, HEAD_DIM:HEAD_DIM + 1])


def _selection_rank(score_t):
    NS, TQ = score_t.shape
    sub = lax.broadcasted_iota(jnp.int32, (SUBLANES, TQ), 0)
    groups = [score_t[v * SUBLANES:(v + 1) * SUBLANES, :] for v in range(NS // SUBLANES)]
    ranks = [jnp.zeros((SUBLANES, TQ), F32) for _ in groups]
    for j in range(NS):
        sj = score_t[j:j + 1, :]
        for v, sv in enumerate(groups):
            if j < v * SUBLANES:
                ahead = sj >= sv
            elif j >= (v + 1) * SUBLANES:
                ahead = sj > sv
            else:
                ahead = (sj > sv) | ((sj == sv) & (sub > j - v * SUBLANES))
            ranks[v] = ranks[v] + jnp.where(ahead, 1.0, 0.0)
    return jnp.concatenate(ranks, axis=0)


def _nsa_kernel(q_ref, kcmp_ref, vcmp_ref, ovlt_ref, ks_ref, vs_ref, kw_ref, vw_ref, gate_ref,
                out_ref, s_ref, mx_ref, acc_ref, part_ref, gs_ref, tiles_ref, *, n_top):
    TQ = q_ref.shape[2]
    M = GQA_R * TQ
    NC = kcmp_ref.shape[2]
    S = ks_ref.shape[2]
    NS = S // L_SEL
    g = pl.program_id(1)
    t0 = pl.program_id(2) * TQ
    row = lax.broadcasted_iota(jnp.int32, (M, 1), 0)
    t_i = t0 + (row & (TQ - 1))
    slope = jnp.zeros((M, 1), F32)
    for r in range(GQA_R):
        s_r = jnp.where(g == 0, 2.0 ** -(r + 1), 2.0 ** -(GQA_R + r + 1)).astype(F32)
        slope = jnp.where(row // TQ == r, s_r, slope)
    lane = lax.broadcasted_iota(jnp.int32, (M, LANES), 1)
    slope_cols = jnp.where(lane < 2, slope, 0.0).astype(BF16)
    q4 = jnp.concatenate([q_ref[0, r] for r in range(GQA_R)], axis=0)
    qa0 = jnp.concatenate([q4, jnp.zeros((M, HEAD_DIM), BF16), slope_cols], axis=-1)

    WK = WINDOW + TQ
    start = pl.multiple_of(jnp.maximum(t0 - WINDOW, 0), TQ)
    s = _dot_nt(qa0, kw_ref[0, 0, pl.ds(start, WK), :])
    dw = t_i - (start + lax.broadcasted_iota(jnp.int32, (1, WK), 1))
    s = jnp.where(dw.astype(jnp.uint32) < WINDOW, s, NEG)
    e = jnp.exp((s - jnp.max(s, axis=-1, keepdims=True)).astype(BF16))
    o_w = _normalized(_dot(e, vw_ref[0, 0, pl.ds(start, WK), :]))

    n_i = lax.broadcasted_iota(jnp.int32, (1, NC), 1)
    mask_c = (n_i * STRIDE_CMP + (L_CMP - 1)) <= t_i
    s = jnp.where(mask_c, _dot_nt(qa0, kcmp_ref[0, 0]), NEG)
    e = jnp.where(mask_c, jnp.exp(s - jnp.max(s, axis=-1, keepdims=True)), 0.0)
    p = e * (1.0 / jnp.maximum(jnp.sum(e, axis=-1, keepdims=True), 1e-30))
    o_c = _dot(p.astype(BF16), vcmp_ref[0, 0])
    gl = gate_ref[0]
    gsel = jnp.where(g == 0, gl[:, :LANES // 2], gl[:, LANES // 2:])

    def gate_col(c):
        return jnp.concatenate([gsel[:, 3 * r + c:3 * r + c + 1] for r in range(GQA_R)], axis=0)

    part_ref[...] = gate_col(0) * o_c + gate_col(2) * o_w
    gs_ref[...] = jnp.broadcast_to(gate_col(1), (M, HEAD_DIM))
    p_sum = p[0:TQ]
    for r in range(1, GQA_R):
        p_sum = p_sum + p[r * TQ:(r + 1) * TQ]

    imp_t = lax.dot_general(ovlt_ref[...], p_sum, (((1,), (1,)), ((), ())),
                            preferred_element_type=F32, precision=lax.Precision.HIGHEST)
    blk = lax.broadcasted_iota(jnp.int32, (NS, TQ), 0)
    tq = t0 + lax.broadcasted_iota(jnp.int32, (NS, TQ), 1)
    cur = tq // L_SEL
    valid = (blk * L_SEL) <= tq
    forced = valid & ((blk == 0) | (blk == cur) | (blk == cur - 1))
    score_t = jnp.where(forced, BIG, jnp.where(valid, imp_t, NEG))
    unsel_t = jnp.where(_selection_rank(score_t) < n_top, 0.0, 1.0).astype(BF16)
    eye = jnp.where(lax.broadcasted_iota(jnp.int32, (TQ, TQ), 0)
                    == lax.broadcasted_iota(jnp.int32, (TQ, TQ), 1), 1.0, 0.0).astype(BF16)
    bias = (_dot_nt(eye, unsel_t) * NEG).astype(BF16)
    if NS < HEAD_DIM:
        bias = jnp.concatenate([bias, jnp.zeros((TQ, HEAD_DIM - NS), BF16)], axis=-1)
    qa = jnp.concatenate([q4, jnp.concatenate([bias] * GQA_R, axis=0), slope_cols], axis=-1)

    n_lane_tiles = K_TILE // LANES
    jd = t0 // K_TILE

    def lane_fold(x, op):
        out = x[:, 0:LANES]
        for c in range(1, n_lane_tiles):
            out = op(out, x[:, c * LANES:(c + 1) * LANES])
        return out

    def keys(j):
        return ks_ref[0, 0, pl.ds(pl.multiple_of(j * K_TILE, K_TILE), K_TILE), :]

    pos = jd * K_TILE + lax.broadcasted_iota(jnp.int32, (1, K_TILE), 1)
    s = jnp.where(pos <= t_i, _dot_nt(qa, keys(jd)), NEG)
    s_ref[jd] = s
    mx_ref[...] = lane_fold(s, jnp.maximum)

    def two_at_a_time(n, tile_fn):
        def pair(i, c):
            tile_fn(2 * i)
            tile_fn(2 * i + 1)
            return c

        lax.fori_loop(0, n // 2, pair, 0)

        @pl.when(n % 2 == 1)
        def _():
            tile_fn(n - 1)

    blocks_per_tile = K_TILE // L_SEL
    picked = jnp.max(1.0 - unsel_t.astype(F32), axis=1, keepdims=True)
    n_tiles = NS // blocks_per_tile
    n_live = jnp.int32(0)
    for j in range(n_tiles):
        hit = jnp.max(picked[j * blocks_per_tile:(j + 1) * blocks_per_tile, :]) > 0.0
        live = hit & (j < jd)
        tiles_ref[jnp.where(live, n_live, n_tiles)] = j
        n_live = n_live + live.astype(jnp.int32)

    def qk_tile(i):
        j = tiles_ref[i]
        s = _dot_nt(qa, keys(j))
        s_ref[j] = s
        mx_ref[...] = jnp.maximum(mx_ref[...], lane_fold(s, jnp.maximum))

    two_at_a_time(n_live, qk_tile)
    mx_ref[...] = jnp.broadcast_to(jnp.max(mx_ref[...], axis=-1, keepdims=True), (M, LANES))

    def pv(j):
        mb = mx_ref[...]
        p = jnp.exp((s_ref[j] - jnp.concatenate([mb] * n_lane_tiles, axis=-1)).astype(BF16))
        k0 = pl.multiple_of(j * K_TILE, K_TILE)
        return _dot(p, vs_ref[0, 0, pl.ds(k0, K_TILE), :])

    acc_ref[...] = pv(jd)

    def pv_tile(i):
        acc_ref[...] += pv(tiles_ref[i])

    two_at_a_time(n_live, pv_tile)
    o_s = _normalized(acc_ref[...])

    o = part_ref[...] + gs_ref[...] * o_s
    out_ref[0] = jnp.concatenate([o[r * TQ:(r + 1) * TQ] for r in range(GQA_R)],
                                 axis=-1).astype(out_ref.dtype)


def _nsa(q, kcmp, vcmp, ovlt, ks, vs, kw, vw, gates):
    B, _, S, _ = q.shape
    NC = kcmp.shape[2]
    TQ = min(Q_TILE, S)
    assert TQ & (TQ - 1) == 0 and S // L_SEL <= HEAD_DIM and S >= WINDOW + TQ
    n_top = min(N_SEL, S // L_SEL)
    M = GQA_R * TQ
    cmp_spec = lambda w: pl.BlockSpec((1, 1, NC, w), lambda b, g, i: (b, g, 0, 0))
    seq_spec = lambda w: pl.BlockSpec((1, 1, S, w), lambda b, g, i: (b, g, 0, 0))
    return pl.pallas_call(
        functools.partial(_nsa_kernel, n_top=n_top),
        grid=(B, N_KV, S // TQ),
        in_specs=[pl.BlockSpec((1, GQA_R, TQ, HEAD_DIM), lambda b, g, i: (b, g, i, 0)),
                  cmp_spec(AUG_W), cmp_spec(HEAD_DIM), _full(ovlt.shape),
                  seq_spec(AUG_W), seq_spec(LANES), seq_spec(AUG_W), seq_spec(LANES),
                  pl.BlockSpec((1, TQ, LANES), lambda b, g, i: (b, i, 0))],
        out_specs=pl.BlockSpec((1, TQ, GQA_R * HEAD_DIM), lambda b, g, i: (b, i, g)),
        out_shape=jax.ShapeDtypeStruct((B, S, D_ATT), F32),
        scratch_shapes=[pltpu.VMEM((pl.cdiv(S, K_TILE), M, K_TILE), F32),
                        pltpu.VMEM((M, LANES), F32), pltpu.VMEM((M, LANES), F32),
                        pltpu.VMEM((M, HEAD_DIM), F32), pltpu.VMEM((M, HEAD_DIM), F32),
                        pltpu.SMEM((pl.cdiv(S, K_TILE) + 1,), jnp.int32)],
        compiler_params=_cparams(("parallel", "parallel", "arbitrary")),
        name="nsa",
    )(q, kcmp, vcmp, ovlt, ks, vs, kw, vw, gates)


def _out_proj_kernel(h_ref, rec_ref, att_ref, an_ref, wor_ref, woa_ref, fn_ref, wr_ref, br_ref,
                     h2_ref, u_ref, route_ref, count_ref):
    att = _rms(att_ref[...], an_ref[...]).astype(BF16)
    h2 = h_ref[...] + _dot(rec_ref[...], wor_ref[...]) + _dot(att, woa_ref[...])
    h2_ref[...] = h2
    u = _rms(h2, fn_ref[...])
    ub = u.astype(BF16)
    u_ref[...] = ub
    lg = _dot(ub, wr_ref[...]) + br_ref[...]
    lane = lax.broadcasted_iota(jnp.int32, lg.shape, 1)
    is_grp = lane < N_GROUPS
    gl = jnp.where(is_grp, lg, NEG)
    gm = jnp.max(gl, axis=-1, keepdims=True)
    p_g = 1.0 / jnp.sum(jnp.where(is_grp, jnp.exp(gl - gm), 0.0), axis=-1, keepdims=True)
    g_idx = jnp.min(jnp.where(gl == gm, lane, LANES), axis=-1, keepdims=True)
    lo = N_GROUPS + EXP_PER_GROUP * g_idx
    in_g = (lane >= lo) & (lane < lo + EXP_PER_GROUP)
    el = jnp.where(in_g, lg, NEG)
    v1 = jnp.max(el, axis=-1, keepdims=True)
    i1 = jnp.min(jnp.where(in_g & (el == v1), lane, LANES), axis=-1, keepdims=True)
    rest = in_g & (lane != i1)
    el2 = jnp.where(rest, lg, NEG)
    v2 = jnp.max(el2, axis=-1, keepdims=True)
    i2 = jnp.min(jnp.where(rest & (el2 == v2), lane, LANES), axis=-1, keepdims=True)
    e2 = jnp.exp(v2 - v1)
    den = 1.0 + e2
    route = jnp.where(lane == 0, (i1 - N_GROUPS).astype(F32),
                      jnp.where(lane == 1, (i2 - N_GROUPS).astype(F32),
                                jnp.where(lane == 2, p_g * (1.0 / den),
                                          jnp.where(lane == 3, p_g * (e2 / den), 0.0))))
    route_ref[...] = route
    chosen = jnp.where((lane == i1 - N_GROUPS) | (lane == i2 - N_GROUPS), 1.0, 0.0)
    count_ref[0] = jnp.broadcast_to(jnp.sum(chosen, axis=0, keepdims=True), (SUBLANES, LANES))


def _out_proj(h, rec, att, an, wor, woa, fn, wr, br):
    N, D = h.shape
    TM = min(ROW_TILE, N)
    row = lambda w: pl.BlockSpec((TM, w), lambda i: (i, 0))
    args = (an, wor, woa, fn, wr, br)
    return pl.pallas_call(
        _out_proj_kernel,
        grid=(N // TM,),
        in_specs=[row(D), row(D_REC), row(D_ATT)] + [_full(a.shape) for a in args],
        out_specs=[row(D), row(D), row(LANES),
                   pl.BlockSpec((1, SUBLANES, LANES), lambda i: (i, 0, 0))],
        out_shape=[jax.ShapeDtypeStruct((N, D), F32), jax.ShapeDtypeStruct((N, D), BF16),
                   jax.ShapeDtypeStruct((N, LANES), F32),
                   jax.ShapeDtypeStruct((N // TM, SUBLANES, LANES), F32)],
        compiler_params=_cparams(("parallel",)),
        name="out_proj",
    )(h, rec, att, *args)


def _expert_kernel(ce_ref, xs_ref, wg_ref, wu_ref, wd_ref, *rest):
    y_ref, wgb, wub, wdb = rest[-4:]
    c = pl.program_id(0)
    prev = ce_ref[jnp.maximum(c - 1, 0)]

    @pl.when((c == 0) | (ce_ref[c] != prev))
    def _():
        wgb[...] = wg_ref[0, 0].astype(BF16)
        wub[...] = wu_ref[0, 0].astype(BF16)
        wdb[...] = wd_ref[0, 0].astype(BF16)

    xs = xs_ref[...]
    hdn = jax.nn.silu(_dot(xs, wgb[...])) * _dot(xs, wub[...])
    y_ref[...] = _dot(hdn.astype(BF16), wdb[...]).astype(y_ref.dtype)


def _experts(chunk_e, xs, wg, wu, wd, layer, y_prev, piece, n_pieces):
    P, D = xs.shape
    DE = wg.shape[-1]
    C = EXPERT_CHUNK
    first = piece * (P // C)
    in_specs = [pl.BlockSpec((C, D), lambda c, ce: (c, 0)),
                pl.BlockSpec((1, 1, D, DE), lambda c, ce: (layer, ce[c], 0, 0)),
                pl.BlockSpec((1, 1, D, DE), lambda c, ce: (layer, ce[c], 0, 0)),
                pl.BlockSpec((1, 1, DE, D), lambda c, ce: (layer, ce[c], 0, 0))]
    args = [chunk_e, xs, wg, wu, wd]
    aliases = {}
    if y_prev is not None:
        in_specs.append(pl.BlockSpec(memory_space=pl.ANY))
        args.append(y_prev)
        aliases = {len(args) - 1: 0}
    grid_spec = pltpu.PrefetchScalarGridSpec(
        num_scalar_prefetch=1,
        grid=(P // C,),
        in_specs=in_specs,
        out_specs=pl.BlockSpec((C, D), lambda c, ce: (first + c, 0)),
        scratch_shapes=[pltpu.VMEM((D, DE), BF16), pltpu.VMEM((D, DE), BF16),
                        pltpu.VMEM((DE, D), BF16)],
    )
    return pl.pallas_call(
        _expert_kernel,
        grid_spec=grid_spec,
        out_shape=jax.ShapeDtypeStruct((n_pieces * P, D), BF16),
        input_output_aliases=aliases,
        compiler_params=_cparams(("arbitrary",)),
        name="experts",
    )(*args)


def _slot_kernel(route_ref, count_ref, dest_ref, ends_ref, run_ref):
    TM = route_ref.shape[0]

    @pl.when(pl.program_id(0) == 0)
    def _():
        counts = jnp.sum(count_ref[...], axis=0).astype(jnp.int32)
        padded = ((counts + (EXPERT_CHUNK - 1)) // EXPERT_CHUNK * EXPERT_CHUNK).astype(F32)
        below = jnp.where(lax.broadcasted_iota(jnp.int32, (LANES, LANES), 0)
                          < lax.broadcasted_iota(jnp.int32, (LANES, LANES), 1), 1.0, 0.0)
        starts = jnp.dot(padded, below, preferred_element_type=F32,
                         precision=lax.Precision.HIGHEST)
        ends_ref[...] = starts + padded
        run_ref[...] = starts[0:1]

    x = route_ref[...]
    lane = lax.broadcasted_iota(jnp.int32, (TM, LANES), 1).astype(F32)
    hot0 = jnp.where(lane == x[:, 0:1], 1.0, 0.0)
    hot1 = jnp.where(lane == x[:, 1:2], 1.0, 0.0)
    tot0 = jnp.sum(hot0, axis=0, keepdims=True)
    earlier = jnp.where(lax.broadcasted_iota(jnp.int32, (TM, TM), 0)
                        > lax.broadcasted_iota(jnp.int32, (TM, TM), 1), 1.0, 0.0).astype(BF16)
    before0 = _dot(earlier, hot0.astype(BF16))
    before1 = _dot(earlier, hot1.astype(BF16))
    run = run_ref[...]
    d0 = jnp.sum(hot0 * (before0 + run), axis=-1, keepdims=True)
    d1 = jnp.sum(hot1 * (before1 + (run + tot0)), axis=-1, keepdims=True)
    run_ref[...] = run + tot0 + jnp.sum(hot1, axis=0, keepdims=True)
    both = jnp.where(lane == 0.0, d0, jnp.where(lane == 1.0, d1, 0.0))
    pick = jnp.where(lax.broadcasted_iota(jnp.int32, (SUBLANES, LANES), 0)
                     == lax.broadcasted_iota(jnp.int32, (SUBLANES, LANES), 1), 1.0, 0.0)
    dest_ref[...] = lax.dot_general(pick, both, (((1,), (1,)), ((), ())),
                                    preferred_element_type=F32,
                                    precision=lax.Precision.HIGHEST)


def _slots(route, counts):
    N = route.shape[0]
    TM = min(ROW_TILE, N)
    return pl.pallas_call(
        _slot_kernel,
        grid=(N // TM,),
        in_specs=[pl.BlockSpec((TM, LANES), lambda i: (i, 0)), _full(counts.shape)],
        out_specs=[pl.BlockSpec((SUBLANES, TM), lambda i: (0, i)),
                   pl.BlockSpec((SUBLANES, LANES), lambda i: (0, 0))],
        out_shape=[jax.ShapeDtypeStruct((SUBLANES, N), F32),
                   jax.ShapeDtypeStruct((SUBLANES, LANES), F32)],
        scratch_shapes=[pltpu.VMEM((1, LANES), F32)],
        compiler_params=_cparams(("arbitrary",)),
        name="slots",
    )(route, counts)


def _dispatch(route, counts, n_tok):
    C = EXPERT_CHUNK
    dest, ends = _slots(route, counts)
    dest0 = dest[0].astype(jnp.int32)
    dest1 = dest[1].astype(jnp.int32)
    n_chunks = -(-2 * n_tok // C) + N_EXPERTS
    tok = jnp.arange(n_tok, dtype=jnp.int32)
    buf_tok = jnp.zeros((n_chunks * C,), jnp.int32).at[jnp.concatenate([dest0, dest1])].set(
        jnp.concatenate([tok, tok]), unique_indices=True)
    chunk_start = jnp.arange(n_chunks, dtype=jnp.int32) * C
    seg_end = ends[0, :N_EXPERTS].astype(jnp.int32)
    chunk_e = jnp.minimum(jnp.sum(seg_end[None, :] <= chunk_start[:, None], axis=1),
                          N_EXPERTS - 1).astype(jnp.int32)
    return buf_tok, chunk_e, dest0, dest1


def _ple_kernel(h_ref, y0_ref, y1_ref, route_ref, p_ref, pn_ref, wg_ref, bg_ref, wu_ref,
                fnorm_ref, o_ref, *, final):
    r = route_ref[...]
    h3 = h_ref[...] + (r[:, 2:3] * y0_ref[...].astype(F32) + r[:, 3:4] * y1_ref[...].astype(F32))
    gate = jax.nn.sigmoid(_dot(_rms(h3, pn_ref[...]).astype(BF16), wg_ref[...]) + bg_ref[...])
    h4 = h3 + _dot(p_ref[0].astype(BF16), wu_ref[...]) * gate
    o_ref[...] = _rms(h4, fnorm_ref[...]) if final else h4


def _ple_kernel_piece(h_ref, y0_ref, y1_ref, route_ref, p_ref, pn_ref, wg_ref, bg_ref, wu_ref,
                      fnorm_ref, *rest, final):
    _ple_kernel(h_ref, y0_ref, y1_ref, route_ref, p_ref, pn_ref, wg_ref, bg_ref, wu_ref,
                fnorm_ref, rest[-1], final=final)


def _ple(h, y0, y1, route, p, layer, pn, wg, bg, wu, fnorm, final, out_prev, piece, n_pieces):
    N, D = h.shape
    TM = min(ROW_TILE, N)
    steps = N // n_pieces // TM
    first = piece * steps
    row = lambda w: pl.BlockSpec((TM, w), lambda i: (first + i, 0))
    local = pl.BlockSpec((TM, D), lambda i: (i, 0))
    consts = (pn, wg, bg, wu, fnorm)
    in_specs = [row(D), local, local, row(LANES),
                pl.BlockSpec((1, TM, p.shape[2]), lambda i: (layer, first + i, 0))]
    in_specs += [_full(a.shape) for a in consts]
    args = [h, y0, y1, route, p, *consts]
    aliases = {}
    if out_prev is not None:
        in_specs.append(pl.BlockSpec(memory_space=pl.ANY))
        args.append(out_prev)
        aliases = {len(args) - 1: 0}
    return pl.pallas_call(
        functools.partial(_ple_kernel_piece, final=final),
        grid=(steps,),
        in_specs=in_specs,
        out_specs=row(D),
        out_shape=jax.ShapeDtypeStruct((N, D), F32),
        input_output_aliases=aliases,
        compiler_params=_cparams(("parallel",)),
        name="ple",
    )(*args)


def _block_diag_tiles(w):
    nb, k, _ = w.shape
    per = MXU_DIM // k
    tiles = jnp.zeros((nb // per, MXU_DIM, MXU_DIM), w.dtype)
    for b in range(nb):
        t, o = divmod(b, per)
        tiles = tiles.at[t, o * k:(o + 1) * k, o * k:(o + 1) * k].set(w[b])
    return tiles


def _overlap(S):
    n = jnp.arange(S // STRIDE_CMP) * STRIDE_CMP
    m = jnp.arange(S // L_SEL) * L_SEL
    ov = jnp.clip(jnp.minimum(n[:, None] + L_CMP, m[None, :] + L_SEL)
                  - jnp.maximum(n[:, None], m[None, :]), 0)
    return ov.astype(F32) / STRIDE_CMP


def kernel(x, p, mix_norm, w_in, conv_w, conv_b, lru_wa, lru_ba, lru_wx, lru_bx, lru_lambda, cmp_pe_k, cmp_pe_v, cmp_wk1, cmp_bk1, cmp_wk2, cmp_bk2, cmp_wv1, cmp_bv1, cmp_wv2, cmp_bv2, rec_out_norm, att_out_norm, w_out, ffn_norm, w_group_router, b_group_router, w_expert_router, b_expert_router, w_gate_exp, w_up_exp, w_down_exp, ple_norm, w_ple_gate, b_ple_gate, w_ple_up, final_norm):
    B, S, D = x.shape
    depth = w_in.shape[0]
    N = B * S
    half = L_CMP // 2 * HEAD_DIM
    row = lambda v: v.reshape(1, -1)
    ovlt = _overlap(S).T
    p_flat = p.reshape(depth, N, -1)
    o_q = 2 * D_REC
    o_kv = o_q + D_ATT
    o_g = o_kv + 6 * KV_W
    h = x
    for i in range(depth):
        wi = w_in[i]
        wgate = jnp.zeros((D, LANES), F32)
        for g in range(N_KV):
            wgate = wgate.at[:, g * (LANES // 2):g * (LANES // 2) + 3 * GQA_R].set(
                wi[:, o_g + g * 3 * GQA_R:o_g + (g + 1) * 3 * GQA_R])
        w_route = jnp.zeros((D, LANES), F32)
        w_route = w_route.at[:, :N_GROUPS].set(w_group_router[i])
        w_route = w_route.at[:, N_GROUPS:N_GROUPS + N_EXPERTS].set(w_expert_router[i])
        b_route = jnp.zeros((1, LANES), F32)
        b_route = b_route.at[0, :N_GROUPS].set(b_group_router[i])
        b_route = b_route.at[0, N_GROUPS:N_GROUPS + N_EXPERTS].set(b_expert_router[i])

        xr, gr, q, kc, vc, ks, vs, kw, vw, gates = _proj_in(
            h.reshape(B, S, D), row(mix_norm[i]), wi[:, :o_q].astype(BF16),
            wi[:, o_q:o_kv].astype(BF16), wi[:, o_kv:o_g].astype(BF16), wgate.astype(BF16))

        rec = _rglru(xr, gr, conv_w[i], row(conv_b[i]),
                     _block_diag_tiles(lru_wa[i]).astype(BF16), row(lru_ba[i]),
                     _block_diag_tiles(lru_wx[i]).astype(BF16), row(lru_bx[i]),
                     row(lru_lambda[i]), row(rec_out_norm[i]))

        nb = S // STRIDE_CMP
        kcmp, vcmp = _compress(
            kc.reshape(B, N_KV, nb, half), vc.reshape(B, N_KV, nb, half),
            cmp_pe_k[i].reshape(2, half), cmp_pe_v[i].reshape(2, half),
            cmp_wk1[i].reshape(2, half, -1).astype(BF16), row(cmp_bk1[i]),
            cmp_wk2[i].astype(BF16), row(cmp_bk2[i]),
            cmp_wv1[i].reshape(2, half, -1).astype(BF16), row(cmp_bv1[i]),
            cmp_wv2[i].astype(BF16), row(cmp_bv2[i]))

        att = _nsa(q, kcmp, vcmp, ovlt, ks, vs, kw, vw, gates)

        wo = w_out[i].astype(BF16)
        h2, u, route, counts = _out_proj(
            h.reshape(N, D), rec.reshape(N, D_REC), att.reshape(N, D_ATT), row(att_out_norm[i]),
            wo[:D_REC], wo[D_REC:], row(ffn_norm[i]), w_route.astype(BF16), b_route)

        buf_tok, chunk_e, dest0, dest1 = _dispatch(route, counts, N)
        n_chunks = chunk_e.shape[0]
        pieces = MOE_PIECES if n_chunks % MOE_PIECES == 0 else 1
        rows = n_chunks // pieces * EXPERT_CHUNK
        y = None
        for k in range(pieces):
            y = _experts(chunk_e[k * (n_chunks // pieces):(k + 1) * (n_chunks // pieces)],
                         u.at[buf_tok[k * rows:(k + 1) * rows]].get(mode="promise_in_bounds"),
                         w_gate_exp, w_up_exp, w_down_exp, i, y, k, pieces)
        last = i == depth - 1
        ple_pieces = PLE_PIECES if N % (PLE_PIECES * ROW_TILE) == 0 else 1
        rows = N // ple_pieces
        h = None
        for k in range(ple_pieces):
            sl = slice(k * rows, (k + 1) * rows)
            h = _ple(h2, y.at[dest0[sl]].get(mode="promise_in_bounds"),
                     y.at[dest1[sl]].get(mode="promise_in_bounds"), route, p_flat, i,
                     row(ple_norm[i]), w_ple_gate[i].astype(BF16), row(b_ple_gate[i]),
                     w_ple_up[i].astype(BF16), row(final_norm), last, h, k, ple_pieces)
    return h.reshape(B, S, D)
```
